```python
import math
import jax, jax.numpy as jnp
from jax import lax
import numpy as np

D_MODEL = 1024
BATCH = 32
SEQ = 2048
DEPTH = 2
DEC_BATCH = 32
DEC_SEQ = 64
PAST_LEN = 4096

CHUNK = 64
Q_BLOCK = 128
ROPE_THETA = 10000.0
EPS = 1e-6
NEG_INF = -1e30
HEAD_DIM = D_MODEL // 16
D_FF = 11 * D_MODEL // 4
H_A = 4
Q_LORA = 4 * HEAD_DIM
KV_LORA = 2 * HEAD_DIM
NOPE_A = HEAD_DIM
ROPE_A = HEAD_DIM // 2
V_A = HEAD_DIM
H_B = 4
DH_B = HEAD_DIM
H_IDX = 8
D_IDX = HEAD_DIM
TOPK_MAX = 256
H_C = 4
DH_C = HEAD_DIM
H_D = 4
DQK_D = HEAD_DIM // 2
DV_D = HEAD_DIM
MIX_WIDTH = H_A * V_A + H_B * DH_B + H_C * DH_C + H_D * DV_D
COL_SIZES = (Q_LORA, KV_LORA, ROPE_A,
             H_B * DH_B, DH_B, DH_B, H_IDX * D_IDX, D_IDX, H_IDX,
             H_C * DH_C, H_C * DH_C, H_C * DH_C,
             H_D * 2 * DQK_D, H_D * 2 * DQK_D, H_D * DV_D)
IN_COLS = sum(COL_SIZES)

kernel_name = 'hybrid_streaming_encoder_step'


def _rmsnorm(x, g):
    xf = x.astype(jnp.float32)
    y = xf * lax.rsqrt(jnp.mean(xf * xf, axis=-1, keepdims=True) + EPS)
    return (y * g.astype(jnp.float32)).astype(x.dtype)


def _rope(x, pos):
    d = x.shape[-1]
    half = d // 2
    freqs = ROPE_THETA ** (-jnp.arange(half, dtype=jnp.float32) * 2.0 / d)
    ang = pos.astype(jnp.float32)[:, None] * freqs[None, :]
    shp = (1, pos.shape[0]) + (1,) * (x.ndim - 3) + (half,)
    cos = jnp.cos(ang).reshape(shp)
    sin = jnp.sin(ang).reshape(shp)
    xf = x.astype(jnp.float32)
    x1, x2 = xf[..., :half], xf[..., half:]
    return jnp.concatenate([x1 * cos - x2 * sin, x2 * cos + x1 * sin], axis=-1).astype(x.dtype)


def _swiglu(x, wg, wu, wd):
    return (jax.nn.silu(x @ wg) * (x @ wu)) @ wd


def _chunk_mask(q_pos, k_pos):
    return (k_pos[None, :] // CHUNK) <= (q_pos[:, None] // CHUNK)


def _with_past(past, new):
    return new if past is None else jnp.concatenate([past.astype(new.dtype), new], axis=1)


def _over_query_blocks(fn, q_pos, *q_arrays):
    T = q_pos.shape[0]
    if T <= Q_BLOCK:
        return fn(q_pos, *q_arrays)
    nb = T // Q_BLOCK
    split = lambda a: jnp.moveaxis(a.reshape((a.shape[0], nb, Q_BLOCK) + a.shape[2:]), 1, 0)
    xs = (q_pos.reshape(nb, Q_BLOCK),) + tuple(split(a) for a in q_arrays)
    out = lax.map(lambda args: fn(*args), xs)
    out = jnp.moveaxis(out, 0, 1)
    return out.reshape((out.shape[0], T) + out.shape[3:])


def _softmax_attend(q, k, v, mask):
    s = jnp.einsum('bqhd,bkhd->bhqk', q, k).astype(jnp.float32) * (q.shape[-1] ** -0.5)
    p = jax.nn.softmax(jnp.where(mask[None, None], s, NEG_INF), axis=-1).astype(v.dtype)
    return jnp.einsum('bhqk,bkhd->bqhd', p, v)


def _mla(cq, ckv, kr_raw, past, q_pos, k_pos, q_norm, w_qb, kv_norm, w_kvb):
    B, T = cq.shape[:2]
    q = (_rmsnorm(cq, q_norm) @ w_qb).reshape(B, T, H_A, NOPE_A + ROPE_A)
    q = jnp.concatenate([q[..., :NOPE_A], _rope(q[..., NOPE_A:], q_pos)], axis=-1)
    new_rows = jnp.concatenate([_rmsnorm(ckv, kv_norm), _rope(kr_raw, q_pos)], axis=-1)
    rows = _with_past(past, new_rows)
    L = rows.shape[1]
    kv = (rows[..., :KV_LORA] @ w_kvb).reshape(B, L, H_A, NOPE_A + V_A)
    k = jnp.concatenate([kv[..., :NOPE_A],
                         jnp.broadcast_to(rows[..., None, KV_LORA:], (B, L, H_A, ROPE_A))], axis=-1)
    v = kv[..., NOPE_A:]
    out = _over_query_blocks(lambda qp, qb: _softmax_attend(qb, k, v, _chunk_mask(qp, k_pos)), q_pos, q)
    return out.reshape(B, T, H_A * V_A), new_rows


def _dsa(q, k, v, qi, ki, wi, past, q_pos, k_pos):
    B, T = q.shape[:2]
    q = _rope(q.reshape(B, T, H_B, DH_B), q_pos)
    qi = _rope(qi.reshape(B, T, H_IDX, D_IDX), q_pos)
    wi = wi * (H_IDX ** -0.5)
    new_rows = jnp.concatenate([_rope(k, q_pos), v, _rope(ki, q_pos)], axis=-1)
    rows = _with_past(past, new_rows)
    L = rows.shape[1]
    k_all = rows[..., :DH_B]
    v_all = rows[..., DH_B:2 * DH_B]
    ki_all = rows[..., 2 * DH_B:]
    n_sel = min(TOPK_MAX, L // 4)
    gather = jax.vmap(lambda r, i: r[i])

    def block(qp, qb, qib, wib):
        mask = _chunk_mask(qp, k_pos)
        score = jax.nn.relu(jnp.einsum('bqhd,bkd->bqhk', qib, ki_all).astype(jnp.float32))
        score = jnp.einsum('bqhk,bqh->bqk', score, wib.astype(jnp.float32))
        score = jnp.where(mask[None], score, -jnp.inf)
        _, idx = lax.top_k(score, n_sel)
        valid = (k_pos[idx] // CHUNK) <= (qp[None, :, None] // CHUNK)
        ks = gather(k_all, idx)
        vs = gather(v_all, idx)
        s = jnp.einsum('bqhd,bqkd->bhqk', qb, ks).astype(jnp.float32) * (DH_B ** -0.5)
        p = jax.nn.softmax(jnp.where(valid[:, None], s, NEG_INF), axis=-1).astype(vs.dtype)
        return jnp.einsum('bhqk,bqkd->bqhd', p, vs)

    out = _over_query_blocks(block, q_pos, q, qi, wi)
    return out.reshape(B, T, H_B * DH_B), new_rows


def _stick_breaking(q, k, v, past, q_pos, k_pos):
    B, T = q.shape[:2]
    q = q.reshape(B, T, H_C, DH_C)
    new_rows = jnp.concatenate([k.reshape(B, T, H_C, DH_C), v.reshape(B, T, H_C, DH_C)], axis=-1)
    rows = _with_past(past, new_rows)
    k_all = rows[..., :DH_C]
    v_all = rows[..., DH_C:]

    def block(qp, qb):
        z = jnp.einsum('bqhd,bkhd->bhqk', qb, k_all).astype(jnp.float32) * (DH_C ** -0.5)
        mask = (k_pos[None, :] < qp[:, None])[None, None]
        log_keep = jnp.where(mask, jax.nn.log_sigmoid(-z), 0.0)
        log_after = lax.cumsum(log_keep, axis=3, reverse=True) - log_keep
        a = jnp.where(mask, jnp.exp(jax.nn.log_sigmoid(z) + log_after), 0.0)
        return jnp.einsum('bhqk,bkhd->bqhd', a.astype(v_all.dtype), v_all)

    out = _over_query_blocks(block, q_pos, q)
    return out.reshape(B, T, H_C * DH_C), new_rows


def _diff_attn(q, k, v, past, q_pos, k_pos, lam, sub_norm, lam_init):
    B, T = q.shape[:2]
    q = _rope(q.reshape(B, T, H_D, 2, DQK_D), q_pos)
    k = _rope(k.reshape(B, T, H_D, 2, DQK_D), q_pos).reshape(B, T, H_D, 2 * DQK_D)
    new_rows = jnp.concatenate([k, v.reshape(B, T, H_D, DV_D)], axis=-1)
    rows = _with_past(past, new_rows)
    L = rows.shape[1]
    k_all = rows[..., :2 * DQK_D].reshape(B, L, H_D, 2, DQK_D)
    v_all = rows[..., 2 * DQK_D:]
    lf = lam.astype(jnp.float32)
    lam_full = jnp.exp(jnp.sum(lf[0] * lf[1])) - jnp.exp(jnp.sum(lf[2] * lf[3])) + lam_init

    def block(qp, qb):
        s = jnp.einsum('bqhmd,bkhmd->bhmqk', qb, k_all).astype(jnp.float32) * (DQK_D ** -0.5)
        p = jax.nn.softmax(jnp.where(_chunk_mask(qp, k_pos)[None, None, None], s, NEG_INF), axis=-1)
        w = p[:, :, 0] - lam_full * p[:, :, 1]
        return jnp.einsum('bhqk,bkhd->bqhd', w.astype(v_all.dtype), v_all)

    out = _over_query_blocks(block, q_pos, q)
    out = _rmsnorm(out, sub_norm) * (1.0 - lam_init)
    return out.reshape(B, T, H_D * DV_D), new_rows


def _trunk(x, caches, q_pos, k_pos, params):
    (n1, g1, u1, d1, nm, w_in, qn, wqb, kvn, wkvb, lam, dn, wo, n2, g2, u2, d2, fnorm) = params
    split_at = np.cumsum(COL_SIZES)[:-1].tolist()
    rows_a, rows_b, rows_c, rows_d = [], [], [], []
    for l in range(DEPTH):
        if caches is None:
            pa = pb = pc = pd = None
        else:
            pa, pb, pc, pd = caches[0][l], caches[1][l], caches[2][l], caches[3][l]
        lam_init = 0.8 - 0.6 * math.exp(-0.3 * l)
        x = x + 0.5 * _swiglu(_rmsnorm(x, n1[l]), g1[l], u1[l], d1[l])
        (cq, ckv, kr, qb, kb, vb, qib, kib, wib, qc, kc, vc, qd, kd, vd) = jnp.split(
            _rmsnorm(x, nm[l]) @ w_in[l], split_at, axis=-1)
        oa, ra = _mla(cq, ckv, kr, pa, q_pos, k_pos, qn[l], wqb[l], kvn[l], wkvb[l])
        ob, rb = _dsa(qb, kb, vb, qib, kib, wib, pb, q_pos, k_pos)
        oc, rc = _stick_breaking(qc, kc, vc, pc, q_pos, k_pos)
        od, rd = _diff_attn(qd, kd, vd, pd, q_pos, k_pos, lam[l], dn[l], lam_init)
        x = x + jnp.concatenate([oa, ob, oc, od], axis=-1) @ wo[l]
        x = x + 0.5 * _swiglu(_rmsnorm(x, n2[l]), g2[l], u2[l], d2[l])
        rows_a.append(ra)
        rows_b.append(rb)
        rows_c.append(rc)
        rows_d.append(rd)
    return (_rmsnorm(x, fnorm), jnp.stack(rows_a), jnp.stack(rows_b), jnp.stack(rows_c), jnp.stack(rows_d))


def setup_inputs(seed: int = 0) -> dict:
    key = jax.random.key(seed)
    ks = iter(jax.random.split(key, 32))
    nrm = lambda shape, scale=1.0: scale * jax.random.normal(next(ks), shape, jnp.float32)
    gain = lambda shape: 1.0 + 0.01 * jax.random.normal(next(ks), shape, jnp.float32)
    return {
        'x_prompt': nrm((BATCH, SEQ, D_MODEL)),
        'x_sample': nrm((DEC_BATCH, DEC_SEQ, D_MODEL)),
        'cache_mla': nrm((DEPTH, DEC_BATCH, PAST_LEN, KV_LORA + ROPE_A)),
        'cache_dsa': nrm((DEPTH, DEC_BATCH, PAST_LEN, 2 * DH_B + D_IDX)),
        'cache_sb': nrm((DEPTH, DEC_BATCH, PAST_LEN, H_C, 2 * DH_C)),
        'cache_diff': nrm((DEPTH, DEC_BATCH, PAST_LEN, H_D, 2 * DQK_D + DV_D)),
        'norm_ff1': gain((DEPTH, D_MODEL)),
        'w_ff1_gate': nrm((DEPTH, D_MODEL, D_FF), D_MODEL ** -0.5),
        'w_ff1_up': nrm((DEPTH, D_MODEL, D_FF), D_MODEL ** -0.5),
        'w_ff1_down': nrm((DEPTH, D_FF, D_MODEL), D_FF ** -0.5),
        'norm_mix': gain((DEPTH, D_MODEL)),
        'w_in': nrm((DEPTH, D_MODEL, IN_COLS), D_MODEL ** -0.5),
        'mla_q_norm': gain((DEPTH, Q_LORA)),
        'mla_w_qb': nrm((DEPTH, Q_LORA, H_A * (NOPE_A + ROPE_A)), Q_LORA ** -0.5),
        'mla_kv_norm': gain((DEPTH, KV_LORA)),
        'mla_w_kvb': nrm((DEPTH, KV_LORA, H_A * (NOPE_A + V_A)), KV_LORA ** -0.5),
        'diff_lambda': nrm((DEPTH, 4, DQK_D), 0.1),
        'diff_norm': gain((DEPTH, DV_D)),
        'w_out': nrm((DEPTH, MIX_WIDTH, D_MODEL), MIX_WIDTH ** -0.5),
        'norm_ff2': gain((DEPTH, D_MODEL)),
        'w_ff2_gate': nrm((DEPTH, D_MODEL, D_FF), D_MODEL ** -0.5),
        'w_ff2_up': nrm((DEPTH, D_MODEL, D_FF), D_MODEL ** -0.5),
        'w_ff2_down': nrm((DEPTH, D_FF, D_MODEL), D_FF ** -0.5),
        'final_norm': gain((D_MODEL,)),
    }


def reference(x_prompt, x_sample, cache_mla, cache_dsa, cache_sb, cache_diff,
              norm_ff1, w_ff1_gate, w_ff1_up, w_ff1_down, norm_mix, w_in,
              mla_q_norm, mla_w_qb, mla_kv_norm, mla_w_kvb, diff_lambda, diff_norm, w_out,
              norm_ff2, w_ff2_gate, w_ff2_up, w_ff2_down, final_norm):
    params = (norm_ff1, w_ff1_gate, w_ff1_up, w_ff1_down, norm_mix, w_in,
              mla_q_norm, mla_w_qb, mla_kv_norm, mla_w_kvb, diff_lambda, diff_norm, w_out,
              norm_ff2, w_ff2_gate, w_ff2_up, w_ff2_down, final_norm)
    p_pos = jnp.arange(x_prompt.shape[1], dtype=jnp.int32)
    y_prompt, mla_p, dsa_p, sb_p, diff_p = _trunk(x_prompt, None, p_pos, p_pos, params)
    past_len = cache_mla.shape[2]
    t_new = x_sample.shape[1]
    s_qpos = past_len + jnp.arange(t_new, dtype=jnp.int32)
    s_kpos = jnp.arange(past_len + t_new, dtype=jnp.int32)
    y_sample, mla_s, dsa_s, sb_s, diff_s = _trunk(
        x_sample, (cache_mla, cache_dsa, cache_sb, cache_diff), s_qpos, s_kpos, params)
    return (y_prompt, y_sample, mla_p, dsa_p, sb_p, diff_p, mla_s, dsa_s, sb_s, diff_s)
```

```python
import functools
import math

import numpy as np
import jax
import jax.numpy as jnp
from jax import lax
from jax.experimental import pallas as pl
from jax.experimental.pallas import tpu as pltpu

F32 = jnp.float32
BF16 = jnp.bfloat16

D_MODEL = 1024
CHUNK = 64
ROPE_THETA = 10000.0
EPS = 1e-6
NEG_INF = -1e30
HEAD_DIM = D_MODEL // 16
D_FF = 11 * D_MODEL // 4
H_A = 4
Q_LORA = 4 * HEAD_DIM
KV_LORA = 2 * HEAD_DIM
NOPE_A = HEAD_DIM
ROPE_A = HEAD_DIM // 2
V_A = HEAD_DIM
H_B = 4
DH_B = HEAD_DIM
H_IDX = 8
D_IDX = HEAD_DIM
TOPK_MAX = 256
H_C = 4
DH_C = HEAD_DIM
H_D = 4
DQK_D = HEAD_DIM // 2
DV_D = HEAD_DIM
COL_SIZES = (Q_LORA, KV_LORA, ROPE_A,
             H_B * DH_B, DH_B, DH_B, H_IDX * D_IDX, D_IDX, H_IDX,
             H_C * DH_C, H_C * DH_C, H_C * DH_C,
             H_D * 2 * DQK_D, H_D * 2 * DQK_D, H_D * DV_D)
(O_CQ, O_CKV, O_KR, O_QB, O_KB, O_VB, O_QIB, O_KIB, O_WIB,
 O_QC, O_KC, O_VC, O_QD, O_KD, O_VD) = np.concatenate([[0], np.cumsum(COL_SIZES)[:-1]]).tolist()

LANES = 128
VMEM_LIMIT = 56 * 1024 * 1024
FF_CHUNK = 256
ROW_CHUNK = 512

P_CQ, P_CKV, P_SB, P_QSB, P_WI, P_END = 0, 256, 384, 896, 1408, 1536
R_DSA, R_DIFF, R_KR, R_QDSA, R_QI, R_QDIFF, R_END = 0, 256, 768, 896, 1152, 1664, 2176


def _dot(a, b):
    return jnp.dot(a, b, preferred_element_type=F32)


def _dot_nt(a, b):
    return lax.dot_general(a, b, (((1,), (1,)), ((), ())), preferred_element_type=F32)


def _rms(x, g):
    return x * lax.rsqrt(jnp.mean(x * x, axis=-1, keepdims=True) + EPS) * g


def _const_spec(shape):
    n = len(shape)
    return pl.BlockSpec(shape, lambda *_: (0,) * n)


def _params(*sem):
    return pltpu.CompilerParams(dimension_semantics=sem, vmem_limit_bytes=VMEM_LIMIT)


def _ffn_kernel(*refs, has_mix, has_final):
    it = iter(refs)
    x_ref = next(it)
    if has_mix:
        o_refs = [next(it) for _ in range(4)]
        wo_ref = next(it)
    g_ref, wg_ref, wu_ref, wd_ref = next(it), next(it), next(it), next(it)
    fn_ref = next(it) if has_final else None
    out_ref = next(it)

    x = x_ref[...]
    if has_mix:
        for i, o_ref in enumerate(o_refs):
            x = x + _dot(o_ref[...], wo_ref[256 * i:256 * (i + 1), :])
    xn = _rms(x, g_ref[...]).astype(BF16)
    acc = jnp.zeros_like(x)
    for c0 in range(0, D_FF, FF_CHUNK):
        gate = _dot(xn, wg_ref[:, c0:c0 + FF_CHUNK])
        up = _dot(xn, wu_ref[:, c0:c0 + FF_CHUNK])
        hid = (gate * jax.nn.sigmoid(gate) * up).astype(BF16)
        acc = acc + _dot(hid, wd_ref[c0:c0 + FF_CHUNK, :])
    y = x + 0.5 * acc
    if has_final:
        y = _rms(y, fn_ref[...])
    out_ref[...] = y


def _ffn(x, mix, wo, g, wg, wu, wd, fnorm, tm):
    n = x.shape[0]
    has_mix, has_final = mix is not None, fnorm is not None
    row = lambda w: pl.BlockSpec((tm, w), lambda i: (i, 0))
    args, specs = [x], [row(D_MODEL)]
    if has_mix:
        args += list(mix) + [wo]
        specs += [row(256)] * 4 + [_const_spec(wo.shape)]
    args += [g, wg, wu, wd]
    specs += [_const_spec(g.shape), _const_spec(wg.shape), _const_spec(wu.shape), _const_spec(wd.shape)]
    if has_final:
        args.append(fnorm)
        specs.append(_const_spec(fnorm.shape))
    return pl.pallas_call(
        functools.partial(_ffn_kernel, has_mix=has_mix, has_final=has_final),
        grid=(n // tm,),
        in_specs=specs,
        out_specs=row(D_MODEL),
        out_shape=jax.ShapeDtypeStruct((n, D_MODEL), F32),
        compiler_params=_params("parallel"),
        name="ffn",
    )(*args)


def _proj_kernel(x_ref, g_ref, wp_ref, wr_ref, wrr_ref, cr_ref, sr_ref,
                 qn_ref, wqb_ref, wqbr_ref, cq_ref, sq_ref, kvn_ref,
                 rmla_ref, rdsa_ref, rsb_ref, rdiff_ref,
                 qmla_ref, qdsa_ref, qi_ref, wi_ref, qsb_ref, qdiff_ref):
    xn = _rms(x_ref[...], g_ref[...]).astype(BF16)

    def plain(a, b):
        return _dot(xn, wp_ref[:, a:b])

    def roped(a, b):
        return (_dot(xn, wr_ref[:, a:b]) * cr_ref[:, a:b]
                + _dot(xn, wrr_ref[:, a:b]) * sr_ref[:, a:b])

    cqn = _rms(plain(P_CQ, P_CKV), qn_ref[...]).astype(BF16)
    qmla_ref[...] = (_dot(cqn, wqb_ref[...]) * cq_ref[...]
                     + _dot(cqn, wqbr_ref[...]) * sq_ref[...]).astype(BF16)
    rmla_ref[:, :KV_LORA] = _rms(plain(P_CKV, P_SB), kvn_ref[...])
    rmla_ref[:, KV_LORA:] = roped(R_KR, R_QDSA)[:, :ROPE_A]
    rdsa_ref[...] = roped(R_DSA, R_DIFF)[:, :2 * DH_B + D_IDX]
    qdsa_ref[...] = roped(R_QDSA, R_QI).astype(BF16)
    qi_ref[...] = roped(R_QI, R_QDIFF).astype(BF16)
    wi_ref[...] = plain(P_WI, P_END) * (H_IDX ** -0.5)
    rsb_ref[...] = plain(P_SB, P_QSB)
    qsb_ref[...] = plain(P_QSB, P_WI).astype(BF16)
    rdiff_ref[...] = roped(R_DIFF, R_KR)
    qdiff_ref[...] = roped(R_QDIFF, R_END).astype(BF16)


def _proj(x, lw, tabs, tq):
    b, t, _ = x.shape
    cr, sr, cq, sq = tabs
    tok = lambda w: pl.BlockSpec((None, tq, w), lambda bi, i: (bi, i, 0))
    tab = lambda w: pl.BlockSpec((tq, w), lambda bi, i: (i, 0))
    consts = [lw["nm"], lw["wp"], lw["wr"], lw["wrr"]]
    consts2 = [lw["qn"], lw["wqb"], lw["wqbr"]]
    in_specs = ([tok(D_MODEL)] + [_const_spec(a.shape) for a in consts] + [tab(R_END), tab(R_END)]
                + [_const_spec(a.shape) for a in consts2] + [tab(512), tab(512), _const_spec(lw["kvn"].shape)])
    widths = [(160, F32), (192, F32), (512, F32), (512, F32),
              (512, BF16), (256, BF16), (512, BF16), (LANES, F32), (512, BF16), (512, BF16)]
    return pl.pallas_call(
        _proj_kernel,
        grid=(b, t // tq),
        in_specs=in_specs,
        out_specs=[tok(w) for w, _ in widths],
        out_shape=[jax.ShapeDtypeStruct((b, t, w), dt) for w, dt in widths],
        compiler_params=_params("parallel", "parallel"),
        name="proj",
    )(x, *consts, cr, sr, *consts2, cq, sq, lw["kvn"])


def _chunk_mask(tq):
    ri = lax.broadcasted_iota(jnp.int32, (tq, tq), 0)
    ci = lax.broadcasted_iota(jnp.int32, (tq, tq), 1)
    return (ci // CHUNK) <= (ri // CHUNK)


def _softmax_step(s, v, carry):
    m, l, acc = carry
    m_new = jnp.maximum(m, jnp.max(s, axis=-1, keepdims=True))
    alpha = jnp.exp(m - m_new)
    p = jnp.exp(s - m_new)
    l = alpha * l + jnp.sum(p, axis=-1, keepdims=True)
    acc = alpha * acc + _dot(p.astype(BF16), v)
    return m_new, l, acc


def _softmax_init(tq):
    return (jnp.full((tq, 1), NEG_INF, F32), jnp.zeros((tq, 1), F32), jnp.zeros((tq, LANES), F32))


def _lane_lt(tq, n):
    return lax.broadcasted_iota(jnp.int32, (tq, LANES), 1) < n


def _attn_specs(b_t, tq, q_widths, rows_w, past, layer):
    b, t = b_t
    specs = [pl.BlockSpec((None, tq, w), lambda bi, i: (bi, i, 0)) for w in q_widths]
    specs.append(pl.BlockSpec((None, t, rows_w), lambda bi, i: (bi, 0, 0)))
    if past is not None:
        specs.append(pl.BlockSpec((None, None, past.shape[2], rows_w), lambda bi, i: (layer, bi, 0, 0)))
    return specs


def _out_spec(tq):
    return pl.BlockSpec((None, tq, 256), lambda bi, i: (bi, i, 0))


def _mla_kernel(*refs, p_len, t_len, tq, tk):
    if p_len:
        q_ref, new_ref, past_ref, wk_ref, pk_ref, wv_ref, o_ref, k_scr, v_scr = refs
    else:
        q_ref, new_ref, wk_ref, pk_ref, wv_ref, o_ref, k_scr, v_scr = refs
        past_ref = None
    qi = pl.program_id(1)
    scale = (NOPE_A + ROPE_A) ** -0.5

    @pl.when(qi == 0)
    def _fill():
        def fill(src_ref, dst, n):
            for r0 in range(0, n, ROW_CHUNK):
                rc = min(ROW_CHUNK, n - r0)
                lat = src_ref[r0:r0 + rc, :KV_LORA].astype(BF16)
                rope = src_ref[r0:r0 + rc, KV_LORA:].astype(BF16)
                k_scr[dst + r0:dst + r0 + rc, :] = (_dot(lat, wk_ref[...]) + _dot(rope, pk_ref[...])).astype(BF16)
                v_scr[dst + r0:dst + r0 + rc, :] = _dot(lat, wv_ref[...]).astype(BF16)
        if p_len:
            fill(past_ref, 0, p_len)
        fill(new_ref, p_len, t_len)

    d0 = pl.multiple_of(p_len + qi * tq, tq)
    n_full = (p_len + qi * tq) // tk
    dmask = _chunk_mask(tq)
    low = _lane_lt(tq, V_A)
    for pair in range(H_A // 2):
        outs = []
        for h in (2 * pair, 2 * pair + 1):
            q = q_ref[:, LANES * h:LANES * (h + 1)]
            hs, ps = slice(LANES * h, LANES * (h + 1)), slice(LANES * pair, LANES * (pair + 1))

            def body(kb, carry, q=q, hs=hs, ps=ps):
                r0 = pl.multiple_of(kb * tk, tk)
                s = _dot_nt(q, k_scr[pl.ds(r0, tk), hs]) * scale
                return _softmax_step(s, v_scr[pl.ds(r0, tk), ps], carry)

            carry = lax.fori_loop(0, n_full, body, _softmax_init(tq))
            s = jnp.where(dmask, _dot_nt(q, k_scr[pl.ds(d0, tq), hs]) * scale, NEG_INF)
            _, l, acc = _softmax_step(s, v_scr[pl.ds(d0, tq), ps], carry)
            outs.append(acc / l)
        o_ref[:, LANES * pair:LANES * (pair + 1)] = jnp.where(low, outs[0], outs[1]).astype(BF16)


def _mla(q, rows, past, layer, lw, tq, tk):
    b, t, _ = q.shape
    p_len = 0 if past is None else past.shape[2]
    consts = [lw["wk"], lw["pk"], lw["wv"]]
    args = [q, rows] + ([past] if past is not None else []) + consts
    return pl.pallas_call(
        functools.partial(_mla_kernel, p_len=p_len, t_len=t, tq=tq, tk=tk),
        grid=(b, t // tq),
        in_specs=_attn_specs((b, t), tq, [512], 160, past, layer) + [_const_spec(a.shape) for a in consts],
        out_specs=_out_spec(tq),
        out_shape=jax.ShapeDtypeStruct((b, t, 256), BF16),
        scratch_shapes=[pltpu.VMEM((p_len + t, 512), BF16), pltpu.VMEM((p_len + t, 256), BF16)],
        compiler_params=_params("parallel", "arbitrary"),
        name="mla",
    )(*args)


def _diff_kernel(*refs, p_len, t_len, tq, tk, lam_init):
    if p_len:
        q_ref, new_ref, past_ref, lam_ref, dn_ref, o_ref = refs
    else:
        q_ref, new_ref, lam_ref, dn_ref, o_ref = refs
        past_ref = None
    qi = pl.program_id(1)
    scale = DQK_D ** -0.5
    lam = lam_ref[...]
    lam_full = (jnp.exp(jnp.sum(lam[0:1] * lam[1:2], axis=-1, keepdims=True))
                - jnp.exp(jnp.sum(lam[2:3] * lam[3:4], axis=-1, keepdims=True)) + lam_init)
    dmask = _chunk_mask(tq)
    lane = lax.broadcasted_iota(jnp.int32, (tq, LANES), 1)
    sel1 = jnp.where(lane < DQK_D, 1.0, 0.0).astype(BF16)
    sel2 = jnp.where((lane >= DQK_D) & (lane < 2 * DQK_D), 1.0, 0.0).astype(BF16)
    is_v = lane >= 2 * DQK_D
    n_past = p_len // tk
    n_new = (qi * tq) // tk

    for pair in range(H_D // 2):
        outs = []
        for h in (2 * pair, 2 * pair + 1):
            hs = slice(LANES * h, LANES * (h + 1))
            q1 = q_ref[:, hs] * sel1
            q2 = q_ref[:, hs] * sel2

            def step(kv, mask, carry, q1=q1, q2=q2):
                c1, c2 = carry
                s1 = _dot_nt(q1, kv) * scale
                s2 = _dot_nt(q2, kv) * scale
                if mask is not None:
                    s1 = jnp.where(mask, s1, NEG_INF)
                    s2 = jnp.where(mask, s2, NEG_INF)
                return _softmax_step(s1, kv, c1), _softmax_step(s2, kv, c2)

            def body(src_ref, kb, carry, hs=hs, step=step):
                r0 = pl.multiple_of(kb * tk, tk)
                return step(src_ref[pl.ds(r0, tk), hs].astype(BF16), None, carry)

            carry = (_softmax_init(tq), _softmax_init(tq))
            if p_len:
                carry = lax.fori_loop(0, n_past, functools.partial(body, past_ref), carry)
            carry = lax.fori_loop(0, n_new, functools.partial(body, new_ref), carry)
            d0 = pl.multiple_of(qi * tq, tq)
            (_, l1, a1), (_, l2, a2) = step(new_ref[pl.ds(d0, tq), hs].astype(BF16), dmask, carry)
            o = a1 / l1 - lam_full * (a2 / l2)
            ms = jnp.sum(jnp.where(is_v, o * o, 0.0), axis=-1, keepdims=True) * (1.0 / DV_D)
            outs.append(o * lax.rsqrt(ms + EPS) * dn_ref[...] * (1.0 - lam_init))
        o_ref[:, LANES * pair:LANES * (pair + 1)] = jnp.where(
            is_v, outs[1], pltpu.roll(outs[0], 2 * DQK_D, axis=1)).astype(BF16)


def _diff(q, rows, past, layer, lw, lam_init, tq, tk):
    b, t, _ = q.shape
    p_len = 0 if past is None else past.shape[2]
    consts = [lw["lam"], lw["dn"]]
    args = [q, rows] + ([past] if past is not None else []) + consts
    return pl.pallas_call(
        functools.partial(_diff_kernel, p_len=p_len, t_len=t, tq=tq, tk=tk, lam_init=lam_init),
        grid=(b, t // tq),
        in_specs=_attn_specs((b, t), tq, [512], 512, past, layer) + [_const_spec(a.shape) for a in consts],
        out_specs=_out_spec(tq),
        out_shape=jax.ShapeDtypeStruct((b, t, 256), BF16),
        compiler_params=_params("parallel", "parallel"),
        name="diff",
    )(*args)


def _sb_kernel(*refs, p_len, t_len, tq, tk):
    if p_len:
        q_ref, new_ref, past_ref, o_ref = refs
    else:
        q_ref, new_ref, o_ref = refs
        past_ref = None
    qi = pl.program_id(1)
    n_past = p_len // tk
    n_new = (qi * tq) // tk

    def later(n):
        ri = lax.broadcasted_iota(jnp.int32, (n, n), 0)
        ci = lax.broadcasted_iota(jnp.int32, (n, n), 1)
        return jnp.where(ri > ci, 1.0, 0.0).astype(BF16)

    later_k = later(tk)
    later_q = later_k if tq == tk else later(tq)
    ri = lax.broadcasted_iota(jnp.int32, (tq, tq), 0)
    ci = lax.broadcasted_iota(jnp.int32, (tq, tq), 1)
    causal = ci < ri
    is_v = lax.broadcasted_iota(jnp.int32, (tq, LANES), 1) >= DH_C

    for pair in range(H_C // 2):
        outs = []
        for h in (2 * pair, 2 * pair + 1):
            hs = slice(LANES * h, LANES * (h + 1))
            q = q_ref[:, hs]

            def step(kv, mask, tri, carry, q=q):
                run, acc = carry
                z = _dot_nt(q, kv)
                log_take = jnp.minimum(z, 0.0) - jnp.log1p(jnp.exp(-jnp.abs(z)))
                log_keep = log_take - z
                if mask is not None:
                    log_keep = jnp.where(mask, log_keep, 0.0)
                hi = log_keep.astype(BF16)
                lo = (log_keep - hi.astype(F32)).astype(BF16)
                log_after = _dot(hi, tri) + _dot(lo, tri) + run
                a = jnp.exp(log_take + log_after)
                if mask is not None:
                    a = jnp.where(mask, a, 0.0)
                acc = acc + _dot(a.astype(BF16), kv)
                run = run + jnp.sum(log_keep, axis=-1, keepdims=True)
                return run, acc

            d0 = pl.multiple_of(qi * tq, tq)
            carry = (jnp.zeros((tq, 1), F32), jnp.zeros((tq, LANES), F32))
            carry = step(new_ref[pl.ds(d0, tq), hs].astype(BF16), causal, later_q, carry)

            def body(src_ref, n, i, carry, hs=hs, step=step):
                r0 = pl.multiple_of((n - 1 - i) * tk, tk)
                return step(src_ref[pl.ds(r0, tk), hs].astype(BF16), None, later_k, carry)

            carry = lax.fori_loop(0, n_new, functools.partial(body, new_ref, n_new), carry)
            if p_len:
                carry = lax.fori_loop(0, n_past, functools.partial(body, past_ref, n_past), carry)
            outs.append(carry[1])
        o_ref[:, LANES * pair:LANES * (pair + 1)] = jnp.where(
            is_v, outs[1], pltpu.roll(outs[0], DH_C, axis=1)).astype(BF16)


def _sb(q, rows, past, layer, tq, tk):
    b, t, _ = q.shape
    p_len = 0 if past is None else past.shape[2]
    args = [q, rows] + ([past] if past is not None else [])
    return pl.pallas_call(
        functools.partial(_sb_kernel, p_len=p_len, t_len=t, tq=tq, tk=tk),
        grid=(b, t // tq),
        in_specs=_attn_specs((b, t), tq, [512], 512, past, layer),
        out_specs=_out_spec(tq),
        out_shape=jax.ShapeDtypeStruct((b, t, 256), BF16),
        compiler_params=_params("parallel", "parallel"),
        name="sb",
    )(*args)


def _dsa_kernel(*refs, p_len, t_len, tq, tk, n_sel):
    if p_len:
        q_ref, qi_ref, wi_ref, new_ref, past_ref, o_ref, kk_scr, vv_scr, ki_scr, sc_scr, scd_scr = refs
    else:
        q_ref, qi_ref, wi_ref, new_ref, o_ref, kk_scr, vv_scr, ki_scr, sc_scr, scd_scr = refs
        past_ref = None
    qi = pl.program_id(1)
    l_len = p_len + t_len
    n_tiles = tk // LANES
    k_sel = float(n_sel)

    @pl.when(qi == 0)
    def _fill():
        r1 = lax.broadcasted_iota(jnp.int32, (LANES, 2 * LANES), 0)
        c1 = lax.broadcasted_iota(jnp.int32, (LANES, 2 * LANES), 1)
        dup_kv = jnp.where((c1 % DH_B == r1 % DH_B) & (c1 // LANES == r1 // DH_B), 1.0, 0.0).astype(BF16)
        r2 = lax.broadcasted_iota(jnp.int32, (D_IDX, LANES), 0)
        c2 = lax.broadcasted_iota(jnp.int32, (D_IDX, LANES), 1)
        dup_ki = jnp.where(c2 % D_IDX == r2, 1.0, 0.0).astype(BF16)

        def fill(src_ref, dst, n):
            for r0 in range(0, n, ROW_CHUNK):
                rc = min(ROW_CHUNK, n - r0)
                kv2 = _dot(src_ref[r0:r0 + rc, :LANES].astype(BF16), dup_kv)
                kk_scr[dst + r0:dst + r0 + rc, :] = kv2[:, :LANES].astype(BF16)
                vv_scr[dst + r0:dst + r0 + rc, :] = kv2[:, LANES:].astype(BF16)
                ki_scr[dst + r0:dst + r0 + rc, :] = _dot(src_ref[r0:r0 + rc, LANES:].astype(BF16), dup_ki).astype(BF16)
        if p_len:
            fill(past_ref, 0, p_len)
        fill(new_ref, p_len, t_len)

    d0 = pl.multiple_of(p_len + qi * tq, tq)
    n_full = (p_len + qi * tq) // tk
    lane = lax.broadcasted_iota(jnp.int32, (tq, LANES), 1)
    low = lane < DH_B
    sel_lo = jnp.where(low, 1.0, 0.0).astype(BF16)
    sel_hi = jnp.where(low, 0.0, 1.0).astype(BF16)
    dmask = _chunk_mask(tq)

    wi = wi_ref[...]
    wcol = [jnp.sum(jnp.where(lane == h, wi, 0.0), axis=-1, keepdims=True) for h in range(H_IDX)]
    q_idx = []
    for pr in range(H_IDX // 2):
        blk = qi_ref[:, LANES * pr:LANES * (pr + 1)]
        q_idx += [blk * sel_lo, blk * sel_hi]

    def idx_score(ki2):
        sc = None
        for h in range(H_IDX):
            term = jnp.maximum(_dot_nt(q_idx[h], ki2), 0.0) * wcol[h]
            sc = term if sc is None else sc + term
        return sc

    def score_body(kb, _):
        r0 = pl.multiple_of(kb * tk, tk)
        sc_scr[kb] = idx_score(ki_scr[pl.ds(r0, tk), :])
        return 0

    lax.fori_loop(0, n_full, score_body, 0)
    scd_scr[...] = jnp.where(dmask, idx_score(ki_scr[pl.ds(d0, tq), :]), -jnp.inf)

    def fold(tile_fn, diag_fn, comb, reduce, init):
        def body(kb, acc):
            for j in range(n_tiles):
                acc = comb(acc, tile_fn(sc_scr[kb, :, LANES * j:LANES * (j + 1)], kb * tk + LANES * j))
            return acc
        acc = lax.fori_loop(0, n_full, body, jnp.full((tq, LANES), init, F32))
        return comb(reduce(acc, axis=-1, keepdims=True), reduce(diag_fn(scd_scr[...]), axis=-1, keepdims=True))

    def wide(v):
        return jnp.broadcast_to(v, (tq, LANES))

    def count_ge(thr):
        thr_w = wide(thr)
        return fold(lambda s, _: jnp.where(s >= thr_w, 1.0, 0.0), lambda s: jnp.where(s >= thr, 1.0, 0.0),
                    jnp.add, jnp.sum, 0.0)

    def count_gt(thr):
        thr_w = wide(thr)
        return fold(lambda s, _: jnp.where(s > thr_w, 1.0, 0.0), lambda s: jnp.where(s > thr, 1.0, 0.0),
                    jnp.add, jnp.sum, 0.0)

    def min_ge(thr):
        thr_w = wide(thr)
        return fold(lambda s, _: jnp.where(s >= thr_w, s, jnp.inf), lambda s: jnp.where(s >= thr, s, jnp.inf),
                    jnp.minimum, jnp.min, jnp.inf)

    row_max = fold(lambda s, _: s, lambda s: s, jnp.maximum, jnp.max, -jnp.inf)
    row_min = fold(lambda s, _: s, lambda s: jnp.where(s > -jnp.inf, s, jnp.inf), jnp.minimum, jnp.min, jnp.inf)

    def bisect(_, lh):
        lo, hi = lh
        mid = lo + 0.5 * (hi - lo)
        ge = count_ge(mid) >= k_sel
        return jnp.where(ge, mid, lo), jnp.where(ge, hi, mid)

    def settle(lo):
        thr = min_ge(lo)
        n_gt = count_gt(thr)
        return thr, n_gt, jnp.max(jnp.where(n_gt >= k_sel, 1, 0))

    hi0 = jnp.where(row_max > 0, 2.0 * row_max, 0.5 * row_max) + 1.0
    lo, hi = lax.fori_loop(0, 16, bisect, (row_min, hi0))
    thr, n_gt, bad = settle(lo)

    def more(state):
        lo, hi, _, _, _, it = state
        lo, hi = lax.fori_loop(0, 4, bisect, (lo, hi))
        thr, n_gt, bad = settle(lo)
        return lo, hi, thr, n_gt, bad, it + 1

    _, _, thr, n_gt, _, _ = lax.while_loop(lambda st: (st[4] > 0) & (st[5] < 64), more,
                                           (lo, hi, thr, n_gt, bad, 0))

    n_ge = count_ge(thr)
    need = k_sel - n_gt
    excess = n_ge > k_sel
    thr_w = wide(thr)

    def cut_search(_):
        def count_tied_upto(j):
            j_w = wide(j)
            def tile(s, base):
                idx = base + lane
                return jnp.where((s == thr_w) & (idx <= j_w), 1.0, 0.0)
            def diag(s):
                idx = d0 + lax.broadcasted_iota(jnp.int32, (tq, tq), 1)
                return jnp.where((s == thr) & (idx <= j), 1.0, 0.0)
            return fold(tile, diag, jnp.add, jnp.sum, 0.0)

        def step(_, jj):
            jlo, jhi = jj
            mid = (jlo + jhi) >> 1
            ok = count_tied_upto(mid) >= need
            return jnp.where(ok, jlo, mid), jnp.where(ok, mid, jhi)

        steps = int(math.ceil(math.log2(l_len))) + 1
        _, jhi = lax.fori_loop(0, steps, step,
                               (jnp.full((tq, 1), -1, jnp.int32), jnp.full((tq, 1), l_len - 1, jnp.int32)))
        return jnp.where(excess, jhi, l_len)

    cut = lax.cond(jnp.max(jnp.where(excess, 1, 0)) > 0, cut_search,
                   lambda _: jnp.full((tq, 1), l_len, jnp.int32), 0)
    cut_w = wide(cut)

    def picked_tile(s, base):
        return (s > thr_w) | ((s == thr_w) & (base + lane <= cut_w))

    idx_d = d0 + lax.broadcasted_iota(jnp.int32, (tq, tq), 1)
    sd = scd_scr[...]
    picked_d = (sd > thr) | ((sd == thr) & (idx_d <= cut))

    for pair in range(H_B // 2):
        blk = q_ref[:, LANES * pair:LANES * (pair + 1)]
        outs = []
        for q in (blk * sel_lo, blk * sel_hi):
            def body(kb, carry, q=q):
                r0 = pl.multiple_of(kb * tk, tk)
                raw = _dot_nt(q, kk_scr[pl.ds(r0, tk), :])
                s = jnp.concatenate(
                    [jnp.where(picked_tile(sc_scr[kb, :, LANES * j:LANES * (j + 1)], kb * tk + LANES * j),
                               raw[:, LANES * j:LANES * (j + 1)], NEG_INF) for j in range(n_tiles)], axis=1)
                return _softmax_step(s, vv_scr[pl.ds(r0, tk), :], carry)

            carry = lax.fori_loop(0, n_full, body, _softmax_init(tq))
            s = jnp.where(picked_d, _dot_nt(q, kk_scr[pl.ds(d0, tq), :]), NEG_INF)
            _, l, acc = _softmax_step(s, vv_scr[pl.ds(d0, tq), :], carry)
            outs.append(acc / l)
        o_ref[:, LANES * pair:LANES * (pair + 1)] = jnp.where(low, outs[0], outs[1]).astype(BF16)


def _dsa(q, qidx, wi, rows, past, layer, tq, tk):
    b, t, _ = q.shape
    p_len = 0 if past is None else past.shape[2]
    l_len = p_len + t
    n_sel = min(TOPK_MAX, l_len // 4)
    n_blocks = max((l_len - tq) // tk, 1)
    args = [q, qidx, wi, rows] + ([past] if past is not None else [])
    return pl.pallas_call(
        functools.partial(_dsa_kernel, p_len=p_len, t_len=t, tq=tq, tk=tk, n_sel=n_sel),
        grid=(b, t // tq),
        in_specs=_attn_specs((b, t), tq, [256, 512, LANES], 192, past, layer),
        out_specs=_out_spec(tq),
        out_shape=jax.ShapeDtypeStruct((b, t, 256), BF16),
        scratch_shapes=[pltpu.VMEM((l_len, LANES), BF16)] * 3
                       + [pltpu.VMEM((n_blocks, tq, tk), F32), pltpu.VMEM((tq, tq), F32)],
        compiler_params=_params("parallel", "arbitrary"),
        name="dsa",
    )(*args)


def _gather_cols(w, src, scale):
    src = [int(s) for s in src]
    scale = [float(c) if s >= 0 else 0.0 for s, c in zip(src, scale)]
    pieces, i = [], 0
    while i < len(src):
        j = i + 1
        while j < len(src) and scale[j] == scale[i] and (src[j] == src[j - 1] + 1 if src[i] >= 0 else src[j] < 0):
            j += 1
        if src[i] < 0:
            pieces.append(jnp.zeros((w.shape[0], j - i), w.dtype))
        else:
            run = w[:, src[i]:src[i] + j - i]
            pieces.append(run if scale[i] == 1.0 else run * scale[i])
        i = j
    return jnp.concatenate(pieces, axis=1)


class _Cols:
    def __init__(self):
        self.src, self.scale, self.rot_src, self.rot_sign, self.rope = [], [], [], [], []

    def plain(self, start, n, scale=1.0):
        for i in range(n):
            self._add(start + i, scale, -1, 0.0, None)

    def pad(self, n):
        for _ in range(n):
            self._add(-1, 0.0, -1, 0.0, None)

    def roped(self, start, d, scale=1.0):
        half = d // 2
        for i in range(d):
            partner, sign = (start + i + half, -1.0) if i < half else (start + i - half, 1.0)
            self._add(start + i, scale, partner, sign * scale, (d, i % half))

    def _add(self, src, scale, rot_src, rot_sign, rope):
        self.src.append(src)
        self.scale.append(scale)
        self.rot_src.append(rot_src)
        self.rot_sign.append(rot_sign)
        self.rope.append(rope)

    def weights(self, w):
        return (_gather_cols(w, self.src, self.scale).astype(BF16),
                _gather_cols(w, self.rot_src, self.rot_sign).astype(BF16))

    def tables(self, pos):
        cs = {}
        for d in sorted({r[0] for r in self.rope if r is not None}):
            freqs = ROPE_THETA ** (-jnp.arange(d // 2, dtype=F32) * 2.0 / d)
            ang = pos.astype(F32)[:, None] * freqs[None, :]
            cs[d] = (jnp.cos(ang), jnp.sin(ang))
        cos, sin, i, n = [], [], 0, pos.shape[0]
        while i < len(self.rope):
            r, j = self.rope[i], i + 1
            if r is None:
                while j < len(self.rope) and self.rope[j] is None:
                    j += 1
                cos.append(jnp.ones((n, j - i), F32))
                sin.append(jnp.zeros((n, j - i), F32))
            else:
                while j < len(self.rope) and self.rope[j] == (r[0], r[1] + j - i):
                    j += 1
                cos.append(cs[r[0]][0][:, r[1]:r[1] + j - i])
                sin.append(cs[r[0]][1][:, r[1]:r[1] + j - i])
            i = j
        return jnp.concatenate(cos, axis=1), jnp.concatenate(sin, axis=1)


def _plans():
    p = _Cols()
    p.plain(O_CQ, Q_LORA)
    p.plain(O_CKV, KV_LORA)
    for h in range(H_C):
        p.plain(O_KC + DH_C * h, DH_C)
        p.plain(O_VC + DH_C * h, DH_C)
    for h in range(H_C):
        p.plain(O_QC + DH_C * h, DH_C, DH_C ** -0.5)
        p.pad(LANES - DH_C)
    p.plain(O_WIB, H_IDX)
    p.pad(LANES - H_IDX)
    assert len(p.src) == P_END

    r = _Cols()
    r.roped(O_KB, DH_B)
    r.plain(O_VB, DH_B)
    r.roped(O_KIB, D_IDX)
    r.pad(R_DIFF - (2 * DH_B + D_IDX))
    for h in range(H_D):
        r.roped(O_KD + 2 * DQK_D * h, DQK_D)
        r.roped(O_KD + 2 * DQK_D * h + DQK_D, DQK_D)
        r.plain(O_VD + DV_D * h, DV_D)
    r.roped(O_KR, ROPE_A)
    r.pad(LANES - ROPE_A)
    for h in range(H_B):
        r.roped(O_QB + DH_B * h, DH_B, DH_B ** -0.5)
    for h in range(H_IDX):
        r.roped(O_QIB + D_IDX * h, D_IDX)
    for h in range(H_D):
        r.roped(O_QD + 2 * DQK_D * h, DQK_D)
        r.roped(O_QD + 2 * DQK_D * h + DQK_D, DQK_D)
        r.pad(LANES - 2 * DQK_D)
    assert len(r.src) == R_END

    q = _Cols()
    for h in range(H_A):
        q.plain((NOPE_A + ROPE_A) * h, NOPE_A)
        q.roped((NOPE_A + ROPE_A) * h + NOPE_A, ROPE_A)
        q.pad(LANES - NOPE_A - ROPE_A)
    return p, r, q


def _layer_weights(l, prm, plans):
    p, r, q = plans
    w_in = prm["w_in"][l]
    wr, wrr = r.weights(w_in)
    wqb, wqbr = q.weights(prm["mla_w_qb"][l])
    wkvb = prm["mla_w_kvb"][l]
    k_src, v_src = [], []
    for h in range(H_A):
        k_src += list(range((NOPE_A + V_A) * h, (NOPE_A + V_A) * h + NOPE_A)) + [-1] * (LANES - NOPE_A)
        v_src += list(range((NOPE_A + V_A) * h + NOPE_A, (NOPE_A + V_A) * (h + 1)))
    place = np.zeros((ROPE_A, H_A * LANES), np.float32)
    for h in range(H_A):
        place[np.arange(ROPE_A), LANES * h + NOPE_A + np.arange(ROPE_A)] = 1.0
    dn = jnp.concatenate([jnp.zeros((2 * DQK_D,), F32), prm["diff_norm"][l]])[None, :]

    def ff(tag):
        return (prm[f"norm_{tag}"][l][None, :], prm[f"w_{tag}_gate"][l].astype(BF16),
                prm[f"w_{tag}_up"][l].astype(BF16), prm[f"w_{tag}_down"][l].astype(BF16))

    return {
        "ff1": ff("ff1"), "ff2": ff("ff2"),
        "nm": prm["norm_mix"][l][None, :],
        "wp": p.weights(w_in)[0], "wr": wr, "wrr": wrr,
        "qn": prm["mla_q_norm"][l][None, :], "wqb": wqb, "wqbr": wqbr,
        "kvn": prm["mla_kv_norm"][l][None, :],
        "wk": _gather_cols(wkvb, k_src, np.ones(len(k_src))).astype(BF16),
        "pk": jnp.asarray(place).astype(BF16),
        "wv": _gather_cols(wkvb, v_src, np.ones(len(v_src))).astype(BF16),
        "lam": prm["diff_lambda"][l], "dn": dn,
        "wo": prm["w_out"][l].astype(BF16),
    }


def _trunk(x, caches, pos, layers, plans, fnorm, tq, tk, proj_fold):
    b, t, _ = x.shape
    depth = len(layers)
    tm = min(256, b * t)
    _, r_plan, q_plan = plans
    pb, pt = (b // proj_fold, t * proj_fold)
    pos_p = jnp.tile(pos, proj_fold)
    tabs = r_plan.tables(pos_p) + q_plan.tables(pos_p)
    tq_p = min(256, pt)
    xf = x.reshape(b * t, D_MODEL)
    rows = [[], [], [], []]
    for l, lw in enumerate(layers):
        lam_init = 0.8 - 0.6 * math.exp(-0.3 * l)
        xf = _ffn(xf, None, None, *lw["ff1"], None, tm)
        outs = _proj(xf.reshape(pb, pt, D_MODEL), lw, tabs, tq_p)
        r_mla, r_dsa, r_sb, r_diff, q_mla, q_dsa, q_idx, w_idx, q_sb, q_diff = [
            o.reshape(b, t, o.shape[-1]) for o in outs]
        past = [None] * 4 if caches is None else caches
        mix = (
            _mla(q_mla, r_mla, past[0], l, lw, tq, tk),
            _dsa(q_dsa, q_idx, w_idx, r_dsa, past[1], l, tq, tk),
            _sb(q_sb, r_sb, past[2], l, tq, tk),
            _diff(q_diff, r_diff, past[3], l, lw, lam_init, tq, tk),
        )
        mix = [m.reshape(b * t, 256) for m in mix]
        xf = _ffn(xf, mix, lw["wo"], *lw["ff2"], fnorm if l == depth - 1 else None, tm)
        for acc, r in zip(rows, (r_mla, r_dsa, r_sb, r_diff)):
            acc.append(r)
    y = xf.reshape(b, t, D_MODEL)
    return (y, jnp.stack(rows[0]), jnp.stack(rows[1]),
            jnp.stack(rows[2]).reshape(depth, b, t, H_C, 2 * DH_C),
            jnp.stack(rows[3]).reshape(depth, b, t, H_D, 2 * DQK_D + DV_D))


def kernel(x_prompt, x_sample, cache_mla, cache_dsa, cache_sb, cache_diff, norm_ff1, w_ff1_gate, w_ff1_up, w_ff1_down, norm_mix, w_in, mla_q_norm, mla_w_qb, mla_kv_norm, mla_w_kvb, diff_lambda, diff_norm, w_out, norm_ff2, w_ff2_gate, w_ff2_up, w_ff2_down, final_norm):
    prm = dict(norm_ff1=norm_ff1, w_ff1_gate=w_ff1_gate, w_ff1_up=w_ff1_up, w_ff1_down=w_ff1_down,
               norm_mix=norm_mix, w_in=w_in, mla_q_norm=mla_q_norm, mla_w_qb=mla_w_qb,
               mla_kv_norm=mla_kv_norm, mla_w_kvb=mla_w_kvb, diff_lambda=diff_lambda, diff_norm=diff_norm,
               w_out=w_out, norm_ff2=norm_ff2, w_ff2_gate=w_ff2_gate, w_ff2_up=w_ff2_up, w_ff2_down=w_ff2_down)
    depth = w_in.shape[0]
    plans = _plans()
    layers = [_layer_weights(l, prm, plans) for l in range(depth)]
    fnorm = final_norm[None, :]

    t_p = x_prompt.shape[1]
    assert t_p % CHUNK == 0
    tq_p = min(256, t_p)
    out_p = _trunk(x_prompt, None, jnp.arange(t_p, dtype=jnp.int32), layers, plans, fnorm, tq_p, tq_p, 1)


    b_s, t_s, _ = x_sample.shape
    past = cache_mla.shape[2]
    assert past % CHUNK == 0 and t_s <= CHUNK, "new sample frames must sit in one chunk after whole cached chunks"
    tk_s = math.gcd(past, 512)
    caches = (cache_mla, cache_dsa,
              cache_sb.reshape(cache_sb.shape[:3] + (-1,)), cache_diff.reshape(cache_diff.shape[:3] + (-1,)))
    fold = math.gcd(b_s, max(1, 256 // t_s))
    out_s = _trunk(x_sample, caches, past + jnp.arange(t_s, dtype=jnp.int32), layers, plans, fnorm, t_s, tk_s, fold)

    return (out_p[0], out_s[0]) + tuple(out_p[1:]) + tuple(out_s[1:])
```

```python
import functools
import math

import numpy as np
import jax
import jax.numpy as jnp
from jax import lax
from jax.experimental import pallas as pl
from jax.experimental.pallas import tpu as pltpu

F32 = jnp.float32
BF16 = jnp.bfloat16

D_MODEL = 1024
CHUNK = 64
ROPE_THETA = 10000.0
EPS = 1e-6
NEG_INF = -1e30
HEAD_DIM = D_MODEL // 16
D_FF = 11 * D_MODEL // 4
H_A = 4
Q_LORA = 4 * HEAD_DIM
KV_LORA = 2 * HEAD_DIM
NOPE_A = HEAD_DIM
ROPE_A = HEAD_DIM // 2
V_A = HEAD_DIM
H_B = 4
DH_B = HEAD_DIM
H_IDX = 8
D_IDX = HEAD_DIM
TOPK_MAX = 256
H_C = 4
DH_C = HEAD_DIM
H_D = 4
DQK_D = HEAD_DIM // 2
DV_D = HEAD_DIM
COL_SIZES = (Q_LORA, KV_LORA, ROPE_A,
             H_B * DH_B, DH_B, DH_B, H_IDX * D_IDX, D_IDX, H_IDX,
             H_C * DH_C, H_C * DH_C, H_C * DH_C,
             H_D * 2 * DQK_D, H_D * 2 * DQK_D, H_D * DV_D)
(O_CQ, O_CKV, O_KR, O_QB, O_KB, O_VB, O_QIB, O_KIB, O_WIB,
 O_QC, O_KC, O_VC, O_QD, O_KD, O_VD) = np.concatenate([[0], np.cumsum(COL_SIZES)[:-1]]).tolist()

LANES = 128
VMEM_LIMIT = 56 * 1024 * 1024
FF_CHUNK = 256
ROW_CHUNK = 512

P_CQ, P_CKV, P_SB, P_QSB, P_WI, P_END = 0, 256, 384, 896, 1408, 1536
R_DSA, R_DIFF, R_KR, R_QDSA, R_QI, R_QDIFF, R_END = 0, 256, 768, 896, 1152, 1664, 2176


def _dot(a, b):
    return jnp.dot(a, b, preferred_element_type=F32)


def _dot_nt(a, b):
    return lax.dot_general(a, b, (((1,), (1,)), ((), ())), preferred_element_type=F32)


def _rms(x, g):
    return x * lax.rsqrt(jnp.mean(x * x, axis=-1, keepdims=True) + EPS) * g


def _const_spec(shape):
    n = len(shape)
    return pl.BlockSpec(shape, lambda *_: (0,) * n, pipeline_mode=pl.Buffered(1))


def _params(*sem):
    return pltpu.CompilerParams(dimension_semantics=sem, vmem_limit_bytes=VMEM_LIMIT)


def _ffn_kernel(*refs, has_mix, has_final):
    it = iter(refs)
    x_ref = next(it)
    if has_mix:
        o_refs = [next(it) for _ in range(4)]
        wo_ref = next(it)
    g_ref, wg_ref, wu_ref, wd_ref = next(it), next(it), next(it), next(it)
    fn_ref = next(it) if has_final else None
    out_ref = next(it)

    x = x_ref[...]
    if has_mix:
        for i, o_ref in enumerate(o_refs):
            x = x + _dot(o_ref[...], wo_ref[256 * i:256 * (i + 1), :])
    xn = _rms(x, g_ref[...]).astype(BF16)
    acc = jnp.zeros_like(x)
    for c0 in range(0, D_FF, FF_CHUNK):
        gate = _dot(xn, wg_ref[:, c0:c0 + FF_CHUNK])
        up = _dot(xn, wu_ref[:, c0:c0 + FF_CHUNK])
        hid = (gate * jax.nn.sigmoid(gate) * up).astype(BF16)
        acc = acc + _dot(hid, wd_ref[c0:c0 + FF_CHUNK, :])
    y = x + 0.5 * acc
    if has_final:
        y = _rms(y, fn_ref[...])
    out_ref[...] = y


def _ffn(x, mix, wo, g, wg, wu, wd, fnorm, tm):
    n = x.shape[0]
    has_mix, has_final = mix is not None, fnorm is not None
    row = lambda w: pl.BlockSpec((tm, w), lambda i: (i, 0))
    args, specs = [x], [row(D_MODEL)]
    if has_mix:
        args += list(mix) + [wo]
        specs += [row(256)] * 4 + [_const_spec(wo.shape)]
    args += [g, wg, wu, wd]
    specs += [_const_spec(g.shape), _const_spec(wg.shape), _const_spec(wu.shape), _const_spec(wd.shape)]
    if has_final:
        args.append(fnorm)
        specs.append(_const_spec(fnorm.shape))
    return pl.pallas_call(
        functools.partial(_ffn_kernel, has_mix=has_mix, has_final=has_final),
        grid=(n // tm,),
        in_specs=specs,
        out_specs=row(D_MODEL),
        out_shape=jax.ShapeDtypeStruct((n, D_MODEL), F32),
        compiler_params=_params("parallel"),
        name="ffn",
    )(*args)


def _proj_kernel(x_ref, g_ref, wp_ref, wr_ref, wrr_ref, cr_ref, sr_ref,
                 qn_ref, wqb_ref, wqbr_ref, cq_ref, sq_ref, kvn_ref,
                 rmla_ref, rdsa_ref, rsb_ref, rdiff_ref,
                 qmla_ref, qdsa_ref, qi_ref, wi_ref, qsb_ref, qdiff_ref):
    xn = _rms(x_ref[...], g_ref[...]).astype(BF16)

    def plain(a, b):
        return _dot(xn, wp_ref[:, a:b])

    def roped(a, b):
        return (_dot(xn, wr_ref[:, a:b]) * cr_ref[:, a:b]
                + _dot(xn, wrr_ref[:, a:b]) * sr_ref[:, a:b])

    cqn = _rms(plain(P_CQ, P_CKV), qn_ref[...]).astype(BF16)
    qmla_ref[...] = (_dot(cqn, wqb_ref[...]) * cq_ref[...]
                     + _dot(cqn, wqbr_ref[...]) * sq_ref[...]).astype(BF16)
    rmla_ref[:, :KV_LORA] = _rms(plain(P_CKV, P_SB), kvn_ref[...])
    rmla_ref[:, KV_LORA:] = roped(R_KR, R_QDSA)[:, :ROPE_A]
    rdsa_ref[...] = roped(R_DSA, R_DIFF)[:, :2 * DH_B + D_IDX]
    qdsa_ref[...] = roped(R_QDSA, R_QI).astype(BF16)
    qi_ref[...] = roped(R_QI, R_QDIFF).astype(BF16)
    wi_ref[...] = plain(P_WI, P_END) * (H_IDX ** -0.5)
    rsb_ref[...] = plain(P_SB, P_QSB)
    qsb_ref[...] = plain(P_QSB, P_WI).astype(BF16)
    rdiff_ref[...] = roped(R_DIFF, R_KR)
    qdiff_ref[...] = roped(R_QDIFF, R_END).astype(BF16)


def _proj(x, lw, tabs, tq):
    b, t, _ = x.shape
    cr, sr, cq, sq = tabs
    tok = lambda w: pl.BlockSpec((None, tq, w), lambda bi, i: (bi, i, 0))
    tab = lambda w: pl.BlockSpec((tq, w), lambda bi, i: (i, 0))
    consts = [lw["nm"], lw["wp"], lw["wr"], lw["wrr"]]
    consts2 = [lw["qn"], lw["wqb"], lw["wqbr"]]
    in_specs = ([tok(D_MODEL)] + [_const_spec(a.shape) for a in consts] + [tab(R_END), tab(R_END)]
                + [_const_spec(a.shape) for a in consts2] + [tab(512), tab(512), _const_spec(lw["kvn"].shape)])
    widths = [(160, F32), (192, F32), (512, F32), (512, F32),
              (512, BF16), (256, BF16), (512, BF16), (LANES, F32), (512, BF16), (512, BF16)]
    return pl.pallas_call(
        _proj_kernel,
        grid=(b, t // tq),
        in_specs=in_specs,
        out_specs=[tok(w) for w, _ in widths],
        out_shape=[jax.ShapeDtypeStruct((b, t, w), dt) for w, dt in widths],
        compiler_params=_params("parallel", "parallel"),
        name="proj",
    )(x, *consts, cr, sr, *consts2, cq, sq, lw["kvn"])


def _chunk_mask(tq):
    ri = lax.broadcasted_iota(jnp.int32, (tq, tq), 0)
    ci = lax.broadcasted_iota(jnp.int32, (tq, tq), 1)
    return (ci // CHUNK) <= (ri // CHUNK)


def _softmax_step(s, v, carry):
    m, l, acc = carry
    m_new = jnp.maximum(m, jnp.max(s, axis=-1, keepdims=True))
    alpha = jnp.exp(m - m_new)
    p = jnp.exp(s - m_new)
    l = alpha * l + jnp.sum(p, axis=-1, keepdims=True)
    acc = alpha * acc + _dot(p.astype(BF16), v)
    return m_new, l, acc


def _softmax_init(tq):
    return (jnp.full((tq, 1), NEG_INF, F32), jnp.zeros((tq, 1), F32), jnp.zeros((tq, LANES), F32))


def _lane_lt(tq, n):
    return lax.broadcasted_iota(jnp.int32, (tq, LANES), 1) < n


def _attn_specs(b_t, tq, q_widths, rows_w, past, layer):
    b, t = b_t
    specs = [pl.BlockSpec((None, tq, w), lambda bi, i: (bi, i, 0)) for w in q_widths]
    specs.append(pl.BlockSpec((None, t, rows_w), lambda bi, i: (bi, 0, 0)))
    if past is not None:
        specs.append(pl.BlockSpec((None, None, past.shape[2], rows_w), lambda bi, i: (layer, bi, 0, 0)))
    return specs


def _out_spec(tq):
    return pl.BlockSpec((None, tq, 256), lambda bi, i: (bi, i, 0))


def _mla_kernel(*refs, p_len, t_len, tq, tk):
    if p_len:
        q_ref, new_ref, past_ref, wk_ref, pk_ref, wv_ref, o_ref, k_scr, v_scr = refs
    else:
        q_ref, new_ref, wk_ref, pk_ref, wv_ref, o_ref, k_scr, v_scr = refs
        past_ref = None
    qi = pl.program_id(1)
    scale = (NOPE_A + ROPE_A) ** -0.5

    @pl.when(qi == 0)
    def _fill():
        def fill(src_ref, dst, n):
            for r0 in range(0, n, ROW_CHUNK):
                rc = min(ROW_CHUNK, n - r0)
                lat = src_ref[r0:r0 + rc, :KV_LORA].astype(BF16)
                rope = src_ref[r0:r0 + rc, KV_LORA:].astype(BF16)
                k_scr[dst + r0:dst + r0 + rc, :] = (_dot(lat, wk_ref[...]) + _dot(rope, pk_ref[...])).astype(BF16)
                v_scr[dst + r0:dst + r0 + rc, :] = _dot(lat, wv_ref[...]).astype(BF16)
        if p_len:
            fill(past_ref, 0, p_len)
        fill(new_ref, p_len, t_len)

    d0 = pl.multiple_of(p_len + qi * tq, tq)
    n_full = (p_len + qi * tq) // tk
    dmask = _chunk_mask(tq)
    low = _lane_lt(tq, V_A)
    qs = [q_ref[:, LANES * h:LANES * (h + 1)] for h in range(H_A)]

    def step(r0, n, mask, carries):
        out = []
        for h in range(H_A):
            s = _dot_nt(qs[h], k_scr[pl.ds(r0, n), LANES * h:LANES * (h + 1)]) * scale
            if mask is not None:
                s = jnp.where(mask, s, NEG_INF)
            out.append(_softmax_step(s, v_scr[pl.ds(r0, n), LANES * (h // 2):LANES * (h // 2 + 1)], carries[h]))
        return tuple(out)

    carries = lax.fori_loop(0, n_full, lambda kb, c: step(pl.multiple_of(kb * tk, tk), tk, None, c),
                            tuple(_softmax_init(tq) for _ in range(H_A)))
    carries = step(d0, tq, dmask, carries)
    outs = [acc / l for _, l, acc in carries]
    for pair in range(H_A // 2):
        o_ref[:, LANES * pair:LANES * (pair + 1)] = jnp.where(low, outs[2 * pair], outs[2 * pair + 1]).astype(BF16)


def _mla(q, rows, past, layer, lw, tq, tk):
    b, t, _ = q.shape
    p_len = 0 if past is None else past.shape[2]
    consts = [lw["wk"], lw["pk"], lw["wv"]]
    args = [q, rows] + ([past] if past is not None else []) + consts
    return pl.pallas_call(
        functools.partial(_mla_kernel, p_len=p_len, t_len=t, tq=tq, tk=tk),
        grid=(b, t // tq),
        in_specs=_attn_specs((b, t), tq, [512], 160, past, layer) + [_const_spec(a.shape) for a in consts],
        out_specs=_out_spec(tq),
        out_shape=jax.ShapeDtypeStruct((b, t, 256), BF16),
        scratch_shapes=[pltpu.VMEM((p_len + t, 512), BF16), pltpu.VMEM((p_len + t, 256), BF16)],
        compiler_params=_params("parallel", "arbitrary"),
        name="mla",
    )(*args)


def _diff_kernel(*refs, p_len, t_len, tq, tk, lam_init):
    if p_len:
        q_ref, new_ref, past_ref, lam_ref, dn_ref, o_ref = refs
    else:
        q_ref, new_ref, lam_ref, dn_ref, o_ref = refs
        past_ref = None
    qi = pl.program_id(1)
    scale = DQK_D ** -0.5
    lam = lam_ref[...]
    lam_full = (jnp.exp(jnp.sum(lam[0:1] * lam[1:2], axis=-1, keepdims=True))
                - jnp.exp(jnp.sum(lam[2:3] * lam[3:4], axis=-1, keepdims=True)) + lam_init)
    dmask = _chunk_mask(tq)
    lane = lax.broadcasted_iota(jnp.int32, (tq, LANES), 1)
    sel1 = jnp.where(lane < DQK_D, 1.0, 0.0).astype(BF16)
    sel2 = jnp.where((lane >= DQK_D) & (lane < 2 * DQK_D), 1.0, 0.0).astype(BF16)
    is_v = lane >= 2 * DQK_D
    n_past = p_len // tk
    n_new = (qi * tq) // tk

    q12 = []
    for h in range(H_D):
        qh = q_ref[:, LANES * h:LANES * (h + 1)]
        q12 += [qh * sel1, qh * sel2]

    def step(src_ref, r0, n, mask, carries):
        out = []
        for h in range(H_D):
            kv = src_ref[pl.ds(r0, n), LANES * h:LANES * (h + 1)].astype(BF16)
            for m in range(2):
                s = _dot_nt(q12[2 * h + m], kv) * scale
                if mask is not None:
                    s = jnp.where(mask, s, NEG_INF)
                out.append(_softmax_step(s, kv, carries[2 * h + m]))
        return tuple(out)

    def body(src_ref, kb, carries):
        return step(src_ref, pl.multiple_of(kb * tk, tk), tk, None, carries)

    carries = tuple(_softmax_init(tq) for _ in range(2 * H_D))
    if p_len:
        carries = lax.fori_loop(0, n_past, functools.partial(body, past_ref), carries)
    carries = lax.fori_loop(0, n_new, functools.partial(body, new_ref), carries)
    carries = step(new_ref, pl.multiple_of(qi * tq, tq), tq, dmask, carries)
    outs = []
    for h in range(H_D):
        (_, l1, a1), (_, l2, a2) = carries[2 * h], carries[2 * h + 1]
        o = a1 / l1 - lam_full * (a2 / l2)
        ms = jnp.sum(jnp.where(is_v, o * o, 0.0), axis=-1, keepdims=True) * (1.0 / DV_D)
        outs.append(o * lax.rsqrt(ms + EPS) * dn_ref[...] * (1.0 - lam_init))
    for pair in range(H_D // 2):
        o_ref[:, LANES * pair:LANES * (pair + 1)] = jnp.where(
            is_v, outs[2 * pair + 1], pltpu.roll(outs[2 * pair], 2 * DQK_D, axis=1)).astype(BF16)


def _diff(q, rows, past, layer, lw, lam_init, tq, tk):
    b, t, _ = q.shape
    p_len = 0 if past is None else past.shape[2]
    consts = [lw["lam"], lw["dn"]]
    args = [q, rows] + ([past] if past is not None else []) + consts
    return pl.pallas_call(
        functools.partial(_diff_kernel, p_len=p_len, t_len=t, tq=tq, tk=tk, lam_init=lam_init),
        grid=(b, t // tq),
        in_specs=_attn_specs((b, t), tq, [512], 512, past, layer) + [_const_spec(a.shape) for a in consts],
        out_specs=_out_spec(tq),
        out_shape=jax.ShapeDtypeStruct((b, t, 256), BF16),
        compiler_params=_params("parallel", "parallel"),
        name="diff",
    )(*args)


def _sb_kernel(*refs, p_len, t_len, tq, tk):
    if p_len:
        q_ref, new_ref, past_ref, o_ref = refs
    else:
        q_ref, new_ref, o_ref = refs
        past_ref = None
    qi = pl.program_id(1)
    n_past = p_len // tk
    n_new = (qi * tq) // tk

    def later(n):
        ri = lax.broadcasted_iota(jnp.int32, (n, n), 0)
        ci = lax.broadcasted_iota(jnp.int32, (n, n), 1)
        return jnp.where(ri > ci, 1.0, 0.0).astype(BF16)

    later_k = later(tk)
    later_q = later_k if tq == tk else later(tq)
    ri = lax.broadcasted_iota(jnp.int32, (tq, tq), 0)
    ci = lax.broadcasted_iota(jnp.int32, (tq, tq), 1)
    causal = ci < ri
    is_v = lax.broadcasted_iota(jnp.int32, (tq, LANES), 1) >= DH_C

    qs = [q_ref[:, LANES * h:LANES * (h + 1)] for h in range(H_C)]

    def step(src_ref, r0, n, mask, tri, carries):
        out = []
        for h in range(H_C):
            run, acc = carries[h]
            kv = src_ref[pl.ds(r0, n), LANES * h:LANES * (h + 1)].astype(BF16)
            z = _dot_nt(qs[h], kv)
            log_take = jnp.minimum(z, 0.0) - jnp.log1p(jnp.exp(-jnp.abs(z)))
            log_keep = log_take - z
            if mask is not None:
                log_keep = jnp.where(mask, log_keep, 0.0)
            hi = log_keep.astype(BF16)
            lo = (log_keep - hi.astype(F32)).astype(BF16)
            log_after = _dot(hi, tri) + _dot(lo, tri) + run
            a = jnp.exp(log_take + log_after)
            if mask is not None:
                a = jnp.where(mask, a, 0.0)
            acc = acc + _dot(a.astype(BF16), kv)
            run = run + jnp.sum(log_keep, axis=-1, keepdims=True)
            out.append((run, acc))
        return tuple(out)

    carries = tuple((jnp.zeros((tq, 1), F32), jnp.zeros((tq, LANES), F32)) for _ in range(H_C))
    carries = step(new_ref, pl.multiple_of(qi * tq, tq), tq, causal, later_q, carries)

    def body(src_ref, n, i, carries):
        return step(src_ref, pl.multiple_of((n - 1 - i) * tk, tk), tk, None, later_k, carries)

    carries = lax.fori_loop(0, n_new, functools.partial(body, new_ref, n_new), carries)
    if p_len:
        carries = lax.fori_loop(0, n_past, functools.partial(body, past_ref, n_past), carries)
    for pair in range(H_C // 2):
        o_ref[:, LANES * pair:LANES * (pair + 1)] = jnp.where(
            is_v, carries[2 * pair + 1][1], pltpu.roll(carries[2 * pair][1], DH_C, axis=1)).astype(BF16)


def _sb(q, rows, past, layer, tq, tk):
    b, t, _ = q.shape
    p_len = 0 if past is None else past.shape[2]
    args = [q, rows] + ([past] if past is not None else [])
    return pl.pallas_call(
        functools.partial(_sb_kernel, p_len=p_len, t_len=t, tq=tq, tk=tk),
        grid=(b, t // tq),
        in_specs=_attn_specs((b, t), tq, [512], 512, past, layer),
        out_specs=_out_spec(tq),
        out_shape=jax.ShapeDtypeStruct((b, t, 256), BF16),
        compiler_params=_params("parallel", "parallel"),
        name="sb",
    )(*args)


def _dsa_kernel(*refs, p_len, t_len, tq, tk, n_sel):
    if p_len:
        q_ref, qi_ref, wi_ref, new_ref, past_ref, o_ref, kk_scr, vv_scr, ki_scr, sc_scr, scd_scr = refs
    else:
        q_ref, qi_ref, wi_ref, new_ref, o_ref, kk_scr, vv_scr, ki_scr, sc_scr, scd_scr = refs
        past_ref = None
    qi = pl.program_id(1)
    l_len = p_len + t_len
    n_tiles = tk // LANES
    k_sel = float(n_sel)

    @pl.when(qi == 0)
    def _fill():
        r1 = lax.broadcasted_iota(jnp.int32, (LANES, 2 * LANES), 0)
        c1 = lax.broadcasted_iota(jnp.int32, (LANES, 2 * LANES), 1)
        dup_kv = jnp.where((c1 % DH_B == r1 % DH_B) & (c1 // LANES == r1 // DH_B), 1.0, 0.0).astype(BF16)
        r2 = lax.broadcasted_iota(jnp.int32, (D_IDX, LANES), 0)
        c2 = lax.broadcasted_iota(jnp.int32, (D_IDX, LANES), 1)
        dup_ki = jnp.where(c2 % D_IDX == r2, 1.0, 0.0).astype(BF16)

        def fill(src_ref, dst, n):
            for r0 in range(0, n, ROW_CHUNK):
                rc = min(ROW_CHUNK, n - r0)
                kv2 = _dot(src_ref[r0:r0 + rc, :LANES].astype(BF16), dup_kv)
                kk_scr[dst + r0:dst + r0 + rc, :] = kv2[:, :LANES].astype(BF16)
                vv_scr[dst + r0:dst + r0 + rc, :] = kv2[:, LANES:].astype(BF16)
                ki_scr[dst + r0:dst + r0 + rc, :] = _dot(src_ref[r0:r0 + rc, LANES:].astype(BF16), dup_ki).astype(BF16)
        if p_len:
            fill(past_ref, 0, p_len)
        fill(new_ref, p_len, t_len)

    d0 = pl.multiple_of(p_len + qi * tq, tq)
    n_full = (p_len + qi * tq) // tk
    lane = lax.broadcasted_iota(jnp.int32, (tq, LANES), 1)
    low = lane < DH_B
    sel_lo = jnp.where(low, 1.0, 0.0).astype(BF16)
    sel_hi = jnp.where(low, 0.0, 1.0).astype(BF16)
    dmask = _chunk_mask(tq)

    wi = wi_ref[...]
    wcol = [jnp.sum(jnp.where(lane == h, wi, 0.0), axis=-1, keepdims=True) for h in range(H_IDX)]
    q_idx = []
    for pr in range(H_IDX // 2):
        blk = qi_ref[:, LANES * pr:LANES * (pr + 1)]
        q_idx += [blk * sel_lo, blk * sel_hi]

    def idx_score(ki2):
        sc = None
        for h in range(H_IDX):
            term = jnp.maximum(_dot_nt(q_idx[h], ki2), 0.0) * wcol[h]
            sc = term if sc is None else sc + term
        return sc

    def score_body(kb, _):
        r0 = pl.multiple_of(kb * tk, tk)
        sc_scr[kb] = idx_score(ki_scr[pl.ds(r0, tk), :])
        return 0

    lax.fori_loop(0, n_full, score_body, 0)
    scd_scr[...] = jnp.where(dmask, idx_score(ki_scr[pl.ds(d0, tq), :]), -jnp.inf)

    def fold(tile_fn, diag_fn, comb, reduce, init):
        def body(kb, acc):
            for j in range(n_tiles):
                acc = comb(acc, tile_fn(sc_scr[kb, :, LANES * j:LANES * (j + 1)], kb * tk + LANES * j))
            return acc
        acc = lax.fori_loop(0, n_full, body, jnp.full((tq, LANES), init, F32))
        return comb(reduce(acc, axis=-1, keepdims=True), reduce(diag_fn(scd_scr[...]), axis=-1, keepdims=True))

    def wide(v):
        return jnp.broadcast_to(v, (tq, LANES))

    def count_ge(thr):
        thr_w = wide(thr)
        return fold(lambda s, _: jnp.where(s >= thr_w, 1.0, 0.0), lambda s: jnp.where(s >= thr, 1.0, 0.0),
                    jnp.add, jnp.sum, 0.0)

    def count_gt(thr):
        thr_w = wide(thr)
        return fold(lambda s, _: jnp.where(s > thr_w, 1.0, 0.0), lambda s: jnp.where(s > thr, 1.0, 0.0),
                    jnp.add, jnp.sum, 0.0)

    def min_ge(thr):
        thr_w = wide(thr)
        return fold(lambda s, _: jnp.where(s >= thr_w, s, jnp.inf), lambda s: jnp.where(s >= thr, s, jnp.inf),
                    jnp.minimum, jnp.min, jnp.inf)

    row_max = fold(lambda s, _: s, lambda s: s, jnp.maximum, jnp.max, -jnp.inf)
    row_min = fold(lambda s, _: s, lambda s: jnp.where(s > -jnp.inf, s, jnp.inf), jnp.minimum, jnp.min, jnp.inf)

    def bisect(_, lh):
        lo, hi = lh
        mid = lo + 0.5 * (hi - lo)
        ge = count_ge(mid) >= k_sel
        return jnp.where(ge, mid, lo), jnp.where(ge, hi, mid)

    def settle(lo):
        thr = min_ge(lo)
        n_gt = count_gt(thr)
        return thr, n_gt, jnp.max(jnp.where(n_gt >= k_sel, 1, 0))

    hi0 = jnp.where(row_max > 0, 2.0 * row_max, 0.5 * row_max) + 1.0
    lo, hi = lax.fori_loop(0, 16, bisect, (row_min, hi0))
    thr, n_gt, bad = settle(lo)

    def more(state):
        lo, hi, _, _, _, it = state
        lo, hi = lax.fori_loop(0, 4, bisect, (lo, hi))
        thr, n_gt, bad = settle(lo)
        return lo, hi, thr, n_gt, bad, it + 1

    _, _, thr, n_gt, _, _ = lax.while_loop(lambda st: (st[4] > 0) & (st[5] < 64), more,
                                           (lo, hi, thr, n_gt, bad, 0))

    n_ge = count_ge(thr)
    need = k_sel - n_gt
    excess = n_ge > k_sel
    thr_w = wide(thr)

    def cut_search(_):
        def count_tied_upto(j):
            j_w = wide(j)
            def tile(s, base):
                idx = base + lane
                return jnp.where((s == thr_w) & (idx <= j_w), 1.0, 0.0)
            def diag(s):
                idx = d0 + lax.broadcasted_iota(jnp.int32, (tq, tq), 1)
                return jnp.where((s == thr) & (idx <= j), 1.0, 0.0)
            return fold(tile, diag, jnp.add, jnp.sum, 0.0)

        def step(_, jj):
            jlo, jhi = jj
            mid = (jlo + jhi) >> 1
            ok = count_tied_upto(mid) >= need
            return jnp.where(ok, jlo, mid), jnp.where(ok, mid, jhi)

        steps = int(math.ceil(math.log2(l_len))) + 1
        _, jhi = lax.fori_loop(0, steps, step,
                               (jnp.full((tq, 1), -1, jnp.int32), jnp.full((tq, 1), l_len - 1, jnp.int32)))
        return jnp.where(excess, jhi, l_len)

    cut = lax.cond(jnp.max(jnp.where(excess, 1, 0)) > 0, cut_search,
                   lambda _: jnp.full((tq, 1), l_len, jnp.int32), 0)
    cut_w = wide(cut)

    def picked_tile(s, base):
        return (s > thr_w) | ((s == thr_w) & (base + lane <= cut_w))

    idx_d = d0 + lax.broadcasted_iota(jnp.int32, (tq, tq), 1)
    sd = scd_scr[...]
    picked_d = (sd > thr) | ((sd == thr) & (idx_d <= cut))

    qs = []
    for pair in range(H_B // 2):
        blk = q_ref[:, LANES * pair:LANES * (pair + 1)]
        qs += [blk * sel_lo, blk * sel_hi]

    def attend(r0, n, bias, carries):
        kk, vv = kk_scr[pl.ds(r0, n), :], vv_scr[pl.ds(r0, n), :]
        return tuple(_softmax_step(_dot_nt(qs[h], kk) + bias, vv, carries[h]) for h in range(H_B))

    def body(kb, carries):
        bias = jnp.concatenate(
            [jnp.where(picked_tile(sc_scr[kb, :, LANES * j:LANES * (j + 1)], kb * tk + LANES * j), 0.0, NEG_INF)
             for j in range(n_tiles)], axis=1)
        return attend(pl.multiple_of(kb * tk, tk), tk, bias, carries)

    carries = lax.fori_loop(0, n_full, body, tuple(_softmax_init(tq) for _ in range(H_B)))
    carries = attend(d0, tq, jnp.where(picked_d, 0.0, NEG_INF), carries)
    outs = [acc / l for _, l, acc in carries]
    for pair in range(H_B // 2):
        o_ref[:, LANES * pair:LANES * (pair + 1)] = jnp.where(low, outs[2 * pair], outs[2 * pair + 1]).astype(BF16)


def _dsa(q, qidx, wi, rows, past, layer, tq, tk):
    b, t, _ = q.shape
    p_len = 0 if past is None else past.shape[2]
    l_len = p_len + t
    n_sel = min(TOPK_MAX, l_len // 4)
    n_blocks = max((l_len - tq) // tk, 1)
    args = [q, qidx, wi, rows] + ([past] if past is not None else [])
    return pl.pallas_call(
        functools.partial(_dsa_kernel, p_len=p_len, t_len=t, tq=tq, tk=tk, n_sel=n_sel),
        grid=(b, t // tq),
        in_specs=_attn_specs((b, t), tq, [256, 512, LANES], 192, past, layer),
        out_specs=_out_spec(tq),
        out_shape=jax.ShapeDtypeStruct((b, t, 256), BF16),
        scratch_shapes=[pltpu.VMEM((l_len, LANES), BF16)] * 3
                       + [pltpu.VMEM((n_blocks, tq, tk), F32), pltpu.VMEM((tq, tq), F32)],
        compiler_params=_params("parallel", "arbitrary"),
        name="dsa",
    )(*args)


def _gather_cols(w, src, scale):
    src = [int(s) for s in src]
    scale = [float(c) if s >= 0 else 0.0 for s, c in zip(src, scale)]
    pieces, i = [], 0
    while i < len(src):
        j = i + 1
        while j < len(src) and scale[j] == scale[i] and (src[j] == src[j - 1] + 1 if src[i] >= 0 else src[j] < 0):
            j += 1
        if src[i] < 0:
            pieces.append(jnp.zeros((w.shape[0], j - i), w.dtype))
        else:
            run = w[:, src[i]:src[i] + j - i]
            pieces.append(run if scale[i] == 1.0 else run * scale[i])
        i = j
    return jnp.concatenate(pieces, axis=1)


class _Cols:
    def __init__(self):
        self.src, self.scale, self.rot_src, self.rot_sign, self.rope = [], [], [], [], []

    def plain(self, start, n, scale=1.0):
        for i in range(n):
            self._add(start + i, scale, -1, 0.0, None)

    def pad(self, n):
        for _ in range(n):
            self._add(-1, 0.0, -1, 0.0, None)

    def roped(self, start, d, scale=1.0):
        half = d // 2
        for i in range(d):
            partner, sign = (start + i + half, -1.0) if i < half else (start + i - half, 1.0)
            self._add(start + i, scale, partner, sign * scale, (d, i % half))

    def _add(self, src, scale, rot_src, rot_sign, rope):
        self.src.append(src)
        self.scale.append(scale)
        self.rot_src.append(rot_src)
        self.rot_sign.append(rot_sign)
        self.rope.append(rope)

    def weights(self, w):
        return (_gather_cols(w, self.src, self.scale).astype(BF16),
                _gather_cols(w, self.rot_src, self.rot_sign).astype(BF16))

    def tables(self, pos):
        cs = {}
        for d in sorted({r[0] for r in self.rope if r is not None}):
            freqs = ROPE_THETA ** (-jnp.arange(d // 2, dtype=F32) * 2.0 / d)
            ang = pos.astype(F32)[:, None] * freqs[None, :]
            cs[d] = (jnp.cos(ang), jnp.sin(ang))
        cos, sin, i, n = [], [], 0, pos.shape[0]
        while i < len(self.rope):
            r, j = self.rope[i], i + 1
            if r is None:
                while j < len(self.rope) and self.rope[j] is None:
                    j += 1
                cos.append(jnp.ones((n, j - i), F32))
                sin.append(jnp.zeros((n, j - i), F32))
            else:
                while j < len(self.rope) and self.rope[j] == (r[0], r[1] + j - i):
                    j += 1
                cos.append(cs[r[0]][0][:, r[1]:r[1] + j - i])
                sin.append(cs[r[0]][1][:, r[1]:r[1] + j - i])
            i = j
        return jnp.concatenate(cos, axis=1), jnp.concatenate(sin, axis=1)


def _plans():
    p = _Cols()
    p.plain(O_CQ, Q_LORA)
    p.plain(O_CKV, KV_LORA)
    for h in range(H_C):
        p.plain(O_KC + DH_C * h, DH_C)
        p.plain(O_VC + DH_C * h, DH_C)
    for h in range(H_C):
        p.plain(O_QC + DH_C * h, DH_C, DH_C ** -0.5)
        p.pad(LANES - DH_C)
    p.plain(O_WIB, H_IDX)
    p.pad(LANES - H_IDX)
    assert len(p.src) == P_END

    r = _Cols()
    r.roped(O_KB, DH_B)
    r.plain(O_VB, DH_B)
    r.roped(O_KIB, D_IDX)
    r.pad(R_DIFF - (2 * DH_B + D_IDX))
    for h in range(H_D):
        r.roped(O_KD + 2 * DQK_D * h, DQK_D)
        r.roped(O_KD + 2 * DQK_D * h + DQK_D, DQK_D)
        r.plain(O_VD + DV_D * h, DV_D)
    r.roped(O_KR, ROPE_A)
    r.pad(LANES - ROPE_A)
    for h in range(H_B):
        r.roped(O_QB + DH_B * h, DH_B, DH_B ** -0.5)
    for h in range(H_IDX):
        r.roped(O_QIB + D_IDX * h, D_IDX)
    for h in range(H_D):
        r.roped(O_QD + 2 * DQK_D * h, DQK_D)
        r.roped(O_QD + 2 * DQK_D * h + DQK_D, DQK_D)
        r.pad(LANES - 2 * DQK_D)
    assert len(r.src) == R_END

    q = _Cols()
    for h in range(H_A):
        q.plain((NOPE_A + ROPE_A) * h, NOPE_A)
        q.roped((NOPE_A + ROPE_A) * h + NOPE_A, ROPE_A)
        q.pad(LANES - NOPE_A - ROPE_A)
    return p, r, q


def _layer_weights(l, prm, plans):
    p, r, q = plans
    w_in = prm["w_in"][l]
    wr, wrr = r.weights(w_in)
    wqb, wqbr = q.weights(prm["mla_w_qb"][l])
    wkvb = prm["mla_w_kvb"][l]
    k_src, v_src = [], []
    for h in range(H_A):
        k_src += list(range((NOPE_A + V_A) * h, (NOPE_A + V_A) * h + NOPE_A)) + [-1] * (LANES - NOPE_A)
        v_src += list(range((NOPE_A + V_A) * h + NOPE_A, (NOPE_A + V_A) * (h + 1)))
    place = np.zeros((ROPE_A, H_A * LANES), np.float32)
    for h in range(H_A):
        place[np.arange(ROPE_A), LANES * h + NOPE_A + np.arange(ROPE_A)] = 1.0
    dn = jnp.concatenate([jnp.zeros((2 * DQK_D,), F32), prm["diff_norm"][l]])[None, :]

    def ff(tag):
        return (prm[f"norm_{tag}"][l][None, :], prm[f"w_{tag}_gate"][l].astype(BF16),
                prm[f"w_{tag}_up"][l].astype(BF16), prm[f"w_{tag}_down"][l].astype(BF16))

    return {
        "ff1": ff("ff1"), "ff2": ff("ff2"),
        "nm": prm["norm_mix"][l][None, :],
        "wp": p.weights(w_in)[0], "wr": wr, "wrr": wrr,
        "qn": prm["mla_q_norm"][l][None, :], "wqb": wqb, "wqbr": wqbr,
        "kvn": prm["mla_kv_norm"][l][None, :],
        "wk": _gather_cols(wkvb, k_src, np.ones(len(k_src))).astype(BF16),
        "pk": jnp.asarray(place).astype(BF16),
        "wv": _gather_cols(wkvb, v_src, np.ones(len(v_src))).astype(BF16),
        "lam": prm["diff_lambda"][l], "dn": dn,
        "wo": prm["w_out"][l].astype(BF16),
    }


def _trunk(x, caches, pos, layers, plans, fnorm, tq, tk, proj_fold):
    b, t, _ = x.shape
    depth = len(layers)
    tm = min(512, b * t)
    _, r_plan, q_plan = plans
    pb, pt = (b // proj_fold, t * proj_fold)
    pos_p = jnp.tile(pos, proj_fold)
    tabs = r_plan.tables(pos_p) + q_plan.tables(pos_p)
    tq_p = min(256, pt)
    xf = x.reshape(b * t, D_MODEL)
    rows = [[], [], [], []]
    for l, lw in enumerate(layers):
        lam_init = 0.8 - 0.6 * math.exp(-0.3 * l)
        xf = _ffn(xf, None, None, *lw["ff1"], None, tm)
        outs = _proj(xf.reshape(pb, pt, D_MODEL), lw, tabs, tq_p)
        r_mla, r_dsa, r_sb, r_diff, q_mla, q_dsa, q_idx, w_idx, q_sb, q_diff = [
            o.reshape(b, t, o.shape[-1]) for o in outs]
        past = [None] * 4 if caches is None else caches
        mix = (
            _mla(q_mla, r_mla, past[0], l, lw, tq, tk),
            _dsa(q_dsa, q_idx, w_idx, r_dsa, past[1], l, tq, tk),
            _sb(q_sb, r_sb, past[2], l, tq, tk),
            _diff(q_diff, r_diff, past[3], l, lw, lam_init, tq, tk),
        )
        mix = [m.reshape(b * t, 256) for m in mix]
        xf = _ffn(xf, mix, lw["wo"], *lw["ff2"], fnorm if l == depth - 1 else None, tm)
        for acc, r in zip(rows, (r_mla, r_dsa, r_sb, r_diff)):
            acc.append(r)
    y = xf.reshape(b, t, D_MODEL)
    return (y, jnp.stack(rows[0]), jnp.stack(rows[1]),
            jnp.stack(rows[2]).reshape(depth, b, t, H_C, 2 * DH_C),
            jnp.stack(rows[3]).reshape(depth, b, t, H_D, 2 * DQK_D + DV_D))


def kernel(x_prompt, x_sample, cache_mla, cache_dsa, cache_sb, cache_diff, norm_ff1, w_ff1_gate, w_ff1_up, w_ff1_down, norm_mix, w_in, mla_q_norm, mla_w_qb, mla_kv_norm, mla_w_kvb, diff_lambda, diff_norm, w_out, norm_ff2, w_ff2_gate, w_ff2_up, w_ff2_down, final_norm):
    prm = dict(norm_ff1=norm_ff1, w_ff1_gate=w_ff1_gate, w_ff1_up=w_ff1_up, w_ff1_down=w_ff1_down,
               norm_mix=norm_mix, w_in=w_in, mla_q_norm=mla_q_norm, mla_w_qb=mla_w_qb,
               mla_kv_norm=mla_kv_norm, mla_w_kvb=mla_w_kvb, diff_lambda=diff_lambda, diff_norm=diff_norm,
               w_out=w_out, norm_ff2=norm_ff2, w_ff2_gate=w_ff2_gate, w_ff2_up=w_ff2_up, w_ff2_down=w_ff2_down)
    depth = w_in.shape[0]
    plans = _plans()
    layers = [_layer_weights(l, prm, plans) for l in range(depth)]
    fnorm = final_norm[None, :]

    t_p = x_prompt.shape[1]
    assert t_p % CHUNK == 0
    tq_p = min(256, t_p)
    out_p = _trunk(x_prompt, None, jnp.arange(t_p, dtype=jnp.int32), layers, plans, fnorm, tq_p, tq_p, 1)


    b_s, t_s, _ = x_sample.shape
    past = cache_mla.shape[2]
    assert past % CHUNK == 0 and t_s <= CHUNK, "new sample frames must sit in one chunk after whole cached chunks"
    tk_s = math.gcd(past, 512)
    caches = (cache_mla, cache_dsa,
              cache_sb.reshape(cache_sb.shape[:3] + (-1,)), cache_diff.reshape(cache_diff.shape[:3] + (-1,)))
    fold = math.gcd(b_s, max(1, 256 // t_s))
    out_s = _trunk(x_sample, caches, past + jnp.arange(t_s, dtype=jnp.int32), layers, plans, fnorm, t_s, tk_s, fold)

    return (out_p[0], out_s[0]) + tuple(out_p[1:]) + tuple(out_s[1:])
```

```python
import functools
import math

import numpy as np
import jax
import jax.numpy as jnp
from jax import lax
from jax.experimental import pallas as pl
from jax.experimental.pallas import tpu as pltpu

F32 = jnp.float32
BF16 = jnp.bfloat16

D_MODEL = 1024
CHUNK = 64
ROPE_THETA = 10000.0
EPS = 1e-6
NEG_INF = -1e30
HEAD_DIM = D_MODEL // 16
D_FF = 11 * D_MODEL // 4
H_A = 4
Q_LORA = 4 * HEAD_DIM
KV_LORA = 2 * HEAD_DIM
NOPE_A = HEAD_DIM
ROPE_A = HEAD_DIM // 2
V_A = HEAD_DIM
H_B = 4
DH_B = HEAD_DIM
H_IDX = 8
D_IDX = HEAD_DIM
TOPK_MAX = 256
H_C = 4
DH_C = HEAD_DIM
H_D = 4
DQK_D = HEAD_DIM // 2
DV_D = HEAD_DIM
COL_SIZES = (Q_LORA, KV_LORA, ROPE_A,
             H_B * DH_B, DH_B, DH_B, H_IDX * D_IDX, D_IDX, H_IDX,
             H_C * DH_C, H_C * DH_C, H_C * DH_C,
             H_D * 2 * DQK_D, H_D * 2 * DQK_D, H_D * DV_D)
(O_CQ, O_CKV, O_KR, O_QB, O_KB, O_VB, O_QIB, O_KIB, O_WIB,
 O_QC, O_KC, O_VC, O_QD, O_KD, O_VD) = np.concatenate([[0], np.cumsum(COL_SIZES)[:-1]]).tolist()

LANES = 128
VMEM_LIMIT = 56 * 1024 * 1024
FF_CHUNK = 256
ROW_CHUNK = 512

P_CQ, P_CKV, P_SB, P_QSB, P_WI, P_END = 0, 256, 384, 896, 1408, 1536
R_DSA, R_DIFF, R_KR, R_QDSA, R_QI, R_QDIFF, R_END = 0, 256, 768, 896, 1152, 1664, 2176


def _dot(a, b):
    return jnp.dot(a, b, preferred_element_type=F32)


def _dot_nt(a, b):
    return lax.dot_general(a, b, (((1,), (1,)), ((), ())), preferred_element_type=F32)


def _rms(x, g):
    return x * lax.rsqrt(jnp.mean(x * x, axis=-1, keepdims=True) + EPS) * g


def _const_spec(shape):
    n = len(shape)
    return pl.BlockSpec(shape, lambda *_: (0,) * n, pipeline_mode=pl.Buffered(1))


def _params(*sem):
    return pltpu.CompilerParams(dimension_semantics=sem, vmem_limit_bytes=VMEM_LIMIT)


def _ffn_kernel(*refs, has_mix, has_final):
    it = iter(refs)
    x_ref = next(it)
    if has_mix:
        o_refs = [next(it) for _ in range(4)]
        wo_ref = next(it)
    g_ref, wg_ref, wu_ref, wd_ref = next(it), next(it), next(it), next(it)
    fn_ref = next(it) if has_final else None
    out_ref = next(it)

    x = x_ref[...]
    if has_mix:
        for i, o_ref in enumerate(o_refs):
            x = x + _dot(o_ref[...], wo_ref[256 * i:256 * (i + 1), :])
    xn = _rms(x, g_ref[...]).astype(BF16)
    acc = jnp.zeros_like(x)
    for c0 in range(0, D_FF, FF_CHUNK):
        gate = _dot(xn, wg_ref[:, c0:c0 + FF_CHUNK])
        up = _dot(xn, wu_ref[:, c0:c0 + FF_CHUNK])
        hid = (gate * jax.nn.sigmoid(gate) * up).astype(BF16)
        acc = acc + _dot(hid, wd_ref[c0:c0 + FF_CHUNK, :])
    y = x + 0.5 * acc
    if has_final:
        y = _rms(y, fn_ref[...])
    out_ref[...] = y


def _ffn(x, mix, wo, g, wg, wu, wd, fnorm, tm):
    n = x.shape[0]
    has_mix, has_final = mix is not None, fnorm is not None
    row = lambda w: pl.BlockSpec((tm, w), lambda i: (i, 0))
    args, specs = [x], [row(D_MODEL)]
    if has_mix:
        args += list(mix) + [wo]
        specs += [row(256)] * 4 + [_const_spec(wo.shape)]
    args += [g, wg, wu, wd]
    specs += [_const_spec(g.shape), _const_spec(wg.shape), _const_spec(wu.shape), _const_spec(wd.shape)]
    if has_final:
        args.append(fnorm)
        specs.append(_const_spec(fnorm.shape))
    return pl.pallas_call(
        functools.partial(_ffn_kernel, has_mix=has_mix, has_final=has_final),
        grid=(n // tm,),
        in_specs=specs,
        out_specs=row(D_MODEL),
        out_shape=jax.ShapeDtypeStruct((n, D_MODEL), F32),
        compiler_params=_params("parallel"),
        name="ffn",
    )(*args)


def _proj_kernel(x_ref, g_ref, wp_ref, wr_ref, wrr_ref, cr_ref, sr_ref,
                 qn_ref, wqb_ref, wqbr_ref, cq_ref, sq_ref, kvn_ref,
                 rmla_ref, rdsa_ref, rsb_ref, rdiff_ref,
                 qmla_ref, qdsa_ref, qi_ref, wi_ref, qsb_ref, qdiff_ref):
    xn = _rms(x_ref[...], g_ref[...]).astype(BF16)

    def plain(a, b):
        return _dot(xn, wp_ref[:, a:b])

    def roped(a, b):
        return (_dot(xn, wr_ref[:, a:b]) * cr_ref[:, a:b]
                + _dot(xn, wrr_ref[:, a:b]) * sr_ref[:, a:b])

    cqn = _rms(plain(P_CQ, P_CKV), qn_ref[...]).astype(BF16)
    qmla_ref[...] = (_dot(cqn, wqb_ref[...]) * cq_ref[...]
                     + _dot(cqn, wqbr_ref[...]) * sq_ref[...]).astype(BF16)
    rmla_ref[:, :KV_LORA] = _rms(plain(P_CKV, P_SB), kvn_ref[...])
    rmla_ref[:, KV_LORA:] = roped(R_KR, R_QDSA)[:, :ROPE_A]
    rdsa_ref[...] = roped(R_DSA, R_DIFF)[:, :2 * DH_B + D_IDX]
    qdsa_ref[...] = roped(R_QDSA, R_QI).astype(BF16)
    qi_ref[...] = roped(R_QI, R_QDIFF).astype(BF16)
    wi_ref[...] = plain(P_WI, P_END) * (H_IDX ** -0.5)
    rsb_ref[...] = plain(P_SB, P_QSB)
    qsb_ref[...] = plain(P_QSB, P_WI).astype(BF16)
    rdiff_ref[...] = roped(R_DIFF, R_KR)
    qdiff_ref[...] = roped(R_QDIFF, R_END).astype(BF16)


def _proj(x, lw, tabs, tq):
    b, t, _ = x.shape
    cr, sr, cq, sq = tabs
    tok = lambda w: pl.BlockSpec((None, tq, w), lambda bi, i: (bi, i, 0))
    tab = lambda w: pl.BlockSpec((tq, w), lambda bi, i: (i, 0))
    consts = [lw["nm"], lw["wp"], lw["wr"], lw["wrr"]]
    consts2 = [lw["qn"], lw["wqb"], lw["wqbr"]]
    in_specs = ([tok(D_MODEL)] + [_const_spec(a.shape) for a in consts] + [tab(R_END), tab(R_END)]
                + [_const_spec(a.shape) for a in consts2] + [tab(512), tab(512), _const_spec(lw["kvn"].shape)])
    widths = [(160, F32), (192, F32), (512, F32), (512, F32),
              (512, BF16), (256, BF16), (512, BF16), (LANES, F32), (512, BF16), (512, BF16)]
    return pl.pallas_call(
        _proj_kernel,
        grid=(b, t // tq),
        in_specs=in_specs,
        out_specs=[tok(w) for w, _ in widths],
        out_shape=[jax.ShapeDtypeStruct((b, t, w), dt) for w, dt in widths],
        compiler_params=_params("parallel", "parallel"),
        name="proj",
    )(x, *consts, cr, sr, *consts2, cq, sq, lw["kvn"])


def _chunk_mask(tq):
    ri = lax.broadcasted_iota(jnp.int32, (tq, tq), 0)
    ci = lax.broadcasted_iota(jnp.int32, (tq, tq), 1)
    return (ci // CHUNK) <= (ri // CHUNK)


def _softmax_step(s, v, carry):
    m, l, acc = carry
    m_new = jnp.maximum(m, jnp.max(s, axis=-1, keepdims=True))
    alpha = jnp.exp(m - m_new)
    p = jnp.exp(s - m_new)
    l = alpha * l + jnp.sum(p, axis=-1, keepdims=True)
    acc = alpha * acc + _dot(p.astype(BF16), v)
    return m_new, l, acc


def _softmax_init(tq):
    return (jnp.full((tq, 1), NEG_INF, F32), jnp.zeros((tq, 1), F32), jnp.zeros((tq, LANES), F32))


def _lane_lt(tq, n):
    return lax.broadcasted_iota(jnp.int32, (tq, LANES), 1) < n


def _attn_specs(b_t, tq, q_widths, rows_w, past, layer):
    b, t = b_t
    specs = [pl.BlockSpec((None, tq, w), lambda bi, i: (bi, i, 0)) for w in q_widths]
    specs.append(pl.BlockSpec((None, t, rows_w), lambda bi, i: (bi, 0, 0)))
    if past is not None:
        specs.append(pl.BlockSpec((None, None, past.shape[2], rows_w), lambda bi, i: (layer, bi, 0, 0)))
    return specs


def _out_spec(tq):
    return pl.BlockSpec((None, tq, 256), lambda bi, i: (bi, i, 0))


def _mla_kernel(*refs, p_len, t_len, tq, tk):
    if p_len:
        q_ref, new_ref, past_ref, wk_ref, pk_ref, wv_ref, o_ref, k_scr, v_scr = refs
    else:
        q_ref, new_ref, wk_ref, pk_ref, wv_ref, o_ref, k_scr, v_scr = refs
        past_ref = None
    qi = pl.program_id(1)
    scale = (NOPE_A + ROPE_A) ** -0.5

    @pl.when(qi == 0)
    def _fill():
        def fill(src_ref, dst, n):
            for r0 in range(0, n, ROW_CHUNK):
                rc = min(ROW_CHUNK, n - r0)
                lat = src_ref[r0:r0 + rc, :KV_LORA].astype(BF16)
                rope = src_ref[r0:r0 + rc, KV_LORA:].astype(BF16)
                k_scr[dst + r0:dst + r0 + rc, :] = (_dot(lat, wk_ref[...]) + _dot(rope, pk_ref[...])).astype(BF16)
                v_scr[dst + r0:dst + r0 + rc, :] = _dot(lat, wv_ref[...]).astype(BF16)
        if p_len:
            fill(past_ref, 0, p_len)
        fill(new_ref, p_len, t_len)

    d0 = pl.multiple_of(p_len + qi * tq, tq)
    n_full = (p_len + qi * tq) // tk
    dmask = _chunk_mask(tq)
    low = _lane_lt(tq, V_A)
    qs = [q_ref[:, LANES * h:LANES * (h + 1)] for h in range(H_A)]

    def step(r0, n, mask, carries):
        out = []
        for h in range(H_A):
            s = _dot_nt(qs[h], k_scr[pl.ds(r0, n), LANES * h:LANES * (h + 1)]) * scale
            if mask is not None:
                s = jnp.where(mask, s, NEG_INF)
            out.append(_softmax_step(s, v_scr[pl.ds(r0, n), LANES * (h // 2):LANES * (h // 2 + 1)], carries[h]))
        return tuple(out)

    carries = lax.fori_loop(0, n_full, lambda kb, c: step(pl.multiple_of(kb * tk, tk), tk, None, c),
                            tuple(_softmax_init(tq) for _ in range(H_A)))
    carries = step(d0, tq, dmask, carries)
    outs = [acc / l for _, l, acc in carries]
    for pair in range(H_A // 2):
        o_ref[:, LANES * pair:LANES * (pair + 1)] = jnp.where(low, outs[2 * pair], outs[2 * pair + 1]).astype(BF16)


def _mla(q, rows, past, layer, lw, tq, tk):
    b, t, _ = q.shape
    p_len = 0 if past is None else past.shape[2]
    consts = [lw["wk"], lw["pk"], lw["wv"]]
    args = [q, rows] + ([past] if past is not None else []) + consts
    return pl.pallas_call(
        functools.partial(_mla_kernel, p_len=p_len, t_len=t, tq=tq, tk=tk),
        grid=(b, t // tq),
        in_specs=_attn_specs((b, t), tq, [512], 160, past, layer) + [_const_spec(a.shape) for a in consts],
        out_specs=_out_spec(tq),
        out_shape=jax.ShapeDtypeStruct((b, t, 256), BF16),
        scratch_shapes=[pltpu.VMEM((p_len + t, 512), BF16), pltpu.VMEM((p_len + t, 256), BF16)],
        compiler_params=_params("parallel", "arbitrary"),
        name="mla",
    )(*args)


def _diff_kernel(*refs, p_len, t_len, tq, tk, lam_init):
    if p_len:
        q_ref, new_ref, past_ref, lam_ref, dn_ref, o_ref = refs
    else:
        q_ref, new_ref, lam_ref, dn_ref, o_ref = refs
        past_ref = None
    qi = pl.program_id(1)
    scale = DQK_D ** -0.5
    lam = lam_ref[...]
    lam_full = (jnp.exp(jnp.sum(lam[0:1] * lam[1:2], axis=-1, keepdims=True))
                - jnp.exp(jnp.sum(lam[2:3] * lam[3:4], axis=-1, keepdims=True)) + lam_init)
    dmask = _chunk_mask(tq)
    lane = lax.broadcasted_iota(jnp.int32, (tq, LANES), 1)
    sel1 = jnp.where(lane < DQK_D, 1.0, 0.0).astype(BF16)
    sel2 = jnp.where((lane >= DQK_D) & (lane < 2 * DQK_D), 1.0, 0.0).astype(BF16)
    is_v = lane >= 2 * DQK_D
    n_past = p_len // tk
    n_new = (qi * tq) // tk

    q12 = []
    for h in range(H_D):
        qh = q_ref[:, LANES * h:LANES * (h + 1)]
        q12 += [qh * sel1, qh * sel2]

    def step(src_ref, r0, n, mask, carries):
        out = []
        for h in range(H_D):
            kv = src_ref[pl.ds(r0, n), LANES * h:LANES * (h + 1)].astype(BF16)
            for m in range(2):
                s = _dot_nt(q12[2 * h + m], kv) * scale
                if mask is not None:
                    s = jnp.where(mask, s, NEG_INF)
                out.append(_softmax_step(s, kv, carries[2 * h + m]))
        return tuple(out)

    def body(src_ref, kb, carries):
        return step(src_ref, pl.multiple_of(kb * tk, tk), tk, None, carries)

    carries = tuple(_softmax_init(tq) for _ in range(2 * H_D))
    if p_len:
        carries = lax.fori_loop(0, n_past, functools.partial(body, past_ref), carries)
    carries = lax.fori_loop(0, n_new, functools.partial(body, new_ref), carries)
    carries = step(new_ref, pl.multiple_of(qi * tq, tq), tq, dmask, carries)
    outs = []
    for h in range(H_D):
        (_, l1, a1), (_, l2, a2) = carries[2 * h], carries[2 * h + 1]
        o = a1 / l1 - lam_full * (a2 / l2)
        ms = jnp.sum(jnp.where(is_v, o * o, 0.0), axis=-1, keepdims=True) * (1.0 / DV_D)
        outs.append(o * lax.rsqrt(ms + EPS) * dn_ref[...] * (1.0 - lam_init))
    for pair in range(H_D // 2):
        o_ref[:, LANES * pair:LANES * (pair + 1)] = jnp.where(
            is_v, outs[2 * pair + 1], pltpu.roll(outs[2 * pair], 2 * DQK_D, axis=1)).astype(BF16)


def _diff(q, rows, past, layer, lw, lam_init, tq, tk):
    b, t, _ = q.shape
    p_len = 0 if past is None else past.shape[2]
    consts = [lw["lam"], lw["dn"]]
    args = [q, rows] + ([past] if past is not None else []) + consts
    return pl.pallas_call(
        functools.partial(_diff_kernel, p_len=p_len, t_len=t, tq=tq, tk=tk, lam_init=lam_init),
        grid=(b, t // tq),
        in_specs=_attn_specs((b, t), tq, [512], 512, past, layer) + [_const_spec(a.shape) for a in consts],
        out_specs=_out_spec(tq),
        out_shape=jax.ShapeDtypeStruct((b, t, 256), BF16),
        compiler_params=_params("parallel", "parallel"),
        name="diff",
    )(*args)


def _sb_kernel(*refs, p_len, t_len, tq, tk):
    if p_len:
        q_ref, new_ref, past_ref, o_ref = refs
    else:
        q_ref, new_ref, o_ref = refs
        past_ref = None
    qi = pl.program_id(1)
    n_past = p_len // tk
    n_new = (qi * tq) // tk

    def later(n):
        ri = lax.broadcasted_iota(jnp.int32, (n, n), 0)
        ci = lax.broadcasted_iota(jnp.int32, (n, n), 1)
        return jnp.where(ri > ci, 1.0, 0.0).astype(BF16)

    later_k = later(tk)
    later_q = later_k if tq == tk else later(tq)
    ri = lax.broadcasted_iota(jnp.int32, (tq, tq), 0)
    ci = lax.broadcasted_iota(jnp.int32, (tq, tq), 1)
    causal = ci < ri
    is_v = lax.broadcasted_iota(jnp.int32, (tq, LANES), 1) >= DH_C

    qs = [q_ref[:, LANES * h:LANES * (h + 1)] for h in range(H_C)]

    def step(src_ref, r0, n, mask, tri, carries):
        out = []
        for h in range(H_C):
            run, acc = carries[h]
            kv = src_ref[pl.ds(r0, n), LANES * h:LANES * (h + 1)].astype(BF16)
            z = _dot_nt(qs[h], kv)
            log_take = jnp.minimum(z, 0.0) - jnp.log1p(jnp.exp(-jnp.abs(z)))
            log_keep = log_take - z
            if mask is not None:
                log_keep = jnp.where(mask, log_keep, 0.0)
            hi = log_keep.astype(BF16)
            lo = (log_keep - hi.astype(F32)).astype(BF16)
            log_after = _dot(hi, tri) + _dot(lo, tri) + run
            a = jnp.exp(log_take + log_after)
            if mask is not None:
                a = jnp.where(mask, a, 0.0)
            acc = acc + _dot(a.astype(BF16), kv)
            run = run + jnp.sum(log_keep, axis=-1, keepdims=True)
            out.append((run, acc))
        return tuple(out)

    carries = tuple((jnp.zeros((tq, 1), F32), jnp.zeros((tq, LANES), F32)) for _ in range(H_C))
    carries = step(new_ref, pl.multiple_of(qi * tq, tq), tq, causal, later_q, carries)

    def body(src_ref, n, i, carries):
        return step(src_ref, pl.multiple_of((n - 1 - i) * tk, tk), tk, None, later_k, carries)

    carries = lax.fori_loop(0, n_new, functools.partial(body, new_ref, n_new), carries)
    if p_len:
        carries = lax.fori_loop(0, n_past, functools.partial(body, past_ref, n_past), carries)
    for pair in range(H_C // 2):
        o_ref[:, LANES * pair:LANES * (pair + 1)] = jnp.where(
            is_v, carries[2 * pair + 1][1], pltpu.roll(carries[2 * pair][1], DH_C, axis=1)).astype(BF16)


def _sb(q, rows, past, layer, tq, tk):
    b, t, _ = q.shape
    p_len = 0 if past is None else past.shape[2]
    args = [q, rows] + ([past] if past is not None else [])
    return pl.pallas_call(
        functools.partial(_sb_kernel, p_len=p_len, t_len=t, tq=tq, tk=tk),
        grid=(b, t // tq),
        in_specs=_attn_specs((b, t), tq, [512], 512, past, layer),
        out_specs=_out_spec(tq),
        out_shape=jax.ShapeDtypeStruct((b, t, 256), BF16),
        compiler_params=_params("parallel", "parallel"),
        name="sb",
    )(*args)


def _dsa_kernel(*refs, p_len, t_len, tq, tk, n_sel):
    if p_len:
        q_ref, qi_ref, wi_ref, new_ref, past_ref, o_ref, kk_scr, vv_scr, ki_scr, sc_scr, scd_scr = refs
    else:
        q_ref, qi_ref, wi_ref, new_ref, o_ref, kk_scr, vv_scr, ki_scr, sc_scr, scd_scr = refs
        past_ref = None
    qi = pl.program_id(1)
    l_len = p_len + t_len
    n_tiles = tk // LANES
    k_sel = float(n_sel)

    @pl.when(qi == 0)
    def _fill():
        r1 = lax.broadcasted_iota(jnp.int32, (LANES, 2 * LANES), 0)
        c1 = lax.broadcasted_iota(jnp.int32, (LANES, 2 * LANES), 1)
        dup_kv = jnp.where((c1 % DH_B == r1 % DH_B) & (c1 // LANES == r1 // DH_B), 1.0, 0.0).astype(BF16)
        r2 = lax.broadcasted_iota(jnp.int32, (D_IDX, LANES), 0)
        c2 = lax.broadcasted_iota(jnp.int32, (D_IDX, LANES), 1)
        dup_ki = jnp.where(c2 % D_IDX == r2, 1.0, 0.0).astype(BF16)

        def fill(src_ref, dst, n):
            for r0 in range(0, n, ROW_CHUNK):
                rc = min(ROW_CHUNK, n - r0)
                kv2 = _dot(src_ref[r0:r0 + rc, :LANES].astype(BF16), dup_kv)
                kk_scr[dst + r0:dst + r0 + rc, :] = kv2[:, :LANES].astype(BF16)
                vv_scr[dst + r0:dst + r0 + rc, :] = kv2[:, LANES:].astype(BF16)
                ki_scr[dst + r0:dst + r0 + rc, :] = _dot(src_ref[r0:r0 + rc, LANES:].astype(BF16), dup_ki).astype(BF16)
        if p_len:
            fill(past_ref, 0, p_len)
        fill(new_ref, p_len, t_len)

    d0 = pl.multiple_of(p_len + qi * tq, tq)
    n_full = (p_len + qi * tq) // tk
    lane = lax.broadcasted_iota(jnp.int32, (tq, LANES), 1)
    low = lane < DH_B
    sel_lo = jnp.where(low, 1.0, 0.0).astype(BF16)
    sel_hi = jnp.where(low, 0.0, 1.0).astype(BF16)
    dmask = _chunk_mask(tq)

    wi = wi_ref[...]
    wcol = [jnp.sum(jnp.where(lane == h, wi, 0.0), axis=-1, keepdims=True) for h in range(H_IDX)]
    q_idx = []
    for pr in range(H_IDX // 2):
        blk = qi_ref[:, LANES * pr:LANES * (pr + 1)]
        q_idx += [blk * sel_lo, blk * sel_hi]

    def idx_score(ki2):
        sc = None
        for h in range(H_IDX):
            term = jnp.maximum(_dot_nt(q_idx[h], ki2), 0.0) * wcol[h]
            sc = term if sc is None else sc + term
        return sc

    def score_body(kb, _):
        r0 = pl.multiple_of(kb * tk, tk)
        sc_scr[kb] = idx_score(ki_scr[pl.ds(r0, tk), :])
        return 0

    lax.fori_loop(0, n_full, score_body, 0)
    scd_scr[...] = jnp.where(dmask, idx_score(ki_scr[pl.ds(d0, tq), :]), -jnp.inf)

    def fold(tile_fn, diag_fn, comb, reduce, init):
        def body(kb, acc):
            for j in range(n_tiles):
                acc = comb(acc, tile_fn(sc_scr[kb, :, LANES * j:LANES * (j + 1)], kb * tk + LANES * j))
            return acc
        acc = lax.fori_loop(0, n_full, body, jnp.full((tq, LANES), init, F32))
        return comb(reduce(acc, axis=-1, keepdims=True), reduce(diag_fn(scd_scr[...]), axis=-1, keepdims=True))

    def wide(v):
        return jnp.broadcast_to(v, (tq, LANES))

    def count_ge(thr):
        thr_w = wide(thr)
        return fold(lambda s, _: jnp.where(s >= thr_w, 1.0, 0.0), lambda s: jnp.where(s >= thr, 1.0, 0.0),
                    jnp.add, jnp.sum, 0.0)

    def count_gt(thr):
        thr_w = wide(thr)
        return fold(lambda s, _: jnp.where(s > thr_w, 1.0, 0.0), lambda s: jnp.where(s > thr, 1.0, 0.0),
                    jnp.add, jnp.sum, 0.0)

    def min_ge(thr):
        thr_w = wide(thr)
        return fold(lambda s, _: jnp.where(s >= thr_w, s, jnp.inf), lambda s: jnp.where(s >= thr, s, jnp.inf),
                    jnp.minimum, jnp.min, jnp.inf)

    row_max = fold(lambda s, _: s, lambda s: s, jnp.maximum, jnp.max, -jnp.inf)
    row_min = fold(lambda s, _: s, lambda s: jnp.where(s > -jnp.inf, s, jnp.inf), jnp.minimum, jnp.min, jnp.inf)

    def bisect(_, lh):
        lo, hi = lh
        mid = lo + 0.5 * (hi - lo)
        ge = count_ge(mid) >= k_sel
        return jnp.where(ge, mid, lo), jnp.where(ge, hi, mid)

    def settle(lo):
        thr = min_ge(lo)
        n_gt = count_gt(thr)
        return thr, n_gt, jnp.max(jnp.where(n_gt >= k_sel, 1, 0))

    hi0 = jnp.where(row_max > 0, 2.0 * row_max, 0.5 * row_max) + 1.0
    lo, hi = lax.fori_loop(0, 16, bisect, (row_min, hi0))
    thr, n_gt, bad = settle(lo)

    def more(state):
        lo, hi, _, _, _, it = state
        lo, hi = lax.fori_loop(0, 4, bisect, (lo, hi))
        thr, n_gt, bad = settle(lo)
        return lo, hi, thr, n_gt, bad, it + 1

    _, _, thr, n_gt, _, _ = lax.while_loop(lambda st: (st[4] > 0) & (st[5] < 64), more,
                                           (lo, hi, thr, n_gt, bad, 0))

    n_ge = count_ge(thr)
    need = k_sel - n_gt
    excess = n_ge > k_sel
    thr_w = wide(thr)

    def cut_search(_):
        def count_tied_upto(j):
            j_w = wide(j)
            def tile(s, base):
                idx = base + lane
                return jnp.where((s == thr_w) & (idx <= j_w), 1.0, 0.0)
            def diag(s):
                idx = d0 + lax.broadcasted_iota(jnp.int32, (tq, tq), 1)
                return jnp.where((s == thr) & (idx <= j), 1.0, 0.0)
            return fold(tile, diag, jnp.add, jnp.sum, 0.0)

        def step(_, jj):
            jlo, jhi = jj
            mid = (jlo + jhi) >> 1
            ok = count_tied_upto(mid) >= need
            return jnp.where(ok, jlo, mid), jnp.where(ok, mid, jhi)

        steps = int(math.ceil(math.log2(l_len))) + 1
        _, jhi = lax.fori_loop(0, steps, step,
                               (jnp.full((tq, 1), -1, jnp.int32), jnp.full((tq, 1), l_len - 1, jnp.int32)))
        return jnp.where(excess, jhi, l_len)

    cut = lax.cond(jnp.max(jnp.where(excess, 1, 0)) > 0, cut_search,
                   lambda _: jnp.full((tq, 1), l_len, jnp.int32), 0)
    cut_w = wide(cut)

    def picked_tile(s, base):
        return (s > thr_w) | ((s == thr_w) & (base + lane <= cut_w))

    idx_d = d0 + lax.broadcasted_iota(jnp.int32, (tq, tq), 1)
    sd = scd_scr[...]
    picked_d = (sd > thr) | ((sd == thr) & (idx_d <= cut))

    qs = []
    for pair in range(H_B // 2):
        blk = q_ref[:, LANES * pair:LANES * (pair + 1)]
        qs += [blk * sel_lo, blk * sel_hi]

    def attend(r0, n, bias, carries):
        kk, vv = kk_scr[pl.ds(r0, n), :], vv_scr[pl.ds(r0, n), :]
        return tuple(_softmax_step(_dot_nt(qs[h], kk) + bias, vv, carries[h]) for h in range(H_B))

    def body(kb, carries):
        bias = jnp.concatenate(
            [jnp.where(picked_tile(sc_scr[kb, :, LANES * j:LANES * (j + 1)], kb * tk + LANES * j), 0.0, NEG_INF)
             for j in range(n_tiles)], axis=1)
        return attend(pl.multiple_of(kb * tk, tk), tk, bias, carries)

    carries = lax.fori_loop(0, n_full, body, tuple(_softmax_init(tq) for _ in range(H_B)))
    carries = attend(d0, tq, jnp.where(picked_d, 0.0, NEG_INF), carries)
    outs = [acc / l for _, l, acc in carries]
    for pair in range(H_B // 2):
        o_ref[:, LANES * pair:LANES * (pair + 1)] = jnp.where(low, outs[2 * pair], outs[2 * pair + 1]).astype(BF16)


def _dsat_kernel(*refs, p_len, t_len, tq, tk, n_sel):
    if p_len:
        q_ref, qi_ref, wi_ref, new_ref, past_ref, o_ref, kk_scr, vv_scr, ki_scr, sc_scr, scd_scr = refs
    else:
        q_ref, qi_ref, wi_ref, new_ref, o_ref, kk_scr, vv_scr, ki_scr, sc_scr, scd_scr = refs
        past_ref = None
    qi = pl.program_id(1)
    l_len = p_len + t_len
    k_sel = float(n_sel)
    sub = 8

    @pl.when(qi == 0)
    def _fill():
        r1 = lax.broadcasted_iota(jnp.int32, (LANES, 2 * LANES), 0)
        c1 = lax.broadcasted_iota(jnp.int32, (LANES, 2 * LANES), 1)
        dup_kv = jnp.where((c1 % DH_B == r1 % DH_B) & (c1 // LANES == r1 // DH_B), 1.0, 0.0).astype(BF16)
        r2 = lax.broadcasted_iota(jnp.int32, (D_IDX, LANES), 0)
        c2 = lax.broadcasted_iota(jnp.int32, (D_IDX, LANES), 1)
        dup_ki = jnp.where(c2 % D_IDX == r2, 1.0, 0.0).astype(BF16)

        def fill(src_ref, dst, n):
            for r0 in range(0, n, ROW_CHUNK):
                rc = min(ROW_CHUNK, n - r0)
                kv2 = _dot(src_ref[r0:r0 + rc, :LANES].astype(BF16), dup_kv)
                kk_scr[dst + r0:dst + r0 + rc, :] = kv2[:, :LANES].astype(BF16)
                vv_scr[dst + r0:dst + r0 + rc, :] = kv2[:, LANES:].astype(BF16)
                ki_scr[dst + r0:dst + r0 + rc, :] = _dot(src_ref[r0:r0 + rc, LANES:].astype(BF16), dup_ki).astype(BF16)
        if p_len:
            fill(past_ref, 0, p_len)
        fill(new_ref, p_len, t_len)

    d0 = pl.multiple_of(p_len + qi * tq, tq)
    n_full = (p_len + qi * tq) // tk
    lane = lax.broadcasted_iota(jnp.int32, (tq, LANES), 1)
    low = lane < DH_B
    sel_lo = jnp.where(low, 1.0, 0.0).astype(BF16)
    sel_hi = jnp.where(low, 0.0, 1.0).astype(BF16)
    dmask = (lax.broadcasted_iota(jnp.int32, (tq, tq), 0) // CHUNK) <= (lax.broadcasted_iota(jnp.int32, (tq, tq), 1) // CHUNK)

    w_t = jnp.transpose(wi_ref[...])
    q_idx = []
    for pr in range(H_IDX // 2):
        blk = qi_ref[:, LANES * pr:LANES * (pr + 1)]
        q_idx += [blk * sel_lo, blk * sel_hi]

    def idx_score(ki2):
        sc = None
        for h in range(H_IDX):
            term = jnp.maximum(_dot_nt(ki2, q_idx[h]), 0.0) * w_t[h:h + 1, :]
            sc = term if sc is None else sc + term
        return sc

    def score_body(kb, _):
        sc_scr[kb] = idx_score(ki_scr[pl.ds(pl.multiple_of(kb * tk, tk), tk), :])
        return 0

    lax.fori_loop(0, n_full, score_body, 0)
    scd_scr[...] = jnp.where(dmask, idx_score(ki_scr[pl.ds(d0, tq), :]), -jnp.inf)

    def fold(fn, comb, reduce, init):
        def part(x):
            return reduce(x.reshape(x.shape[0] // sub, sub, tq), axis=0)
        acc = lax.fori_loop(0, n_full, lambda kb, a: comb(a, part(fn(sc_scr[kb], kb * tk))),
                            jnp.full((sub, tq), init, F32))
        acc = comb(acc, part(fn(scd_scr[...], d0)))
        return reduce(acc, axis=0, keepdims=True)

    def count_ge(thr):
        return fold(lambda s, _: jnp.where(s >= thr, 1.0, 0.0), jnp.add, jnp.sum, 0.0)

    def count_gt(thr):
        return fold(lambda s, _: jnp.where(s > thr, 1.0, 0.0), jnp.add, jnp.sum, 0.0)

    def min_ge(thr):
        return fold(lambda s, _: jnp.where(s >= thr, s, jnp.inf), jnp.minimum, jnp.min, jnp.inf)

    row_max = fold(lambda s, _: s, jnp.maximum, jnp.max, -jnp.inf)
    row_min = fold(lambda s, _: jnp.where(s > -jnp.inf, s, jnp.inf), jnp.minimum, jnp.min, jnp.inf)

    def bisect(_, lh):
        lo, hi = lh
        mid = lo + 0.5 * (hi - lo)
        ge = count_ge(mid) >= k_sel
        return jnp.where(ge, mid, lo), jnp.where(ge, hi, mid)

    def settle(lo):
        thr = min_ge(lo)
        n_gt = count_gt(thr)
        return thr, n_gt, jnp.max(jnp.where(n_gt >= k_sel, 1, 0))

    hi0 = jnp.where(row_max > 0, 2.0 * row_max, 0.5 * row_max) + 1.0
    lo, hi = lax.fori_loop(0, 16, bisect, (row_min, hi0))
    thr, n_gt, bad = settle(lo)

    def more(state):
        lo, hi, _, _, _, it = state
        lo, hi = lax.fori_loop(0, 4, bisect, (lo, hi))
        thr, n_gt, bad = settle(lo)
        return lo, hi, thr, n_gt, bad, it + 1

    _, _, thr, n_gt, _, _ = lax.while_loop(lambda st: (st[4] > 0) & (st[5] < 64), more,
                                           (lo, hi, thr, n_gt, bad, 0))

    n_ge = count_ge(thr)
    need = k_sel - n_gt
    excess = n_ge > k_sel

    def key_index(s, base):
        return base + lax.broadcasted_iota(jnp.int32, s.shape, 0)

    def cut_search(_):
        def count_tied_upto(j):
            return fold(lambda s, base: jnp.where((s == thr) & (key_index(s, base) <= j), 1.0, 0.0),
                        jnp.add, jnp.sum, 0.0)

        def step(_, jj):
            jlo, jhi = jj
            mid = (jlo + jhi) >> 1
            ok = count_tied_upto(mid) >= need
            return jnp.where(ok, jlo, mid), jnp.where(ok, mid, jhi)

        steps = int(math.ceil(math.log2(l_len))) + 1
        _, jhi = lax.fori_loop(0, steps, step,
                               (jnp.full((1, tq), -1, jnp.int32), jnp.full((1, tq), l_len - 1, jnp.int32)))
        return jnp.where(excess, jhi, l_len)

    cut = lax.cond(jnp.max(jnp.where(excess, 1, 0)) > 0, cut_search,
                   lambda _: jnp.full((1, tq), l_len, jnp.int32), 0)

    def bias_of(s, base):
        return jnp.where((s > thr) | ((s == thr) & (key_index(s, base) <= cut)), 0.0, NEG_INF)

    qs = []
    for pair in range(H_B // 2):
        blk = q_ref[:, LANES * pair:LANES * (pair + 1)]
        qs += [blk * sel_lo, blk * sel_hi]

    def attend(r0, n, bias_t, carries):
        bias = jnp.transpose(bias_t)
        kk, vv = kk_scr[pl.ds(r0, n), :], vv_scr[pl.ds(r0, n), :]
        return tuple(_softmax_step(_dot_nt(qs[h], kk) + bias, vv, carries[h]) for h in range(H_B))

    carries = lax.fori_loop(
        0, n_full, lambda kb, c: attend(pl.multiple_of(kb * tk, tk), tk, bias_of(sc_scr[kb], kb * tk), c),
        tuple(_softmax_init(tq) for _ in range(H_B)))
    carries = attend(d0, tq, bias_of(scd_scr[...], d0), carries)
    outs = [acc / l for _, l, acc in carries]
    for pair in range(H_B // 2):
        o_ref[:, LANES * pair:LANES * (pair + 1)] = jnp.where(low, outs[2 * pair], outs[2 * pair + 1]).astype(BF16)


def _dsa(q, qidx, wi, rows, past, layer, tq, tk):
    b, t, _ = q.shape
    p_len = 0 if past is None else past.shape[2]
    l_len = p_len + t
    n_sel = min(TOPK_MAX, l_len // 4)
    n_blocks = max((l_len - tq) // tk, 1)
    args = [q, qidx, wi, rows] + ([past] if past is not None else [])
    transposed = tq % LANES == 0
    body = _dsat_kernel if transposed else _dsa_kernel
    return pl.pallas_call(
        functools.partial(body, p_len=p_len, t_len=t, tq=tq, tk=tk, n_sel=n_sel),
        grid=(b, t // tq),
        in_specs=_attn_specs((b, t), tq, [256, 512, LANES], 192, past, layer),
        out_specs=_out_spec(tq),
        out_shape=jax.ShapeDtypeStruct((b, t, 256), BF16),
        scratch_shapes=[pltpu.VMEM((l_len, LANES), BF16)] * 3
                       + [pltpu.VMEM((n_blocks, tk, tq) if transposed else (n_blocks, tq, tk), F32),
                          pltpu.VMEM((tq, tq), F32)],
        compiler_params=_params("parallel", "arbitrary"),
        name="dsa",
    )(*args)


def _gather_cols(w, src, scale):
    src = [int(s) for s in src]
    scale = [float(c) if s >= 0 else 0.0 for s, c in zip(src, scale)]
    pieces, i = [], 0
    while i < len(src):
        j = i + 1
        while j < len(src) and scale[j] == scale[i] and (src[j] == src[j - 1] + 1 if src[i] >= 0 else src[j] < 0):
            j += 1
        if src[i] < 0:
            pieces.append(jnp.zeros((w.shape[0], j - i), w.dtype))
        else:
            run = w[:, src[i]:src[i] + j - i]
            pieces.append(run if scale[i] == 1.0 else run * scale[i])
        i = j
    return jnp.concatenate(pieces, axis=1)


class _Cols:
    def __init__(self):
        self.src, self.scale, self.rot_src, self.rot_sign, self.rope = [], [], [], [], []

    def plain(self, start, n, scale=1.0):
        for i in range(n):
            self._add(start + i, scale, -1, 0.0, None)

    def pad(self, n):
        for _ in range(n):
            self._add(-1, 0.0, -1, 0.0, None)

    def roped(self, start, d, scale=1.0):
        half = d // 2
        for i in range(d):
            partner, sign = (start + i + half, -1.0) if i < half else (start + i - half, 1.0)
            self._add(start + i, scale, partner, sign * scale, (d, i % half))

    def _add(self, src, scale, rot_src, rot_sign, rope):
        self.src.append(src)
        self.scale.append(scale)
        self.rot_src.append(rot_src)
        self.rot_sign.append(rot_sign)
        self.rope.append(rope)

    def weights(self, w):
        return (_gather_cols(w, self.src, self.scale).astype(BF16),
                _gather_cols(w, self.rot_src, self.rot_sign).astype(BF16))

    def tables(self, pos):
        cs = {}
        for d in sorted({r[0] for r in self.rope if r is not None}):
            freqs = ROPE_THETA ** (-jnp.arange(d // 2, dtype=F32) * 2.0 / d)
            ang = pos.astype(F32)[:, None] * freqs[None, :]
            cs[d] = (jnp.cos(ang), jnp.sin(ang))
        cos, sin, i, n = [], [], 0, pos.shape[0]
        while i < len(self.rope):
            r, j = self.rope[i], i + 1
            if r is None:
                while j < len(self.rope) and self.rope[j] is None:
                    j += 1
                cos.append(jnp.ones((n, j - i), F32))
                sin.append(jnp.zeros((n, j - i), F32))
            else:
                while j < len(self.rope) and self.rope[j] == (r[0], r[1] + j - i):
                    j += 1
                cos.append(cs[r[0]][0][:, r[1]:r[1] + j - i])
                sin.append(cs[r[0]][1][:, r[1]:r[1] + j - i])
            i = j
        return jnp.concatenate(cos, axis=1), jnp.concatenate(sin, axis=1)


def _plans():
    p = _Cols()
    p.plain(O_CQ, Q_LORA)
    p.plain(O_CKV, KV_LORA)
    for h in range(H_C):
        p.plain(O_KC + DH_C * h, DH_C)
        p.plain(O_VC + DH_C * h, DH_C)
    for h in range(H_C):
        p.plain(O_QC + DH_C * h, DH_C, DH_C ** -0.5)
        p.pad(LANES - DH_C)
    p.plain(O_WIB, H_IDX)
    p.pad(LANES - H_IDX)
    assert len(p.src) == P_END

    r = _Cols()
    r.roped(O_KB, DH_B)
    r.plain(O_VB, DH_B)
    r.roped(O_KIB, D_IDX)
    r.pad(R_DIFF - (2 * DH_B + D_IDX))
    for h in range(H_D):
        r.roped(O_KD + 2 * DQK_D * h, DQK_D)
        r.roped(O_KD + 2 * DQK_D * h + DQK_D, DQK_D)
        r.plain(O_VD + DV_D * h, DV_D)
    r.roped(O_KR, ROPE_A)
    r.pad(LANES - ROPE_A)
    for h in range(H_B):
        r.roped(O_QB + DH_B * h, DH_B, DH_B ** -0.5)
    for h in range(H_IDX):
        r.roped(O_QIB + D_IDX * h, D_IDX)
    for h in range(H_D):
        r.roped(O_QD + 2 * DQK_D * h, DQK_D)
        r.roped(O_QD + 2 * DQK_D * h + DQK_D, DQK_D)
        r.pad(LANES - 2 * DQK_D)
    assert len(r.src) == R_END

    q = _Cols()
    for h in range(H_A):
        q.plain((NOPE_A + ROPE_A) * h, NOPE_A)
        q.roped((NOPE_A + ROPE_A) * h + NOPE_A, ROPE_A)
        q.pad(LANES - NOPE_A - ROPE_A)
    return p, r, q


def _layer_weights(l, prm, plans):
    p, r, q = plans
    w_in = prm["w_in"][l]
    wr, wrr = r.weights(w_in)
    wqb, wqbr = q.weights(prm["mla_w_qb"][l])
    wkvb = prm["mla_w_kvb"][l]
    k_src, v_src = [], []
    for h in range(H_A):
        k_src += list(range((NOPE_A + V_A) * h, (NOPE_A + V_A) * h + NOPE_A)) + [-1] * (LANES - NOPE_A)
        v_src += list(range((NOPE_A + V_A) * h + NOPE_A, (NOPE_A + V_A) * (h + 1)))
    place = np.zeros((ROPE_A, H_A * LANES), np.float32)
    for h in range(H_A):
        place[np.arange(ROPE_A), LANES * h + NOPE_A + np.arange(ROPE_A)] = 1.0
    dn = jnp.concatenate([jnp.zeros((2 * DQK_D,), F32), prm["diff_norm"][l]])[None, :]

    def ff(tag):
        return (prm[f"norm_{tag}"][l][None, :], prm[f"w_{tag}_gate"][l].astype(BF16),
                prm[f"w_{tag}_up"][l].astype(BF16), prm[f"w_{tag}_down"][l].astype(BF16))

    return {
        "ff1": ff("ff1"), "ff2": ff("ff2"),
        "nm": prm["norm_mix"][l][None, :],
        "wp": p.weights(w_in)[0], "wr": wr, "wrr": wrr,
        "qn": prm["mla_q_norm"][l][None, :], "wqb": wqb, "wqbr": wqbr,
        "kvn": prm["mla_kv_norm"][l][None, :],
        "wk": _gather_cols(wkvb, k_src, np.ones(len(k_src))).astype(BF16),
        "pk": jnp.asarray(place).astype(BF16),
        "wv": _gather_cols(wkvb, v_src, np.ones(len(v_src))).astype(BF16),
        "lam": prm["diff_lambda"][l], "dn": dn,
        "wo": prm["w_out"][l].astype(BF16),
    }


def _trunk(x, caches, pos, layers, plans, fnorm, tq, tk, proj_fold):
    b, t, _ = x.shape
    depth = len(layers)
    tm = min(512, b * t)
    _, r_plan, q_plan = plans
    pb, pt = (b // proj_fold, t * proj_fold)
    pos_p = jnp.tile(pos, proj_fold)
    tabs = r_plan.tables(pos_p) + q_plan.tables(pos_p)
    tq_p = min(256, pt)
    xf = x.reshape(b * t, D_MODEL)
    rows = [[], [], [], []]
    for l, lw in enumerate(layers):
        lam_init = 0.8 - 0.6 * math.exp(-0.3 * l)
        xf = _ffn(xf, None, None, *lw["ff1"], None, tm)
        outs = _proj(xf.reshape(pb, pt, D_MODEL), lw, tabs, tq_p)
        r_mla, r_dsa, r_sb, r_diff, q_mla, q_dsa, q_idx, w_idx, q_sb, q_diff = [
            o.reshape(b, t, o.shape[-1]) for o in outs]
        past = [None] * 4 if caches is None else caches
        mix = (
            _mla(q_mla, r_mla, past[0], l, lw, tq, tk),
            _dsa(q_dsa, q_idx, w_idx, r_dsa, past[1], l, tq, tk),
            _sb(q_sb, r_sb, past[2], l, tq, tk),
            _diff(q_diff, r_diff, past[3], l, lw, lam_init, tq, tk),
        )
        mix = [m.reshape(b * t, 256) for m in mix]
        xf = _ffn(xf, mix, lw["wo"], *lw["ff2"], fnorm if l == depth - 1 else None, tm)
        for acc, r in zip(rows, (r_mla, r_dsa, r_sb, r_diff)):
            acc.append(r)
    y = xf.reshape(b, t, D_MODEL)
    return (y, jnp.stack(rows[0]), jnp.stack(rows[1]),
            jnp.stack(rows[2]).reshape(depth, b, t, H_C, 2 * DH_C),
            jnp.stack(rows[3]).reshape(depth, b, t, H_D, 2 * DQK_D + DV_D))


def kernel(x_prompt, x_sample, cache_mla, cache_dsa, cache_sb, cache_diff, norm_ff1, w_ff1_gate, w_ff1_up, w_ff1_down, norm_mix, w_in, mla_q_norm, mla_w_qb, mla_kv_norm, mla_w_kvb, diff_lambda, diff_norm, w_out, norm_ff2, w_ff2_gate, w_ff2_up, w_ff2_down, final_norm):
    prm = dict(norm_ff1=norm_ff1, w_ff1_gate=w_ff1_gate, w_ff1_up=w_ff1_up, w_ff1_down=w_ff1_down,
               norm_mix=norm_mix, w_in=w_in, mla_q_norm=mla_q_norm, mla_w_qb=mla_w_qb,
               mla_kv_norm=mla_kv_norm, mla_w_kvb=mla_w_kvb, diff_lambda=diff_lambda, diff_norm=diff_norm,
               w_out=w_out, norm_ff2=norm_ff2, w_ff2_gate=w_ff2_gate, w_ff2_up=w_ff2_up, w_ff2_down=w_ff2_down)
    depth = w_in.shape[0]
    plans = _plans()
    layers = [_layer_weights(l, prm, plans) for l in range(depth)]
    fnorm = final_norm[None, :]

    t_p = x_prompt.shape[1]
    assert t_p % CHUNK == 0
    tq_p = min(512, t_p)
    out_p = _trunk(x_prompt, None, jnp.arange(t_p, dtype=jnp.int32), layers, plans, fnorm, tq_p, tq_p, 1)


    b_s, t_s, _ = x_sample.shape
    past = cache_mla.shape[2]
    assert past % CHUNK == 0 and t_s <= CHUNK, "new sample frames must sit in one chunk after whole cached chunks"
    tk_s = math.gcd(past, 512)
    caches = (cache_mla, cache_dsa,
              cache_sb.reshape(cache_sb.shape[:3] + (-1,)), cache_diff.reshape(cache_diff.shape[:3] + (-1,)))
    fold = math.gcd(b_s, max(1, 256 // t_s))
    out_s = _trunk(x_sample, caches, past + jnp.arange(t_s, dtype=jnp.int32), layers, plans, fnorm, t_s, tk_s, fold)

    return (out_p[0], out_s[0]) + tuple(out_p[1:]) + tuple(out_s[1:])
```

```python
import functools
import math

import numpy as np
import jax
import jax.numpy as jnp
from jax import lax
from jax.experimental import pallas as pl
from jax.experimental.pallas import tpu as pltpu

F32 = jnp.float32
BF16 = jnp.bfloat16

D_MODEL = 1024
CHUNK = 64
ROPE_THETA = 10000.0
EPS = 1e-6
NEG_INF = -1e30
HEAD_DIM = D_MODEL // 16
D_FF = 11 * D_MODEL // 4
H_A = 4
Q_LORA = 4 * HEAD_DIM
KV_LORA = 2 * HEAD_DIM
NOPE_A = HEAD_DIM
ROPE_A = HEAD_DIM // 2
V_A = HEAD_DIM
H_B = 4
DH_B = HEAD_DIM
H_IDX = 8
D_IDX = HEAD_DIM
TOPK_MAX = 256
H_C = 4
DH_C = HEAD_DIM
H_D = 4
DQK_D = HEAD_DIM // 2
DV_D = HEAD_DIM
COL_SIZES = (Q_LORA, KV_LORA, ROPE_A,
             H_B * DH_B, DH_B, DH_B, H_IDX * D_IDX, D_IDX, H_IDX,
             H_C * DH_C, H_C * DH_C, H_C * DH_C,
             H_D * 2 * DQK_D, H_D * 2 * DQK_D, H_D * DV_D)
(O_CQ, O_CKV, O_KR, O_QB, O_KB, O_VB, O_QIB, O_KIB, O_WIB,
 O_QC, O_KC, O_VC, O_QD, O_KD, O_VD) = np.concatenate([[0], np.cumsum(COL_SIZES)[:-1]]).tolist()

LANES = 128
VMEM_LIMIT = 56 * 1024 * 1024
FF_CHUNK = 256
ROW_CHUNK = 512

P_CQ, P_CKV, P_SB, P_QSB, P_WI, P_END = 0, 256, 384, 896, 1408, 1536
R_DSA, R_DIFF, R_KR, R_QDSA, R_QI, R_QDIFF, R_END = 0, 256, 768, 896, 1152, 1664, 2176


def _dot(a, b):
    return jnp.dot(a, b, preferred_element_type=F32)


def _dot_nt(a, b):
    return lax.dot_general(a, b, (((1,), (1,)), ((), ())), preferred_element_type=F32)


def _rms(x, g):
    return x * lax.rsqrt(jnp.mean(x * x, axis=-1, keepdims=True) + EPS) * g


def _const_spec(shape):
    n = len(shape)
    return pl.BlockSpec(shape, lambda *_: (0,) * n, pipeline_mode=pl.Buffered(1))


def _params(*sem):
    return pltpu.CompilerParams(dimension_semantics=sem, vmem_limit_bytes=VMEM_LIMIT)


def _ffn_kernel(*refs, has_mix, has_final):
    it = iter(refs)
    x_ref = next(it)
    if has_mix:
        o_refs = [next(it) for _ in range(4)]
        wo_ref = next(it)
    g_ref, wg_ref, wu_ref, wd_ref = next(it), next(it), next(it), next(it)
    fn_ref = next(it) if has_final else None
    out_ref = next(it)

    x = x_ref[...]
    if has_mix:
        for i, o_ref in enumerate(o_refs):
            x = x + _dot(o_ref[...], wo_ref[256 * i:256 * (i + 1), :])
    xn = _rms(x, g_ref[...]).astype(BF16)
    acc = jnp.zeros_like(x)
    for c0 in range(0, D_FF, FF_CHUNK):
        gate = _dot(xn, wg_ref[:, c0:c0 + FF_CHUNK])
        up = _dot(xn, wu_ref[:, c0:c0 + FF_CHUNK])
        hid = (gate * jax.nn.sigmoid(gate) * up).astype(BF16)
        acc = acc + _dot(hid, wd_ref[c0:c0 + FF_CHUNK, :])
    y = x + 0.5 * acc
    if has_final:
        y = _rms(y, fn_ref[...])
    out_ref[...] = y


def _ffn(x, mix, wo, g, wg, wu, wd, fnorm, tm):
    n = x.shape[0]
    has_mix, has_final = mix is not None, fnorm is not None
    row = lambda w: pl.BlockSpec((tm, w), lambda i: (i, 0))
    args, specs = [x], [row(D_MODEL)]
    if has_mix:
        args += list(mix) + [wo]
        specs += [row(256)] * 4 + [_const_spec(wo.shape)]
    args += [g, wg, wu, wd]
    specs += [_const_spec(g.shape), _const_spec(wg.shape), _const_spec(wu.shape), _const_spec(wd.shape)]
    if has_final:
        args.append(fnorm)
        specs.append(_const_spec(fnorm.shape))
    return pl.pallas_call(
        functools.partial(_ffn_kernel, has_mix=has_mix, has_final=has_final),
        grid=(n // tm,),
        in_specs=specs,
        out_specs=row(D_MODEL),
        out_shape=jax.ShapeDtypeStruct((n, D_MODEL), F32),
        compiler_params=_params("parallel"),
        name="ffn",
    )(*args)


def _rope(h, cos, sin_signed, d):
    w = h.shape[1]
    first = lax.broadcasted_iota(jnp.int32, h.shape, 1) % d < d // 2
    partner = jnp.where(first, pltpu.roll(h, w - d // 2, axis=1), pltpu.roll(h, d // 2, axis=1))
    return h * cos + partner * sin_signed


def _proj_kernel(x_ref, g_ref, wp_ref, wr_ref, cr_ref, sr_ref,
                 qn_ref, wqb_ref, cq_ref, sq_ref, kvn_ref,
                 _amla, _adsa, _asb, _adiff,
                 rmla_ref, rdsa_ref, rsbh_ref, rdiffh_ref, rsb_ref, rdiff_ref,
                 qmla_ref, qdsa_ref, qi_ref, wi_ref, qsb_ref, qdiff_ref):
    xn = _rms(x_ref[...], g_ref[...]).astype(BF16)
    tq = x_ref.shape[0]

    def plain(a, b):
        return _dot(xn, wp_ref[:, a:b])

    def roped(a, b, d):
        return _rope(_dot(xn, wr_ref[:, a:b]), cr_ref[:, a:b], sr_ref[:, a:b], d)

    cqn = _rms(plain(P_CQ, P_CKV), qn_ref[...]).astype(BF16)
    qmla_ref[...] = _rope(_dot(cqn, wqb_ref[...]), cq_ref[...], sq_ref[...], ROPE_A).astype(BF16)
    rmla_ref[:, :KV_LORA] = _rms(plain(P_CKV, P_SB), kvn_ref[...])
    rmla_ref[:, KV_LORA:] = roped(R_KR, R_QDSA, ROPE_A)[:, :ROPE_A]
    rdsa_ref[...] = roped(R_DSA, R_DIFF, DH_B)[:, :2 * DH_B + D_IDX]
    qdsa_ref[...] = roped(R_QDSA, R_QI, DH_B).astype(BF16)
    qi_ref[...] = roped(R_QI, R_QDIFF, D_IDX).astype(BF16)
    wi_ref[...] = plain(P_WI, P_END) * (H_IDX ** -0.5)
    rsb = plain(P_SB, P_QSB)
    rsb_ref[...] = rsb
    for h in range(H_C):
        rsbh_ref[pl.ds(h, tq, stride=H_C), :] = rsb[:, LANES * h:LANES * (h + 1)]
    qsb_ref[...] = plain(P_QSB, P_WI).astype(BF16)
    rdiff = roped(R_DIFF, R_KR, DQK_D)
    rdiff_ref[...] = rdiff
    for h in range(H_D):
        rdiffh_ref[pl.ds(h, tq, stride=H_D), :] = rdiff[:, LANES * h:LANES * (h + 1)]
    qdiff_ref[...] = roped(R_QDIFF, R_END, DQK_D).astype(BF16)


def _proj(x, lw, tabs, stacked, layer, tq):
    b, t, _ = x.shape
    cr, sr, cq, sq = tabs
    tok = lambda w: pl.BlockSpec((None, tq, w), lambda bi, i: (bi, i, 0))
    lay = lambda r, w: pl.BlockSpec((None, None, r, w), lambda bi, i: (layer, bi, i, 0))
    tab = lambda w: pl.BlockSpec((tq, w), lambda bi, i: (i, 0))
    consts = [lw["nm"], lw["wp"], lw["wr"]]
    consts2 = [lw["qn"], lw["wqb"]]
    in_specs = ([tok(D_MODEL)] + [_const_spec(a.shape) for a in consts] + [tab(R_END), tab(R_END)]
                + [_const_spec(a.shape) for a in consts2] + [tab(512), tab(512), _const_spec(lw["kvn"].shape)]
                + [pl.BlockSpec(memory_space=pl.ANY)] * 4)
    widths = [(512, F32), (512, F32),
              (512, BF16), (256, BF16), (512, BF16), (LANES, F32), (512, BF16), (512, BF16)]
    n_in = len(in_specs)
    return pl.pallas_call(
        _proj_kernel,
        grid=(b, t // tq),
        in_specs=in_specs,
        out_specs=[lay(tq, 160), lay(tq, 192), lay(tq * H_C, LANES), lay(tq * H_D, LANES)]
                  + [tok(w) for w, _ in widths],
        out_shape=[jax.ShapeDtypeStruct(a.shape, a.dtype) for a in stacked]
                  + [jax.ShapeDtypeStruct((b, t, w), dt) for w, dt in widths],
        input_output_aliases={n_in - 4 + k: k for k in range(4)},
        compiler_params=_params("parallel", "parallel"),
        name="proj",
    )(x, *consts, cr, sr, *consts2, cq, sq, lw["kvn"], *stacked)


def _chunk_mask(tq):
    ri = lax.broadcasted_iota(jnp.int32, (tq, tq), 0)
    ci = lax.broadcasted_iota(jnp.int32, (tq, tq), 1)
    return (ci // CHUNK) <= (ri // CHUNK)


def _softmax_step(s, v, carry):
    m, l, acc = carry
    m_new = jnp.maximum(m, jnp.max(s, axis=-1, keepdims=True))
    alpha = jnp.exp(m - m_new)
    p = jnp.exp(s - m_new)
    l = alpha * l + jnp.sum(p, axis=-1, keepdims=True)
    acc = alpha * acc + _dot(p.astype(BF16), v)
    return m_new, l, acc


def _softmax_init(tq):
    return (jnp.full((tq, 1), NEG_INF, F32), jnp.zeros((tq, 1), F32), jnp.zeros((tq, LANES), F32))


def _lane_lt(tq, n):
    return lax.broadcasted_iota(jnp.int32, (tq, LANES), 1) < n


def _attn_specs(tq, q_widths, rows, past, layer):
    def whole(a):
        if a.ndim == 3:
            return pl.BlockSpec((None,) + a.shape[1:], lambda bi, i: (bi, 0, 0))
        return pl.BlockSpec((None, None) + a.shape[2:], lambda bi, i: (layer, bi, 0, 0))
    specs = [pl.BlockSpec((None, tq, w), lambda bi, i: (bi, i, 0)) for w in q_widths]
    return specs + [whole(rows)] + ([whole(past)] if past is not None else [])


def _out_spec(tq):
    return pl.BlockSpec((None, tq, 256), lambda bi, i: (bi, i, 0))


def _mla_kernel(*refs, p_len, t_len, tq, tk):
    if p_len:
        q_ref, new_ref, past_ref, wk_ref, pk_ref, wv_ref, o_ref, k_scr, v_scr = refs
    else:
        q_ref, new_ref, wk_ref, pk_ref, wv_ref, o_ref, k_scr, v_scr = refs
        past_ref = None
    qi = pl.program_id(1)
    scale = (NOPE_A + ROPE_A) ** -0.5

    @pl.when(qi == 0)
    def _fill():
        def fill(src_ref, dst, n):
            for r0 in range(0, n, ROW_CHUNK):
                rc = min(ROW_CHUNK, n - r0)
                lat = src_ref[r0:r0 + rc, :KV_LORA].astype(BF16)
                rope = src_ref[r0:r0 + rc, KV_LORA:].astype(BF16)
                k_scr[dst + r0:dst + r0 + rc, :] = (_dot(lat, wk_ref[...]) + _dot(rope, pk_ref[...])).astype(BF16)
                v_scr[dst + r0:dst + r0 + rc, :] = _dot(lat, wv_ref[...]).astype(BF16)
        if p_len:
            fill(past_ref, 0, p_len)
        fill(new_ref, p_len, t_len)

    d0 = pl.multiple_of(p_len + qi * tq, tq)
    n_full = (p_len + qi * tq) // tk
    dmask = _chunk_mask(tq)
    low = _lane_lt(tq, V_A)
    qs = [q_ref[:, LANES * h:LANES * (h + 1)] for h in range(H_A)]

    def step(r0, n, mask, carries):
        out = []
        for h in range(H_A):
            s = _dot_nt(qs[h], k_scr[pl.ds(r0, n), LANES * h:LANES * (h + 1)]) * scale
            if mask is not None:
                s = jnp.where(mask, s, NEG_INF)
            out.append(_softmax_step(s, v_scr[pl.ds(r0, n), LANES * (h // 2):LANES * (h // 2 + 1)], carries[h]))
        return tuple(out)

    carries = lax.fori_loop(0, n_full, lambda kb, c: step(pl.multiple_of(kb * tk, tk), tk, None, c),
                            tuple(_softmax_init(tq) for _ in range(H_A)))
    carries = step(d0, tq, dmask, carries)
    outs = [acc / l for _, l, acc in carries]
    for pair in range(H_A // 2):
        o_ref[:, LANES * pair:LANES * (pair + 1)] = jnp.where(low, outs[2 * pair], outs[2 * pair + 1]).astype(BF16)


def _mla(q, rows, past, layer, lw, tq, tk):
    b, t, _ = q.shape
    p_len = 0 if past is None else past.shape[2]
    consts = [lw["wk"], lw["pk"], lw["wv"]]
    args = [q, rows] + ([past] if past is not None else []) + consts
    return pl.pallas_call(
        functools.partial(_mla_kernel, p_len=p_len, t_len=t, tq=tq, tk=tk),
        grid=(b, t // tq),
        in_specs=_attn_specs(tq, [512], rows, past, layer) + [_const_spec(a.shape) for a in consts],
        out_specs=_out_spec(tq),
        out_shape=jax.ShapeDtypeStruct((b, t, 256), BF16),
        scratch_shapes=[pltpu.VMEM((p_len + t, 512), BF16), pltpu.VMEM((p_len + t, 256), BF16)],
        compiler_params=_params("parallel", "arbitrary"),
        name="mla",
    )(*args)


def _diff_kernel(*refs, p_len, t_len, tq, tk, lam_init):
    if p_len:
        q_ref, new_ref, past_ref, lam_ref, dn_ref, o_ref = refs
    else:
        q_ref, new_ref, lam_ref, dn_ref, o_ref = refs
        past_ref = None
    qi = pl.program_id(1)
    scale = DQK_D ** -0.5
    lam = lam_ref[...]
    lam_full = (jnp.exp(jnp.sum(lam[0:1] * lam[1:2], axis=-1, keepdims=True))
                - jnp.exp(jnp.sum(lam[2:3] * lam[3:4], axis=-1, keepdims=True)) + lam_init)
    dmask = _chunk_mask(tq)
    lane = lax.broadcasted_iota(jnp.int32, (tq, LANES), 1)
    sel1 = jnp.where(lane < DQK_D, 1.0, 0.0).astype(BF16)
    sel2 = jnp.where((lane >= DQK_D) & (lane < 2 * DQK_D), 1.0, 0.0).astype(BF16)
    is_v = lane >= 2 * DQK_D
    n_past = p_len // tk
    n_new = (qi * tq) // tk

    q12 = []
    for h in range(H_D):
        qh = q_ref[:, LANES * h:LANES * (h + 1)]
        q12 += [qh * sel1, qh * sel2]

    def step(load, mask, carries):
        out = []
        for h in range(H_D):
            kv = load(h)
            for m in range(2):
                s = _dot_nt(q12[2 * h + m], kv) * scale
                if mask is not None:
                    s = jnp.where(mask, s, NEG_INF)
                out.append(_softmax_step(s, kv, carries[2 * h + m]))
        return tuple(out)

    def new_rows(r0, n):
        return lambda h: new_ref[pl.ds(r0, n), LANES * h:LANES * (h + 1)].astype(BF16)

    def past_rows(kb):
        return lambda h: past_ref[pl.ds(H_D * kb * tk + h, tk, stride=H_D), :].astype(BF16)

    carries = tuple(_softmax_init(tq) for _ in range(2 * H_D))
    for kb in range(n_past):
        carries = step(past_rows(kb), None, carries)
    carries = lax.fori_loop(0, n_new, lambda kb, c: step(new_rows(pl.multiple_of(kb * tk, tk), tk), None, c),
                            carries)
    carries = step(new_rows(pl.multiple_of(qi * tq, tq), tq), dmask, carries)
    outs = []
    for h in range(H_D):
        (_, l1, a1), (_, l2, a2) = carries[2 * h], carries[2 * h + 1]
        o = a1 / l1 - lam_full * (a2 / l2)
        ms = jnp.sum(jnp.where(is_v, o * o, 0.0), axis=-1, keepdims=True) * (1.0 / DV_D)
        outs.append(o * lax.rsqrt(ms + EPS) * dn_ref[...] * (1.0 - lam_init))
    for pair in range(H_D // 2):
        o_ref[:, LANES * pair:LANES * (pair + 1)] = jnp.where(
            is_v, outs[2 * pair + 1], pltpu.roll(outs[2 * pair], 2 * DQK_D, axis=1)).astype(BF16)


def _diff(q, rows, past, layer, lw, lam_init, tq, tk):
    b, t, _ = q.shape
    p_len = 0 if past is None else past.shape[2] // H_D
    consts = [lw["lam"], lw["dn"]]
    args = [q, rows] + ([past] if past is not None else []) + consts
    return pl.pallas_call(
        functools.partial(_diff_kernel, p_len=p_len, t_len=t, tq=tq, tk=tk, lam_init=lam_init),
        grid=(b, t // tq),
        in_specs=_attn_specs(tq, [512], rows, past, layer) + [_const_spec(a.shape) for a in consts],
        out_specs=_out_spec(tq),
        out_shape=jax.ShapeDtypeStruct((b, t, 256), BF16),
        compiler_params=_params("parallel", "parallel"),
        name="diff",
    )(*args)


def _sb_kernel(*refs, p_len, t_len, tq, tk):
    if p_len:
        q_ref, new_ref, past_ref, o_ref = refs
    else:
        q_ref, new_ref, o_ref = refs
        past_ref = None
    qi = pl.program_id(1)
    n_past = p_len // tk
    n_new = (qi * tq) // tk

    def later(n):
        ri = lax.broadcasted_iota(jnp.int32, (n, n), 0)
        ci = lax.broadcasted_iota(jnp.int32, (n, n), 1)
        return jnp.where(ri > ci, 1.0, 0.0).astype(BF16)

    later_k = later(tk)
    later_q = later_k if tq == tk else later(tq)
    ri = lax.broadcasted_iota(jnp.int32, (tq, tq), 0)
    ci = lax.broadcasted_iota(jnp.int32, (tq, tq), 1)
    causal = ci < ri
    is_v = lax.broadcasted_iota(jnp.int32, (tq, LANES), 1) >= DH_C

    qs = [q_ref[:, LANES * h:LANES * (h + 1)] for h in range(H_C)]

    def step(load, mask, tri, carries):
        out = []
        for h in range(H_C):
            run, acc = carries[h]
            kv = load(h)
            z = _dot_nt(qs[h], kv)
            log_take = jnp.minimum(z, 0.0) - jnp.log1p(jnp.exp(-jnp.abs(z)))
            log_keep = log_take - z
            if mask is not None:
                log_keep = jnp.where(mask, log_keep, 0.0)
            hi = log_keep.astype(BF16)
            lo = (log_keep - hi.astype(F32)).astype(BF16)
            log_after = _dot(hi, tri) + _dot(lo, tri) + run
            a = jnp.exp(log_take + log_after)
            if mask is not None:
                a = jnp.where(mask, a, 0.0)
            acc = acc + _dot(a.astype(BF16), kv)
            run = run + jnp.sum(log_keep, axis=-1, keepdims=True)
            out.append((run, acc))
        return tuple(out)

    def new_rows(r0, n):
        return lambda h: new_ref[pl.ds(r0, n), LANES * h:LANES * (h + 1)].astype(BF16)

    def past_rows(kb):
        return lambda h: past_ref[pl.ds(H_C * kb * tk + h, tk, stride=H_C), :].astype(BF16)

    carries = tuple((jnp.zeros((tq, 1), F32), jnp.zeros((tq, LANES), F32)) for _ in range(H_C))
    carries = step(new_rows(pl.multiple_of(qi * tq, tq), tq), causal, later_q, carries)
    carries = lax.fori_loop(
        0, n_new,
        lambda i, c: step(new_rows(pl.multiple_of((n_new - 1 - i) * tk, tk), tk), None, later_k, c), carries)
    for kb in reversed(range(n_past)):
        carries = step(past_rows(kb), None, later_k, carries)
    for pair in range(H_C // 2):
        o_ref[:, LANES * pair:LANES * (pair + 1)] = jnp.where(
            is_v, carries[2 * pair + 1][1], pltpu.roll(carries[2 * pair][1], DH_C, axis=1)).astype(BF16)


def _sb(q, rows, past, layer, tq, tk):
    b, t, _ = q.shape
    p_len = 0 if past is None else past.shape[2] // H_C
    args = [q, rows] + ([past] if past is not None else [])
    return pl.pallas_call(
        functools.partial(_sb_kernel, p_len=p_len, t_len=t, tq=tq, tk=tk),
        grid=(b, t // tq),
        in_specs=_attn_specs(tq, [512], rows, past, layer),
        out_specs=_out_spec(tq),
        out_shape=jax.ShapeDtypeStruct((b, t, 256), BF16),
        compiler_params=_params("parallel", "parallel"),
        name="sb",
    )(*args)


def _dsa_kernel(*refs, p_len, t_len, tq, tk, n_sel):
    if p_len:
        q_ref, qi_ref, wi_ref, new_ref, past_ref, o_ref, kk_scr, vv_scr, ki_scr, sc_scr, scd_scr = refs
    else:
        q_ref, qi_ref, wi_ref, new_ref, o_ref, kk_scr, vv_scr, ki_scr, sc_scr, scd_scr = refs
        past_ref = None
    qi = pl.program_id(1)
    l_len = p_len + t_len
    n_tiles = tk // LANES
    k_sel = float(n_sel)

    @pl.when(qi == 0)
    def _fill():
        r1 = lax.broadcasted_iota(jnp.int32, (LANES, 2 * LANES), 0)
        c1 = lax.broadcasted_iota(jnp.int32, (LANES, 2 * LANES), 1)
        dup_kv = jnp.where((c1 % DH_B == r1 % DH_B) & (c1 // LANES == r1 // DH_B), 1.0, 0.0).astype(BF16)
        r2 = lax.broadcasted_iota(jnp.int32, (D_IDX, LANES), 0)
        c2 = lax.broadcasted_iota(jnp.int32, (D_IDX, LANES), 1)
        dup_ki = jnp.where(c2 % D_IDX == r2, 1.0, 0.0).astype(BF16)

        def fill(src_ref, dst, n):
            for r0 in range(0, n, ROW_CHUNK):
                rc = min(ROW_CHUNK, n - r0)
                kv2 = _dot(src_ref[r0:r0 + rc, :LANES].astype(BF16), dup_kv)
                kk_scr[dst + r0:dst + r0 + rc, :] = kv2[:, :LANES].astype(BF16)
                vv_scr[dst + r0:dst + r0 + rc, :] = kv2[:, LANES:].astype(BF16)
                ki_scr[dst + r0:dst + r0 + rc, :] = _dot(src_ref[r0:r0 + rc, LANES:].astype(BF16), dup_ki).astype(BF16)
        if p_len:
            fill(past_ref, 0, p_len)
        fill(new_ref, p_len, t_len)

    d0 = pl.multiple_of(p_len + qi * tq, tq)
    n_full = (p_len + qi * tq) // tk
    lane = lax.broadcasted_iota(jnp.int32, (tq, LANES), 1)
    low = lane < DH_B
    sel_lo = jnp.where(low, 1.0, 0.0).astype(BF16)
    sel_hi = jnp.where(low, 0.0, 1.0).astype(BF16)
    dmask = _chunk_mask(tq)

    wi = wi_ref[...]
    wcol = [jnp.sum(jnp.where(lane == h, wi, 0.0), axis=-1, keepdims=True) for h in range(H_IDX)]
    q_idx = []
    for pr in range(H_IDX // 2):
        blk = qi_ref[:, LANES * pr:LANES * (pr + 1)]
        q_idx += [blk * sel_lo, blk * sel_hi]

    def idx_score(ki2):
        sc = None
        for h in range(H_IDX):
            term = jnp.maximum(_dot_nt(q_idx[h], ki2), 0.0) * wcol[h]
            sc = term if sc is None else sc + term
        return sc

    def score_body(kb, _):
        r0 = pl.multiple_of(kb * tk, tk)
        sc_scr[kb] = idx_score(ki_scr[pl.ds(r0, tk), :])
        return 0

    lax.fori_loop(0, n_full, score_body, 0)
    scd_scr[...] = jnp.where(dmask, idx_score(ki_scr[pl.ds(d0, tq), :]), -jnp.inf)

    def fold(tile_fn, diag_fn, comb, reduce, init):
        def body(kb, acc):
            for j in range(n_tiles):
                acc = comb(acc, tile_fn(sc_scr[kb, :, LANES * j:LANES * (j + 1)], kb * tk + LANES * j))
            return acc
        acc = lax.fori_loop(0, n_full, body, jnp.full((tq, LANES), init, F32))
        return comb(reduce(acc, axis=-1, keepdims=True), reduce(diag_fn(scd_scr[...]), axis=-1, keepdims=True))

    def wide(v):
        return jnp.broadcast_to(v, (tq, LANES))

    def count_ge(thr):
        thr_w = wide(thr)
        return fold(lambda s, _: jnp.where(s >= thr_w, 1.0, 0.0), lambda s: jnp.where(s >= thr, 1.0, 0.0),
                    jnp.add, jnp.sum, 0.0)

    def count_gt(thr):
        thr_w = wide(thr)
        return fold(lambda s, _: jnp.where(s > thr_w, 1.0, 0.0), lambda s: jnp.where(s > thr, 1.0, 0.0),
                    jnp.add, jnp.sum, 0.0)

    def min_ge(thr):
        thr_w = wide(thr)
        return fold(lambda s, _: jnp.where(s >= thr_w, s, jnp.inf), lambda s: jnp.where(s >= thr, s, jnp.inf),
                    jnp.minimum, jnp.min, jnp.inf)

    row_max = fold(lambda s, _: s, lambda s: s, jnp.maximum, jnp.max, -jnp.inf)
    row_min = fold(lambda s, _: s, lambda s: jnp.where(s > -jnp.inf, s, jnp.inf), jnp.minimum, jnp.min, jnp.inf)

    def bisect(_, lh):
        lo, hi = lh
        mid = lo + 0.5 * (hi - lo)
        ge = count_ge(mid) >= k_sel
        return jnp.where(ge, mid, lo), jnp.where(ge, hi, mid)

    def settle(lo):
        thr = min_ge(lo)
        n_gt = count_gt(thr)
        return thr, n_gt, jnp.max(jnp.where(n_gt >= k_sel, 1, 0))

    hi0 = row_max + jnp.abs(row_max) * 1e-6 + 1e-30
    lo, hi = lax.fori_loop(0, 20, bisect, (row_min, hi0))
    thr, n_gt, bad = settle(lo)

    def more(state):
        lo, hi, _, _, _, it = state
        lo, hi = lax.fori_loop(0, 4, bisect, (lo, hi))
        thr, n_gt, bad = settle(lo)
        return lo, hi, thr, n_gt, bad, it + 1

    _, _, thr, n_gt, _, _ = lax.while_loop(lambda st: (st[4] > 0) & (st[5] < 64), more,
                                           (lo, hi, thr, n_gt, bad, 0))

    n_ge = count_ge(thr)
    need = k_sel - n_gt
    excess = n_ge > k_sel
    thr_w = wide(thr)

    def cut_search(_):
        def count_tied_upto(j):
            j_w = wide(j)
            def tile(s, base):
                idx = base + lane
                return jnp.where((s == thr_w) & (idx <= j_w), 1.0, 0.0)
            def diag(s):
                idx = d0 + lax.broadcasted_iota(jnp.int32, (tq, tq), 1)
                return jnp.where((s == thr) & (idx <= j), 1.0, 0.0)
            return fold(tile, diag, jnp.add, jnp.sum, 0.0)

        def step(_, jj):
            jlo, jhi = jj
            mid = (jlo + jhi) >> 1
            ok = count_tied_upto(mid) >= need
            return jnp.where(ok, jlo, mid), jnp.where(ok, mid, jhi)

        steps = int(math.ceil(math.log2(l_len))) + 1
        _, jhi = lax.fori_loop(0, steps, step,
                               (jnp.full((tq, 1), -1, jnp.int32), jnp.full((tq, 1), l_len - 1, jnp.int32)))
        return jnp.where(excess, jhi, l_len)

    cut = lax.cond(jnp.max(jnp.where(excess, 1, 0)) > 0, cut_search,
                   lambda _: jnp.full((tq, 1), l_len, jnp.int32), 0)
    cut_w = wide(cut)

    def picked_tile(s, base):
        return (s > thr_w) | ((s == thr_w) & (base + lane <= cut_w))

    idx_d = d0 + lax.broadcasted_iota(jnp.int32, (tq, tq), 1)
    sd = scd_scr[...]
    picked_d = (sd > thr) | ((sd == thr) & (idx_d <= cut))

    qs = []
    for pair in range(H_B // 2):
        blk = q_ref[:, LANES * pair:LANES * (pair + 1)]
        qs += [blk * sel_lo, blk * sel_hi]

    def attend(r0, n, bias, carries):
        kk, vv = kk_scr[pl.ds(r0, n), :], vv_scr[pl.ds(r0, n), :]
        return tuple(_softmax_step(_dot_nt(qs[h], kk) + bias, vv, carries[h]) for h in range(H_B))

    def body(kb, carries):
        bias = jnp.concatenate(
            [jnp.where(picked_tile(sc_scr[kb, :, LANES * j:LANES * (j + 1)], kb * tk + LANES * j), 0.0, NEG_INF)
             for j in range(n_tiles)], axis=1)
        return attend(pl.multiple_of(kb * tk, tk), tk, bias, carries)

    carries = lax.fori_loop(0, n_full, body, tuple(_softmax_init(tq) for _ in range(H_B)))
    carries = attend(d0, tq, jnp.where(picked_d, 0.0, NEG_INF), carries)
    outs = [acc / l for _, l, acc in carries]
    for pair in range(H_B // 2):
        o_ref[:, LANES * pair:LANES * (pair + 1)] = jnp.where(low, outs[2 * pair], outs[2 * pair + 1]).astype(BF16)


def _dsat_kernel(*refs, p_len, t_len, tq, tk, n_sel):
    if p_len:
        q_ref, qi_ref, wi_ref, new_ref, past_ref, o_ref, kk_scr, vv_scr, ki_scr, sc_scr, scd_scr = refs
    else:
        q_ref, qi_ref, wi_ref, new_ref, o_ref, kk_scr, vv_scr, ki_scr, sc_scr, scd_scr = refs
        past_ref = None
    qi = pl.program_id(1)
    l_len = p_len + t_len
    k_sel = float(n_sel)
    sub = 8

    @pl.when(qi == 0)
    def _fill():
        r1 = lax.broadcasted_iota(jnp.int32, (LANES, 2 * LANES), 0)
        c1 = lax.broadcasted_iota(jnp.int32, (LANES, 2 * LANES), 1)
        dup_kv = jnp.where((c1 % DH_B == r1 % DH_B) & (c1 // LANES == r1 // DH_B), 1.0, 0.0).astype(BF16)
        r2 = lax.broadcasted_iota(jnp.int32, (D_IDX, LANES), 0)
        c2 = lax.broadcasted_iota(jnp.int32, (D_IDX, LANES), 1)
        dup_ki = jnp.where(c2 % D_IDX == r2, 1.0, 0.0).astype(BF16)

        def fill(src_ref, dst, n):
            for r0 in range(0, n, ROW_CHUNK):
                rc = min(ROW_CHUNK, n - r0)
                kv2 = _dot(src_ref[r0:r0 + rc, :LANES].astype(BF16), dup_kv)
                kk_scr[dst + r0:dst + r0 + rc, :] = kv2[:, :LANES].astype(BF16)
                vv_scr[dst + r0:dst + r0 + rc, :] = kv2[:, LANES:].astype(BF16)
                ki_scr[dst + r0:dst + r0 + rc, :] = _dot(src_ref[r0:r0 + rc, LANES:].astype(BF16), dup_ki).astype(BF16)
        if p_len:
            fill(past_ref, 0, p_len)
        fill(new_ref, p_len, t_len)

    d0 = pl.multiple_of(p_len + qi * tq, tq)
    n_full = (p_len + qi * tq) // tk
    lane = lax.broadcasted_iota(jnp.int32, (tq, LANES), 1)
    low = lane < DH_B
    sel_lo = jnp.where(low, 1.0, 0.0).astype(BF16)
    sel_hi = jnp.where(low, 0.0, 1.0).astype(BF16)
    dmask = (lax.broadcasted_iota(jnp.int32, (tq, tq), 0) // CHUNK) <= (lax.broadcasted_iota(jnp.int32, (tq, tq), 1) // CHUNK)

    w_t = jnp.transpose(wi_ref[...])
    q_idx = []
    for pr in range(H_IDX // 2):
        blk = qi_ref[:, LANES * pr:LANES * (pr + 1)]
        q_idx += [blk * sel_lo, blk * sel_hi]

    def idx_score(ki2):
        sc = None
        for h in range(H_IDX):
            term = jnp.maximum(_dot_nt(ki2, q_idx[h]), 0.0) * w_t[h:h + 1, :]
            sc = term if sc is None else sc + term
        return sc

    def score_body(kb, _):
        sc_scr[kb] = idx_score(ki_scr[pl.ds(pl.multiple_of(kb * tk, tk), tk), :])
        return 0

    lax.fori_loop(0, n_full, score_body, 0)
    scd_scr[...] = jnp.where(dmask, idx_score(ki_scr[pl.ds(d0, tq), :]), -jnp.inf)

    def fold(fn, comb, reduce, init):
        def part(x):
            return reduce(x.reshape(x.shape[0] // sub, sub, tq), axis=0)
        acc = lax.fori_loop(0, n_full, lambda kb, a: comb(a, part(fn(sc_scr[kb], kb * tk))),
                            jnp.full((sub, tq), init, F32))
        acc = comb(acc, part(fn(scd_scr[...], d0)))
        return reduce(acc, axis=0, keepdims=True)

    def count_ge(thr):
        return fold(lambda s, _: jnp.where(s >= thr, 1.0, 0.0), jnp.add, jnp.sum, 0.0)

    def count_gt(thr):
        return fold(lambda s, _: jnp.where(s > thr, 1.0, 0.0), jnp.add, jnp.sum, 0.0)

    def min_ge(thr):
        return fold(lambda s, _: jnp.where(s >= thr, s, jnp.inf), jnp.minimum, jnp.min, jnp.inf)

    row_max = fold(lambda s, _: s, jnp.maximum, jnp.max, -jnp.inf)
    row_min = fold(lambda s, _: jnp.where(s > -jnp.inf, s, jnp.inf), jnp.minimum, jnp.min, jnp.inf)

    def bisect(_, lh):
        lo, hi = lh
        mid = lo + 0.5 * (hi - lo)
        ge = count_ge(mid) >= k_sel
        return jnp.where(ge, mid, lo), jnp.where(ge, hi, mid)

    def settle(lo):
        thr = min_ge(lo)
        n_gt = count_gt(thr)
        return thr, n_gt, jnp.max(jnp.where(n_gt >= k_sel, 1, 0))

    hi0 = row_max + jnp.abs(row_max) * 1e-6 + 1e-30
    lo, hi = lax.fori_loop(0, 20, bisect, (row_min, hi0))
    thr, n_gt, bad = settle(lo)

    def more(state):
        lo, hi, _, _, _, it = state
        lo, hi = lax.fori_loop(0, 4, bisect, (lo, hi))
        thr, n_gt, bad = settle(lo)
        return lo, hi, thr, n_gt, bad, it + 1

    _, _, thr, n_gt, _, _ = lax.while_loop(lambda st: (st[4] > 0) & (st[5] < 64), more,
                                           (lo, hi, thr, n_gt, bad, 0))

    n_ge = count_ge(thr)
    need = k_sel - n_gt
    excess = n_ge > k_sel

    def key_index(s, base):
        return base + lax.broadcasted_iota(jnp.int32, s.shape, 0)

    def cut_search(_):
        def count_tied_upto(j):
            return fold(lambda s, base: jnp.where((s == thr) & (key_index(s, base) <= j), 1.0, 0.0),
                        jnp.add, jnp.sum, 0.0)

        def step(_, jj):
            jlo, jhi = jj
            mid = (jlo + jhi) >> 1
            ok = count_tied_upto(mid) >= need
            return jnp.where(ok, jlo, mid), jnp.where(ok, mid, jhi)

        steps = int(math.ceil(math.log2(l_len))) + 1
        _, jhi = lax.fori_loop(0, steps, step,
                               (jnp.full((1, tq), -1, jnp.int32), jnp.full((1, tq), l_len - 1, jnp.int32)))
        return jnp.where(excess, jhi, l_len)

    cut = lax.cond(jnp.max(jnp.where(excess, 1, 0)) > 0, cut_search,
                   lambda _: jnp.full((1, tq), l_len, jnp.int32), 0)

    def bias_of(s, base):
        return jnp.where((s > thr) | ((s == thr) & (key_index(s, base) <= cut)), 0.0, NEG_INF)

    qs = []
    for pair in range(H_B // 2):
        blk = q_ref[:, LANES * pair:LANES * (pair + 1)]
        qs += [blk * sel_lo, blk * sel_hi]

    def attend(r0, n, bias_t, carries):
        bias = jnp.transpose(bias_t)
        kk, vv = kk_scr[pl.ds(r0, n), :], vv_scr[pl.ds(r0, n), :]
        return tuple(_softmax_step(_dot_nt(qs[h], kk) + bias, vv, carries[h]) for h in range(H_B))

    carries = lax.fori_loop(
        0, n_full, lambda kb, c: attend(pl.multiple_of(kb * tk, tk), tk, bias_of(sc_scr[kb], kb * tk), c),
        tuple(_softmax_init(tq) for _ in range(H_B)))
    carries = attend(d0, tq, bias_of(scd_scr[...], d0), carries)
    outs = [acc / l for _, l, acc in carries]
    for pair in range(H_B // 2):
        o_ref[:, LANES * pair:LANES * (pair + 1)] = jnp.where(low, outs[2 * pair], outs[2 * pair + 1]).astype(BF16)


def _dsa(q, qidx, wi, rows, past, layer, tq, tk):
    b, t, _ = q.shape
    p_len = 0 if past is None else past.shape[2]
    l_len = p_len + t
    n_sel = min(TOPK_MAX, l_len // 4)
    n_blocks = max((l_len - tq) // tk, 1)
    args = [q, qidx, wi, rows] + ([past] if past is not None else [])
    transposed = tq % LANES == 0
    body = _dsat_kernel if transposed else _dsa_kernel
    return pl.pallas_call(
        functools.partial(body, p_len=p_len, t_len=t, tq=tq, tk=tk, n_sel=n_sel),
        grid=(b, t // tq),
        in_specs=_attn_specs(tq, [256, 512, LANES], rows, past, layer),
        out_specs=_out_spec(tq),
        out_shape=jax.ShapeDtypeStruct((b, t, 256), BF16),
        scratch_shapes=[pltpu.VMEM((l_len, LANES), BF16)] * 3
                       + [pltpu.VMEM((n_blocks, tk, tq) if transposed else (n_blocks, tq, tk), F32),
                          pltpu.VMEM((tq, tq), F32)],
        compiler_params=_params("parallel", "arbitrary"),
        name="dsa",
    )(*args)


def _gather_cols(w, src, scale):
    src = [int(s) for s in src]
    scale = [float(c) if s >= 0 else 0.0 for s, c in zip(src, scale)]
    pieces, i = [], 0
    while i < len(src):
        j = i + 1
        while j < len(src) and scale[j] == scale[i] and (src[j] == src[j - 1] + 1 if src[i] >= 0 else src[j] < 0):
            j += 1
        if src[i] < 0:
            pieces.append(jnp.zeros((w.shape[0], j - i), w.dtype))
        else:
            run = w[:, src[i]:src[i] + j - i]
            pieces.append(run if scale[i] == 1.0 else run * scale[i])
        i = j
    return jnp.concatenate(pieces, axis=1)


class _Cols:
    def __init__(self):
        self.src, self.scale, self.rope = [], [], []

    def plain(self, start, n, scale=1.0):
        for i in range(n):
            self._add(start + i, scale, None)

    def pad(self, n):
        for _ in range(n):
            self._add(-1, 0.0, None)

    def roped(self, start, d, scale=1.0):
        half = d // 2
        for i in range(d):
            self._add(start + i, scale, (d, i % half, -1.0 if i < half else 1.0))

    def _add(self, src, scale, rope):
        self.src.append(src)
        self.scale.append(scale)
        self.rope.append(rope)

    def weights(self, w):
        return _gather_cols(w, self.src, self.scale).astype(BF16)

    def tables(self, pos):
        cs = {}
        for d in sorted({r[0] for r in self.rope if r is not None}):
            freqs = ROPE_THETA ** (-jnp.arange(d // 2, dtype=F32) * 2.0 / d)
            ang = pos.astype(F32)[:, None] * freqs[None, :]
            cs[d] = (jnp.cos(ang), jnp.sin(ang))
        cos, sin, i, n = [], [], 0, pos.shape[0]
        while i < len(self.rope):
            r, j = self.rope[i], i + 1
            if r is None:
                while j < len(self.rope) and self.rope[j] is None:
                    j += 1
                cos.append(jnp.ones((n, j - i), F32))
                sin.append(jnp.zeros((n, j - i), F32))
            else:
                while j < len(self.rope) and self.rope[j] == (r[0], r[1] + j - i, r[2]):
                    j += 1
                cos.append(cs[r[0]][0][:, r[1]:r[1] + j - i])
                sin.append(r[2] * cs[r[0]][1][:, r[1]:r[1] + j - i])
            i = j
        return jnp.concatenate(cos, axis=1), jnp.concatenate(sin, axis=1)


def _plans():
    p = _Cols()
    p.plain(O_CQ, Q_LORA)
    p.plain(O_CKV, KV_LORA)
    for h in range(H_C):
        p.plain(O_KC + DH_C * h, DH_C)
        p.plain(O_VC + DH_C * h, DH_C)
    for h in range(H_C):
        p.plain(O_QC + DH_C * h, DH_C, DH_C ** -0.5)
        p.pad(LANES - DH_C)
    p.plain(O_WIB, H_IDX)
    p.pad(LANES - H_IDX)
    assert len(p.src) == P_END

    r = _Cols()
    r.roped(O_KB, DH_B)
    r.plain(O_VB, DH_B)
    r.roped(O_KIB, D_IDX)
    r.pad(R_DIFF - (2 * DH_B + D_IDX))
    for h in range(H_D):
        r.roped(O_KD + 2 * DQK_D * h, DQK_D)
        r.roped(O_KD + 2 * DQK_D * h + DQK_D, DQK_D)
        r.plain(O_VD + DV_D * h, DV_D)
    r.roped(O_KR, ROPE_A)
    r.pad(LANES - ROPE_A)
    for h in range(H_B):
        r.roped(O_QB + DH_B * h, DH_B, DH_B ** -0.5)
    for h in range(H_IDX):
        r.roped(O_QIB + D_IDX * h, D_IDX)
    for h in range(H_D):
        r.roped(O_QD + 2 * DQK_D * h, DQK_D)
        r.roped(O_QD + 2 * DQK_D * h + DQK_D, DQK_D)
        r.pad(LANES - 2 * DQK_D)
    assert len(r.src) == R_END

    q = _Cols()
    for h in range(H_A):
        q.plain((NOPE_A + ROPE_A) * h, NOPE_A)
        q.roped((NOPE_A + ROPE_A) * h + NOPE_A, ROPE_A)
        q.pad(LANES - NOPE_A - ROPE_A)
    return p, r, q


def _layer_weights(l, prm, plans):
    p, r, q = plans
    w_in = prm["w_in"][l]
    wkvb = prm["mla_w_kvb"][l]
    k_src, v_src = [], []
    for h in range(H_A):
        k_src += list(range((NOPE_A + V_A) * h, (NOPE_A + V_A) * h + NOPE_A)) + [-1] * (LANES - NOPE_A)
        v_src += list(range((NOPE_A + V_A) * h + NOPE_A, (NOPE_A + V_A) * (h + 1)))
    place = np.zeros((ROPE_A, H_A * LANES), np.float32)
    for h in range(H_A):
        place[np.arange(ROPE_A), LANES * h + NOPE_A + np.arange(ROPE_A)] = 1.0
    dn = jnp.concatenate([jnp.zeros((2 * DQK_D,), F32), prm["diff_norm"][l]])[None, :]

    def ff(tag):
        return (prm[f"norm_{tag}"][l][None, :], prm[f"w_{tag}_gate"][l].astype(BF16),
                prm[f"w_{tag}_up"][l].astype(BF16), prm[f"w_{tag}_down"][l].astype(BF16))

    return {
        "ff1": ff("ff1"), "ff2": ff("ff2"),
        "nm": prm["norm_mix"][l][None, :],
        "wp": p.weights(w_in), "wr": r.weights(w_in),
        "qn": prm["mla_q_norm"][l][None, :], "wqb": q.weights(prm["mla_w_qb"][l]),
        "kvn": prm["mla_kv_norm"][l][None, :],
        "wk": _gather_cols(wkvb, k_src, np.ones(len(k_src))).astype(BF16),
        "pk": jnp.asarray(place).astype(BF16),
        "wv": _gather_cols(wkvb, v_src, np.ones(len(v_src))).astype(BF16),
        "lam": prm["diff_lambda"][l], "dn": dn,
        "wo": prm["w_out"][l].astype(BF16),
    }


def _trunk(x, caches, pos, layers, plans, fnorm, tq, tk, proj_fold):
    b, t, _ = x.shape
    depth = len(layers)
    tm = min(512, b * t)
    _, r_plan, q_plan = plans
    pb, pt = (b // proj_fold, t * proj_fold)
    pos_p = jnp.tile(pos, proj_fold)
    tabs = r_plan.tables(pos_p) + q_plan.tables(pos_p)
    tq_p = min(256, pt)
    xf = x.reshape(b * t, D_MODEL)
    stacked = [jnp.zeros((depth, pb, pt * r, w), F32)
               for r, w in ((1, KV_LORA + ROPE_A), (1, 2 * DH_B + D_IDX), (H_C, 2 * DH_C), (H_D, 2 * DQK_D + DV_D))]
    past = [None] * 4 if caches is None else caches
    for l, lw in enumerate(layers):
        lam_init = 0.8 - 0.6 * math.exp(-0.3 * l)
        xf = _ffn(xf, None, None, *lw["ff1"], None, tm)
        outs = _proj(xf.reshape(pb, pt, D_MODEL), lw, tabs, stacked, l, tq_p)
        stacked = list(outs[:4])
        r_sb, r_diff, q_mla, q_dsa, q_idx, w_idx, q_sb, q_diff = [o.reshape(b, t, o.shape[-1]) for o in outs[4:]]
        r_mla, r_dsa = (a.reshape(depth, b, t, a.shape[-1]) for a in stacked[:2])
        mix = (
            _mla(q_mla, r_mla, past[0], l, lw, tq, tk),
            _dsa(q_dsa, q_idx, w_idx, r_dsa, past[1], l, tq, tk),
            _sb(q_sb, r_sb, past[2], l, tq, tk),
            _diff(q_diff, r_diff, past[3], l, lw, lam_init, tq, tk),
        )
        mix = [m.reshape(b * t, 256) for m in mix]
        xf = _ffn(xf, mix, lw["wo"], *lw["ff2"], fnorm if l == depth - 1 else None, tm)
    y = xf.reshape(b, t, D_MODEL)
    return (y, stacked[0].reshape(depth, b, t, -1), stacked[1].reshape(depth, b, t, -1),
            stacked[2].reshape(depth, b, t, H_C, 2 * DH_C),
            stacked[3].reshape(depth, b, t, H_D, 2 * DQK_D + DV_D))


def kernel(x_prompt, x_sample, cache_mla, cache_dsa, cache_sb, cache_diff, norm_ff1, w_ff1_gate, w_ff1_up, w_ff1_down, norm_mix, w_in, mla_q_norm, mla_w_qb, mla_kv_norm, mla_w_kvb, diff_lambda, diff_norm, w_out, norm_ff2, w_ff2_gate, w_ff2_up, w_ff2_down, final_norm):
    prm = dict(norm_ff1=norm_ff1, w_ff1_gate=w_ff1_gate, w_ff1_up=w_ff1_up, w_ff1_down=w_ff1_down,
               norm_mix=norm_mix, w_in=w_in, mla_q_norm=mla_q_norm, mla_w_qb=mla_w_qb,
               mla_kv_norm=mla_kv_norm, mla_w_kvb=mla_w_kvb, diff_lambda=diff_lambda, diff_norm=diff_norm,
               w_out=w_out, norm_ff2=norm_ff2, w_ff2_gate=w_ff2_gate, w_ff2_up=w_ff2_up, w_ff2_down=w_ff2_down)
    depth = w_in.shape[0]
    plans = _plans()
    layers = [_layer_weights(l, prm, plans) for l in range(depth)]
    fnorm = final_norm[None, :]

    t_p = x_prompt.shape[1]
    assert t_p % CHUNK == 0
    tq_p = min(512, t_p)
    out_p = _trunk(x_prompt, None, jnp.arange(t_p, dtype=jnp.int32), layers, plans, fnorm, tq_p, tq_p, 1)


    b_s, t_s, _ = x_sample.shape
    past = cache_mla.shape[2]
    assert past % CHUNK == 0 and t_s <= CHUNK, "new sample frames must sit in one chunk after whole cached chunks"
    tk_s = math.gcd(past, 512)
    fold_heads = lambda c: c.reshape(c.shape[:2] + (c.shape[2] * c.shape[3], c.shape[4]))
    caches = (cache_mla, cache_dsa, fold_heads(cache_sb), fold_heads(cache_diff))
    fold = math.gcd(b_s, max(1, 256 // t_s))
    out_s = _trunk(x_sample, caches, past + jnp.arange(t_s, dtype=jnp.int32), layers, plans, fnorm, t_s, tk_s, fold)

    return (out_p[0], out_s[0]) + tuple(out_p[1:]) + tuple(out_s[1:])
```

```python
import functools
import math

import numpy as np
import jax
import jax.numpy as jnp
from jax import lax
from jax.experimental import pallas as pl
from jax.experimental.pallas import tpu as pltpu

F32 = jnp.float32
BF16 = jnp.bfloat16

D_MODEL = 1024
CHUNK = 64
ROPE_THETA = 10000.0
EPS = 1e-6
NEG_INF = -1e30
HEAD_DIM = D_MODEL // 16
D_FF = 11 * D_MODEL // 4
H_A = 4
Q_LORA = 4 * HEAD_DIM
KV_LORA = 2 * HEAD_DIM
NOPE_A = HEAD_DIM
ROPE_A = HEAD_DIM // 2
V_A = HEAD_DIM
H_B = 4
DH_B = HEAD_DIM
H_IDX = 8
D_IDX = HEAD_DIM
TOPK_MAX = 256
H_C = 4
DH_C = HEAD_DIM
H_D = 4
DQK_D = HEAD_DIM // 2
DV_D = HEAD_DIM
COL_SIZES = (Q_LORA, KV_LORA, ROPE_A,
             H_B * DH_B, DH_B, DH_B, H_IDX * D_IDX, D_IDX, H_IDX,
             H_C * DH_C, H_C * DH_C, H_C * DH_C,
             H_D * 2 * DQK_D, H_D * 2 * DQK_D, H_D * DV_D)
(O_CQ, O_CKV, O_KR, O_QB, O_KB, O_VB, O_QIB, O_KIB, O_WIB,
 O_QC, O_KC, O_VC, O_QD, O_KD, O_VD) = np.concatenate([[0], np.cumsum(COL_SIZES)[:-1]]).tolist()

LANES = 128
VMEM_LIMIT = 56 * 1024 * 1024
MXU_TILE = 256
FF_CHUNK = MXU_TILE
ROW_CHUNK = 512

P_CQ, P_CKV, P_SB, P_QSB, P_WI, P_END = 0, 256, 384, 896, 1408, 1536
R_DSA, R_DIFF, R_KR, R_QDSA, R_QI, R_QDIFF, R_END = 0, 256, 768, 896, 1152, 1664, 2176


def _dot(a, b):
    return jnp.dot(a, b, preferred_element_type=F32)


def _dot_nt(a, b):
    return lax.dot_general(a, b, (((1,), (1,)), ((), ())), preferred_element_type=F32)


def _rms(x, g):
    return x * lax.rsqrt(jnp.mean(x * x, axis=-1, keepdims=True) + EPS) * g


def _const_spec(shape):
    n = len(shape)
    return pl.BlockSpec(shape, lambda *_: (0,) * n, pipeline_mode=pl.Buffered(1))


def _params(*sem):
    return pltpu.CompilerParams(dimension_semantics=sem, vmem_limit_bytes=VMEM_LIMIT)


def _ffn_kernel(*refs, has_mix, has_final):
    it = iter(refs)
    x_ref = next(it)
    if has_mix:
        o_refs = [next(it) for _ in range(4)]
        wo_ref = next(it)
    g_ref, wg_ref, wu_ref, wd_ref = next(it), next(it), next(it), next(it)
    fn_ref = next(it) if has_final else None
    out_ref = next(it)

    x = x_ref[...]
    if has_mix:
        for i, o_ref in enumerate(o_refs):
            x = x + _dot(o_ref[...], wo_ref[256 * i:256 * (i + 1), :])
    xn = _rms(x, g_ref[...]).astype(BF16)
    acc = jnp.zeros_like(x)
    for c0 in range(0, D_FF, FF_CHUNK):
        gate = _dot(xn, wg_ref[:, c0:c0 + FF_CHUNK])
        up = _dot(xn, wu_ref[:, c0:c0 + FF_CHUNK])
        hid = (gate * jax.nn.sigmoid(gate) * up).astype(BF16)
        acc = acc + _dot(hid, wd_ref[c0:c0 + FF_CHUNK, :])
    y = x + 0.5 * acc
    if has_final:
        y = _rms(y, fn_ref[...])
    out_ref[...] = y


def _ffn(x, mix, wo, g, wg, wu, wd, fnorm, tm):
    n = x.shape[0]
    has_mix, has_final = mix is not None, fnorm is not None
    row = lambda w: pl.BlockSpec((tm, w), lambda i: (i, 0))
    args, specs = [x], [row(D_MODEL)]
    if has_mix:
        args += list(mix) + [wo]
        specs += [row(256)] * 4 + [_const_spec(wo.shape)]
    args += [g, wg, wu, wd]
    specs += [_const_spec(g.shape), _const_spec(wg.shape), _const_spec(wu.shape), _const_spec(wd.shape)]
    if has_final:
        args.append(fnorm)
        specs.append(_const_spec(fnorm.shape))
    return pl.pallas_call(
        functools.partial(_ffn_kernel, has_mix=has_mix, has_final=has_final),
        grid=(n // tm,),
        in_specs=specs,
        out_specs=row(D_MODEL),
        out_shape=jax.ShapeDtypeStruct((n, D_MODEL), F32),
        compiler_params=_params("parallel"),
        name="ffn",
    )(*args)


def _rope(h, cos, sin_signed, d):
    w = h.shape[1]
    first = lax.broadcasted_iota(jnp.int32, h.shape, 1) % d < d // 2
    partner = jnp.where(first, pltpu.roll(h, w - d // 2, axis=1), pltpu.roll(h, d // 2, axis=1))
    return h * cos + partner * sin_signed


def _proj_kernel(x_ref, g_ref, wp_ref, wr_ref, cr_ref, sr_ref,
                 qn_ref, wqb_ref, cq_ref, sq_ref, kvn_ref,
                 _amla, _adsa, _asb, _adiff,
                 rmla_ref, rdsa_ref, rsbh_ref, rdiffh_ref, rsb_ref, rdiff_ref,
                 qmla_ref, qdsa_ref, qi_ref, wi_ref, qsb_ref, qdiff_ref):
    xn = _rms(x_ref[...], g_ref[...]).astype(BF16)
    tq = x_ref.shape[0]

    def plain(a, b):
        return _dot(xn, wp_ref[:, a:b])

    def roped(a, b, d):
        return _rope(_dot(xn, wr_ref[:, a:b]), cr_ref[:, a:b], sr_ref[:, a:b], d)

    cqn = _rms(plain(P_CQ, P_CKV), qn_ref[...]).astype(BF16)
    qmla_ref[...] = _rope(_dot(cqn, wqb_ref[...]), cq_ref[...], sq_ref[...], ROPE_A).astype(BF16)
    rmla_ref[:, :KV_LORA] = _rms(plain(P_CKV, P_SB), kvn_ref[...])
    rmla_ref[:, KV_LORA:] = roped(R_KR, R_QDSA, ROPE_A)[:, :ROPE_A]
    rdsa_ref[...] = roped(R_DSA, R_DIFF, DH_B)[:, :2 * DH_B + D_IDX]
    qdsa_ref[...] = roped(R_QDSA, R_QI, DH_B).astype(BF16)
    qi_ref[...] = roped(R_QI, R_QDIFF, D_IDX).astype(BF16)
    wi_ref[...] = plain(P_WI, P_END) * (H_IDX ** -0.5)
    rsb = plain(P_SB, P_QSB)
    rsb_ref[...] = rsb
    for h in range(H_C):
        rsbh_ref[pl.ds(h, tq, stride=H_C), :] = rsb[:, LANES * h:LANES * (h + 1)]
    qsb_ref[...] = plain(P_QSB, P_WI).astype(BF16)
    rdiff = roped(R_DIFF, R_KR, DQK_D)
    rdiff_ref[...] = rdiff
    for h in range(H_D):
        rdiffh_ref[pl.ds(h, tq, stride=H_D), :] = rdiff[:, LANES * h:LANES * (h + 1)]
    qdiff_ref[...] = roped(R_QDIFF, R_END, DQK_D).astype(BF16)


def _proj(x, lw, tabs, stacked, layer, tq):
    b, t, _ = x.shape
    cr, sr, cq, sq = tabs
    tok = lambda w: pl.BlockSpec((None, tq, w), lambda bi, i: (bi, i, 0))
    lay = lambda r, w: pl.BlockSpec((None, None, r, w), lambda bi, i: (layer, bi, i, 0))
    tab = lambda w: pl.BlockSpec((tq, w), lambda bi, i: (i, 0))
    consts = [lw["nm"], lw["wp"], lw["wr"]]
    consts2 = [lw["qn"], lw["wqb"]]
    in_specs = ([tok(D_MODEL)] + [_const_spec(a.shape) for a in consts] + [tab(R_END), tab(R_END)]
                + [_const_spec(a.shape) for a in consts2] + [tab(512), tab(512), _const_spec(lw["kvn"].shape)]
                + [pl.BlockSpec(memory_space=pl.ANY)] * 4)
    widths = [(512, F32), (512, F32),
              (512, BF16), (256, BF16), (512, BF16), (LANES, F32), (512, BF16), (512, BF16)]
    n_in = len(in_specs)
    return pl.pallas_call(
        _proj_kernel,
        grid=(b, t // tq),
        in_specs=in_specs,
        out_specs=[lay(tq, 160), lay(tq, 192), lay(tq * H_C, LANES), lay(tq * H_D, LANES)]
                  + [tok(w) for w, _ in widths],
        out_shape=[jax.ShapeDtypeStruct(a.shape, a.dtype) for a in stacked]
                  + [jax.ShapeDtypeStruct((b, t, w), dt) for w, dt in widths],
        input_output_aliases={n_in - 4 + k: k for k in range(4)},
        compiler_params=_params("parallel", "parallel"),
        name="proj",
    )(x, *consts, cr, sr, *consts2, cq, sq, lw["kvn"], *stacked)


def _chunk_mask(tq):
    ri = lax.broadcasted_iota(jnp.int32, (tq, tq), 0)
    ci = lax.broadcasted_iota(jnp.int32, (tq, tq), 1)
    return (ci // CHUNK) <= (ri // CHUNK)


LOG2E = math.log2(math.e)


def _softmax_step(s, v, carry, scale=1.0):
    m, l, acc = carry
    c = scale * LOG2E
    m_new = jnp.maximum(m, jnp.max(s, axis=-1, keepdims=True))
    alpha = jnp.exp2((m - m_new) * c)
    p = jnp.exp2((s - m_new) * c)
    l = alpha * l + jnp.sum(p, axis=-1, keepdims=True)
    acc = alpha * acc + _dot(p.astype(BF16), v)
    return m_new, l, acc


def _softmax_init(tq):
    return (jnp.full((tq, 1), NEG_INF, F32), jnp.zeros((tq, 1), F32), jnp.zeros((tq, LANES), F32))


def _lane_lt(tq, n):
    return lax.broadcasted_iota(jnp.int32, (tq, LANES), 1) < n


def _attn_specs(tq, q_widths, rows, past, layer):
    def whole(a):
        if a.ndim == 3:
            return pl.BlockSpec((None,) + a.shape[1:], lambda bi, i: (bi, 0, 0))
        return pl.BlockSpec((None, None) + a.shape[2:], lambda bi, i: (layer, bi, 0, 0))
    specs = [pl.BlockSpec((None, tq, w), lambda bi, i: (bi, i, 0)) for w in q_widths]
    return specs + [whole(rows)] + ([whole(past)] if past is not None else [])


def _out_spec(tq):
    return pl.BlockSpec((None, tq, 256), lambda bi, i: (bi, i, 0))


def _mla_kernel(*refs, p_len, t_len, tq, tk):
    if p_len:
        q_ref, new_ref, past_ref, wk_ref, pk_ref, wv_ref, o_ref, k_scr, v_scr = refs
    else:
        q_ref, new_ref, wk_ref, pk_ref, wv_ref, o_ref, k_scr, v_scr = refs
        past_ref = None
    qi = pl.program_id(1)
    scale = (NOPE_A + ROPE_A) ** -0.5

    @pl.when(qi == 0)
    def _fill():
        def fill(src_ref, dst, n):
            for r0 in range(0, n, ROW_CHUNK):
                rc = min(ROW_CHUNK, n - r0)
                lat = src_ref[r0:r0 + rc, :KV_LORA].astype(BF16)
                rope = src_ref[r0:r0 + rc, KV_LORA:].astype(BF16)
                k_scr[dst + r0:dst + r0 + rc, :] = (_dot(lat, wk_ref[...]) + _dot(rope, pk_ref[...])).astype(BF16)
                v_scr[dst + r0:dst + r0 + rc, :] = _dot(lat, wv_ref[...]).astype(BF16)
        if p_len:
            fill(past_ref, 0, p_len)
        fill(new_ref, p_len, t_len)

    d0 = pl.multiple_of(p_len + qi * tq, tq)
    n_full = (p_len + qi * tq) // tk
    dmask = _chunk_mask(tq)
    low = _lane_lt(tq, V_A)
    qs = [q_ref[:, LANES * h:LANES * (h + 1)] for h in range(H_A)]

    def step(r0, n, mask, carries):
        out = []
        for h in range(H_A):
            s = _dot_nt(qs[h], k_scr[pl.ds(r0, n), LANES * h:LANES * (h + 1)])
            if mask is not None:
                s = jnp.where(mask, s, NEG_INF)
            out.append(_softmax_step(s, v_scr[pl.ds(r0, n), LANES * (h // 2):LANES * (h // 2 + 1)], carries[h],
                                     scale))
        return tuple(out)

    carries = lax.fori_loop(0, n_full, lambda kb, c: step(pl.multiple_of(kb * tk, tk), tk, None, c),
                            tuple(_softmax_init(tq) for _ in range(H_A)))
    carries = step(d0, tq, dmask, carries)
    outs = [acc / l for _, l, acc in carries]
    for pair in range(H_A // 2):
        o_ref[:, LANES * pair:LANES * (pair + 1)] = jnp.where(low, outs[2 * pair], outs[2 * pair + 1]).astype(BF16)


def _mla(q, rows, past, layer, lw, tq, tk):
    b, t, _ = q.shape
    p_len = 0 if past is None else past.shape[2]
    consts = [lw["wk"], lw["pk"], lw["wv"]]
    args = [q, rows] + ([past] if past is not None else []) + consts
    return pl.pallas_call(
        functools.partial(_mla_kernel, p_len=p_len, t_len=t, tq=tq, tk=tk),
        grid=(b, t // tq),
        in_specs=_attn_specs(tq, [512], rows, past, layer) + [_const_spec(a.shape) for a in consts],
        out_specs=_out_spec(tq),
        out_shape=jax.ShapeDtypeStruct((b, t, 256), BF16),
        scratch_shapes=[pltpu.VMEM((p_len + t, 512), BF16), pltpu.VMEM((p_len + t, 256), BF16)],
        compiler_params=_params("parallel", "arbitrary"),
        name="mla",
    )(*args)


def _diff_kernel(*refs, p_len, t_len, tq, tk, lam_init):
    if p_len:
        q_ref, new_ref, past_ref, lam_ref, dn_ref, o_ref = refs
    else:
        q_ref, new_ref, lam_ref, dn_ref, o_ref = refs
        past_ref = None
    qi = pl.program_id(1)
    scale = DQK_D ** -0.5
    lam = lam_ref[...]
    lam_full = (jnp.exp(jnp.sum(lam[0:1] * lam[1:2], axis=-1, keepdims=True))
                - jnp.exp(jnp.sum(lam[2:3] * lam[3:4], axis=-1, keepdims=True)) + lam_init)
    dmask = _chunk_mask(tq)
    lane = lax.broadcasted_iota(jnp.int32, (tq, LANES), 1)
    sel1 = jnp.where(lane < DQK_D, 1.0, 0.0).astype(BF16)
    sel2 = jnp.where((lane >= DQK_D) & (lane < 2 * DQK_D), 1.0, 0.0).astype(BF16)
    is_v = lane >= 2 * DQK_D
    n_past = p_len // tk
    n_new = (qi * tq) // tk

    q12 = []
    for h in range(H_D):
        qh = q_ref[:, LANES * h:LANES * (h + 1)]
        q12 += [qh * sel1, qh * sel2]

    def step(load, mask, carries):
        out = []
        for h in range(H_D):
            kv = load(h)
            for m in range(2):
                s = _dot_nt(q12[2 * h + m], kv)
                if mask is not None:
                    s = jnp.where(mask, s, NEG_INF)
                out.append(_softmax_step(s, kv, carries[2 * h + m], scale))
        return tuple(out)

    def new_rows(r0, n):
        return lambda h: new_ref[pl.ds(r0, n), LANES * h:LANES * (h + 1)].astype(BF16)

    def past_rows(kb):
        return lambda h: past_ref[pl.ds(H_D * kb * tk + h, tk, stride=H_D), :].astype(BF16)

    carries = tuple(_softmax_init(tq) for _ in range(2 * H_D))
    for kb in range(n_past):
        carries = step(past_rows(kb), None, carries)
    carries = lax.fori_loop(0, n_new, lambda kb, c: step(new_rows(pl.multiple_of(kb * tk, tk), tk), None, c),
                            carries)
    carries = step(new_rows(pl.multiple_of(qi * tq, tq), tq), dmask, carries)
    outs = []
    for h in range(H_D):
        (_, l1, a1), (_, l2, a2) = carries[2 * h], carries[2 * h + 1]
        o = a1 / l1 - lam_full * (a2 / l2)
        ms = jnp.sum(jnp.where(is_v, o * o, 0.0), axis=-1, keepdims=True) * (1.0 / DV_D)
        outs.append(o * lax.rsqrt(ms + EPS) * dn_ref[...] * (1.0 - lam_init))
    for pair in range(H_D // 2):
        o_ref[:, LANES * pair:LANES * (pair + 1)] = jnp.where(
            is_v, outs[2 * pair + 1], pltpu.roll(outs[2 * pair], 2 * DQK_D, axis=1)).astype(BF16)


def _diff(q, rows, past, layer, lw, lam_init, tq, tk):
    b, t, _ = q.shape
    p_len = 0 if past is None else past.shape[2] // H_D
    consts = [lw["lam"], lw["dn"]]
    args = [q, rows] + ([past] if past is not None else []) + consts
    return pl.pallas_call(
        functools.partial(_diff_kernel, p_len=p_len, t_len=t, tq=tq, tk=tk, lam_init=lam_init),
        grid=(b, t // tq),
        in_specs=_attn_specs(tq, [512], rows, past, layer) + [_const_spec(a.shape) for a in consts],
        out_specs=_out_spec(tq),
        out_shape=jax.ShapeDtypeStruct((b, t, 256), BF16),
        compiler_params=_params("parallel", "parallel"),
        name="diff",
    )(*args)


def _sb_kernel(*refs, p_len, t_len, tq, tk):
    if p_len:
        q_ref, new_ref, past_ref, o_ref = refs
    else:
        q_ref, new_ref, o_ref = refs
        past_ref = None
    qi = pl.program_id(1)
    n_past = p_len // tk
    n_new = (qi * tq) // tk

    def later(n):
        n = min(n, MXU_TILE)
        ri = lax.broadcasted_iota(jnp.int32, (n, n), 0)
        ci = lax.broadcasted_iota(jnp.int32, (n, n), 1)
        return jnp.where(ri > ci, 1.0, 0.0).astype(BF16)

    def later_sums(x, tri):
        w = tri.shape[0]
        pieces, after = [], None
        for j in reversed(range(x.shape[1] // w)):
            xj = x[:, w * j:w * (j + 1)]
            hi = xj.astype(BF16)
            lo = (xj - hi.astype(F32)).astype(BF16)
            inside = _dot(hi, tri) + _dot(lo, tri)
            total = inside[:, :1] + xj[:, :1]
            if after is not None:
                inside, total = inside + after, total + after
            pieces.append(inside)
            after = total
        return (pieces[0] if len(pieces) == 1 else jnp.concatenate(pieces[::-1], axis=1)), after

    later_k = later(tk)
    later_q = later_k if min(tq, MXU_TILE) == min(tk, MXU_TILE) else later(tq)
    ri = lax.broadcasted_iota(jnp.int32, (tq, tq), 0)
    ci = lax.broadcasted_iota(jnp.int32, (tq, tq), 1)
    causal = ci < ri
    is_v = lax.broadcasted_iota(jnp.int32, (tq, LANES), 1) >= DH_C

    qs = [q_ref[:, LANES * h:LANES * (h + 1)] for h in range(H_C)]

    def step(load, mask, tri, carries):
        out = []
        for h in range(H_C):
            run, acc = carries[h]
            kv = load(h)
            z = _dot_nt(qs[h], kv)
            log_take = jnp.minimum(z, 0.0) - jnp.log(1.0 + jnp.exp2(jnp.abs(z) * -LOG2E))
            log_keep = log_take - z
            if mask is not None:
                log_keep = jnp.where(mask, log_keep, 0.0)
            later_sum, block_sum = later_sums(log_keep, tri)
            a = jnp.exp2((log_take + (later_sum + run)) * LOG2E)
            if mask is not None:
                a = jnp.where(mask, a, 0.0)
            acc = acc + _dot(a.astype(BF16), kv)
            run = run + block_sum
            out.append((run, acc))
        return tuple(out)

    def new_rows(r0, n):
        return lambda h: new_ref[pl.ds(r0, n), LANES * h:LANES * (h + 1)].astype(BF16)

    def past_rows(kb):
        return lambda h: past_ref[pl.ds(H_C * kb * tk + h, tk, stride=H_C), :].astype(BF16)

    carries = tuple((jnp.zeros((tq, 1), F32), jnp.zeros((tq, LANES), F32)) for _ in range(H_C))
    carries = step(new_rows(pl.multiple_of(qi * tq, tq), tq), causal, later_q, carries)
    carries = lax.fori_loop(
        0, n_new,
        lambda i, c: step(new_rows(pl.multiple_of((n_new - 1 - i) * tk, tk), tk), None, later_k, c), carries)
    for kb in reversed(range(n_past)):
        carries = step(past_rows(kb), None, later_k, carries)
    for pair in range(H_C // 2):
        o_ref[:, LANES * pair:LANES * (pair + 1)] = jnp.where(
            is_v, carries[2 * pair + 1][1], pltpu.roll(carries[2 * pair][1], DH_C, axis=1)).astype(BF16)


def _sb(q, rows, past, layer, tq, tk):
    b, t, _ = q.shape
    p_len = 0 if past is None else past.shape[2] // H_C
    args = [q, rows] + ([past] if past is not None else [])
    return pl.pallas_call(
        functools.partial(_sb_kernel, p_len=p_len, t_len=t, tq=tq, tk=tk),
        grid=(b, t // tq),
        in_specs=_attn_specs(tq, [512], rows, past, layer),
        out_specs=_out_spec(tq),
        out_shape=jax.ShapeDtypeStruct((b, t, 256), BF16),
        compiler_params=_params("parallel", "parallel"),
        name="sb",
    )(*args)


def _dsa_kernel(*refs, p_len, t_len, tq, tk, n_sel):
    if p_len:
        q_ref, qi_ref, wi_ref, new_ref, past_ref, o_ref, kk_scr, vv_scr, ki_scr, sc_scr, scd_scr = refs
    else:
        q_ref, qi_ref, wi_ref, new_ref, o_ref, kk_scr, vv_scr, ki_scr, sc_scr, scd_scr = refs
        past_ref = None
    qi = pl.program_id(1)
    l_len = p_len + t_len
    n_tiles = tk // LANES
    k_sel = float(n_sel)

    @pl.when(qi == 0)
    def _fill():
        r1 = lax.broadcasted_iota(jnp.int32, (LANES, 2 * LANES), 0)
        c1 = lax.broadcasted_iota(jnp.int32, (LANES, 2 * LANES), 1)
        dup_kv = jnp.where((c1 % DH_B == r1 % DH_B) & (c1 // LANES == r1 // DH_B), 1.0, 0.0).astype(BF16)
        r2 = lax.broadcasted_iota(jnp.int32, (D_IDX, LANES), 0)
        c2 = lax.broadcasted_iota(jnp.int32, (D_IDX, LANES), 1)
        dup_ki = jnp.where(c2 % D_IDX == r2, 1.0, 0.0).astype(BF16)

        def fill(src_ref, dst, n):
            for r0 in range(0, n, ROW_CHUNK):
                rc = min(ROW_CHUNK, n - r0)
                kv2 = _dot(src_ref[r0:r0 + rc, :LANES].astype(BF16), dup_kv)
                kk_scr[dst + r0:dst + r0 + rc, :] = kv2[:, :LANES].astype(BF16)
                vv_scr[dst + r0:dst + r0 + rc, :] = kv2[:, LANES:].astype(BF16)
                ki_scr[dst + r0:dst + r0 + rc, :] = _dot(src_ref[r0:r0 + rc, LANES:].astype(BF16), dup_ki).astype(BF16)
        if p_len:
            fill(past_ref, 0, p_len)
        fill(new_ref, p_len, t_len)

    d0 = pl.multiple_of(p_len + qi * tq, tq)
    n_full = (p_len + qi * tq) // tk
    lane = lax.broadcasted_iota(jnp.int32, (tq, LANES), 1)
    low = lane < DH_B
    sel_lo = jnp.where(low, 1.0, 0.0).astype(BF16)
    sel_hi = jnp.where(low, 0.0, 1.0).astype(BF16)
    dmask = _chunk_mask(tq)

    wi = wi_ref[...]
    wcol = [jnp.sum(jnp.where(lane == h, wi, 0.0), axis=-1, keepdims=True) for h in range(H_IDX)]
    q_idx = []
    for pr in range(H_IDX // 2):
        blk = qi_ref[:, LANES * pr:LANES * (pr + 1)]
        q_idx += [blk * sel_lo, blk * sel_hi]

    def idx_score(ki2):
        sc = None
        for h in range(H_IDX):
            term = jnp.maximum(_dot_nt(q_idx[h], ki2), 0.0) * wcol[h]
            sc = term if sc is None else sc + term
        return sc

    def score_body(kb, _):
        r0 = pl.multiple_of(kb * tk, tk)
        sc_scr[kb] = idx_score(ki_scr[pl.ds(r0, tk), :])
        return 0

    lax.fori_loop(0, n_full, score_body, 0)
    scd_scr[...] = jnp.where(dmask, idx_score(ki_scr[pl.ds(d0, tq), :]), -jnp.inf)

    def fold(tile_fn, diag_fn, comb, reduce, init):
        def body(kb, acc):
            for j in range(n_tiles):
                acc = comb(acc, tile_fn(sc_scr[kb, :, LANES * j:LANES * (j + 1)], kb * tk + LANES * j))
            return acc
        acc = lax.fori_loop(0, n_full, body, jnp.full((tq, LANES), init, F32))
        return comb(reduce(acc, axis=-1, keepdims=True), reduce(diag_fn(scd_scr[...]), axis=-1, keepdims=True))

    def wide(v):
        return jnp.broadcast_to(v, (tq, LANES))

    def count_ge(thr):
        thr_w = wide(thr)
        return fold(lambda s, _: jnp.where(s >= thr_w, 1.0, 0.0), lambda s: jnp.where(s >= thr, 1.0, 0.0),
                    jnp.add, jnp.sum, 0.0)

    def count_gt(thr):
        thr_w = wide(thr)
        return fold(lambda s, _: jnp.where(s > thr_w, 1.0, 0.0), lambda s: jnp.where(s > thr, 1.0, 0.0),
                    jnp.add, jnp.sum, 0.0)

    def min_ge(thr):
        thr_w = wide(thr)
        return fold(lambda s, _: jnp.where(s >= thr_w, s, jnp.inf), lambda s: jnp.where(s >= thr, s, jnp.inf),
                    jnp.minimum, jnp.min, jnp.inf)

    row_max = fold(lambda s, _: s, lambda s: s, jnp.maximum, jnp.max, -jnp.inf)
    row_min = fold(lambda s, _: s, lambda s: jnp.where(s > -jnp.inf, s, jnp.inf), jnp.minimum, jnp.min, jnp.inf)

    def bisect(_, lh):
        lo, hi = lh
        mid = lo + 0.5 * (hi - lo)
        ge = count_ge(mid) >= k_sel
        return jnp.where(ge, mid, lo), jnp.where(ge, hi, mid)

    def settle(lo):
        thr = min_ge(lo)
        n_gt = count_gt(thr)
        return thr, n_gt, jnp.max(jnp.where(n_gt >= k_sel, 1, 0))

    hi0 = row_max + jnp.abs(row_max) * 1e-6 + 1e-30
    lo, hi = lax.fori_loop(0, 20, bisect, (row_min, hi0))
    thr, n_gt, bad = settle(lo)

    def more(state):
        lo, hi, _, _, _, it = state
        lo, hi = lax.fori_loop(0, 4, bisect, (lo, hi))
        thr, n_gt, bad = settle(lo)
        return lo, hi, thr, n_gt, bad, it + 1

    _, _, thr, n_gt, _, _ = lax.while_loop(lambda st: (st[4] > 0) & (st[5] < 64), more,
                                           (lo, hi, thr, n_gt, bad, 0))

    n_ge = count_ge(thr)
    need = k_sel - n_gt
    excess = n_ge > k_sel
    thr_w = wide(thr)

    def cut_search(_):
        def count_tied_upto(j):
            j_w = wide(j)
            def tile(s, base):
                idx = base + lane
                return jnp.where((s == thr_w) & (idx <= j_w), 1.0, 0.0)
            def diag(s):
                idx = d0 + lax.broadcasted_iota(jnp.int32, (tq, tq), 1)
                return jnp.where((s == thr) & (idx <= j), 1.0, 0.0)
            return fold(tile, diag, jnp.add, jnp.sum, 0.0)

        def step(_, jj):
            jlo, jhi = jj
            mid = (jlo + jhi) >> 1
            ok = count_tied_upto(mid) >= need
            return jnp.where(ok, jlo, mid), jnp.where(ok, mid, jhi)

        steps = int(math.ceil(math.log2(l_len))) + 1
        _, jhi = lax.fori_loop(0, steps, step,
                               (jnp.full((tq, 1), -1, jnp.int32), jnp.full((tq, 1), l_len - 1, jnp.int32)))
        return jnp.where(excess, jhi, l_len)

    cut = lax.cond(jnp.max(jnp.where(excess, 1, 0)) > 0, cut_search,
                   lambda _: jnp.full((tq, 1), l_len, jnp.int32), 0)
    cut_w = wide(cut)

    def picked_tile(s, base):
        return (s > thr_w) | ((s == thr_w) & (base + lane <= cut_w))

    idx_d = d0 + lax.broadcasted_iota(jnp.int32, (tq, tq), 1)
    sd = scd_scr[...]
    picked_d = (sd > thr) | ((sd == thr) & (idx_d <= cut))

    qs = []
    for pair in range(H_B // 2):
        blk = q_ref[:, LANES * pair:LANES * (pair + 1)]
        qs += [blk * sel_lo, blk * sel_hi]

    def attend(r0, n, bias, carries):
        kk, vv = kk_scr[pl.ds(r0, n), :], vv_scr[pl.ds(r0, n), :]
        return tuple(_softmax_step(_dot_nt(qs[h], kk) + bias, vv, carries[h]) for h in range(H_B))

    def body(kb, carries):
        bias = jnp.concatenate(
            [jnp.where(picked_tile(sc_scr[kb, :, LANES * j:LANES * (j + 1)], kb * tk + LANES * j), 0.0, NEG_INF)
             for j in range(n_tiles)], axis=1)
        return attend(pl.multiple_of(kb * tk, tk), tk, bias, carries)

    carries = lax.fori_loop(0, n_full, body, tuple(_softmax_init(tq) for _ in range(H_B)))
    carries = attend(d0, tq, jnp.where(picked_d, 0.0, NEG_INF), carries)
    outs = [acc / l for _, l, acc in carries]
    for pair in range(H_B // 2):
        o_ref[:, LANES * pair:LANES * (pair + 1)] = jnp.where(low, outs[2 * pair], outs[2 * pair + 1]).astype(BF16)


def _dsat_kernel(*refs, p_len, t_len, tq, tk, n_sel):
    if p_len:
        q_ref, qi_ref, wi_ref, new_ref, past_ref, o_ref, kk_scr, vv_scr, ki_scr, sc_scr, scd_scr = refs
    else:
        q_ref, qi_ref, wi_ref, new_ref, o_ref, kk_scr, vv_scr, ki_scr, sc_scr, scd_scr = refs
        past_ref = None
    qi = pl.program_id(1)
    l_len = p_len + t_len
    k_sel = float(n_sel)
    sub = 8

    @pl.when(qi == 0)
    def _fill():
        r1 = lax.broadcasted_iota(jnp.int32, (LANES, 2 * LANES), 0)
        c1 = lax.broadcasted_iota(jnp.int32, (LANES, 2 * LANES), 1)
        dup_kv = jnp.where((c1 % DH_B == r1 % DH_B) & (c1 // LANES == r1 // DH_B), 1.0, 0.0).astype(BF16)
        r2 = lax.broadcasted_iota(jnp.int32, (D_IDX, LANES), 0)
        c2 = lax.broadcasted_iota(jnp.int32, (D_IDX, LANES), 1)
        dup_ki = jnp.where(c2 % D_IDX == r2, 1.0, 0.0).astype(BF16)

        def fill(src_ref, dst, n):
            for r0 in range(0, n, ROW_CHUNK):
                rc = min(ROW_CHUNK, n - r0)
                kv2 = _dot(src_ref[r0:r0 + rc, :LANES].astype(BF16), dup_kv)
                kk_scr[dst + r0:dst + r0 + rc, :] = kv2[:, :LANES].astype(BF16)
                vv_scr[dst + r0:dst + r0 + rc, :] = kv2[:, LANES:].astype(BF16)
                ki_scr[dst + r0:dst + r0 + rc, :] = _dot(src_ref[r0:r0 + rc, LANES:].astype(BF16), dup_ki).astype(BF16)
        if p_len:
            fill(past_ref, 0, p_len)
        fill(new_ref, p_len, t_len)

    d0 = pl.multiple_of(p_len + qi * tq, tq)
    n_full = (p_len + qi * tq) // tk
    lane = lax.broadcasted_iota(jnp.int32, (tq, LANES), 1)
    low = lane < DH_B
    sel_lo = jnp.where(low, 1.0, 0.0).astype(BF16)
    sel_hi = jnp.where(low, 0.0, 1.0).astype(BF16)
    dmask = (lax.broadcasted_iota(jnp.int32, (tq, tq), 0) // CHUNK) <= (lax.broadcasted_iota(jnp.int32, (tq, tq), 1) // CHUNK)

    w_t = jnp.transpose(wi_ref[...])
    q_idx = []
    for pr in range(H_IDX // 2):
        blk = qi_ref[:, LANES * pr:LANES * (pr + 1)]
        q_idx += [blk * sel_lo, blk * sel_hi]

    def idx_score(ki2):
        sc = None
        for h in range(H_IDX):
            term = jnp.maximum(_dot_nt(ki2, q_idx[h]), 0.0) * w_t[h:h + 1, :]
            sc = term if sc is None else sc + term
        return sc

    def score_body(kb, _):
        sc_scr[kb] = idx_score(ki_scr[pl.ds(pl.multiple_of(kb * tk, tk), tk), :])
        return 0

    lax.fori_loop(0, n_full, score_body, 0)
    scd_scr[...] = jnp.where(dmask, idx_score(ki_scr[pl.ds(d0, tq), :]), -jnp.inf)

    def fold(fn, comb, reduce, init):
        def part(x):
            return reduce(x.reshape(x.shape[0] // sub, sub, tq), axis=0)
        acc = lax.fori_loop(0, n_full, lambda kb, a: comb(a, part(fn(sc_scr[kb], kb * tk))),
                            jnp.full((sub, tq), init, F32))
        acc = comb(acc, part(fn(scd_scr[...], d0)))
        return reduce(acc, axis=0, keepdims=True)

    def count_ge(thr):
        return fold(lambda s, _: jnp.where(s >= thr, 1.0, 0.0), jnp.add, jnp.sum, 0.0)

    def count_gt(thr):
        return fold(lambda s, _: jnp.where(s > thr, 1.0, 0.0), jnp.add, jnp.sum, 0.0)

    def min_ge(thr):
        return fold(lambda s, _: jnp.where(s >= thr, s, jnp.inf), jnp.minimum, jnp.min, jnp.inf)

    row_max = fold(lambda s, _: s, jnp.maximum, jnp.max, -jnp.inf)
    row_min = fold(lambda s, _: jnp.where(s > -jnp.inf, s, jnp.inf), jnp.minimum, jnp.min, jnp.inf)

    def bisect(_, lh):
        lo, hi = lh
        mid = lo + 0.5 * (hi - lo)
        ge = count_ge(mid) >= k_sel
        return jnp.where(ge, mid, lo), jnp.where(ge, hi, mid)

    def settle(lo):
        thr = min_ge(lo)
        n_gt = count_gt(thr)
        return thr, n_gt, jnp.max(jnp.where(n_gt >= k_sel, 1, 0))

    hi0 = row_max + jnp.abs(row_max) * 1e-6 + 1e-30
    lo, hi = lax.fori_loop(0, 20, bisect, (row_min, hi0))
    thr, n_gt, bad = settle(lo)

    def more(state):
        lo, hi, _, _, _, it = state
        lo, hi = lax.fori_loop(0, 4, bisect, (lo, hi))
        thr, n_gt, bad = settle(lo)
        return lo, hi, thr, n_gt, bad, it + 1

    _, _, thr, n_gt, _, _ = lax.while_loop(lambda st: (st[4] > 0) & (st[5] < 64), more,
                                           (lo, hi, thr, n_gt, bad, 0))

    n_ge = count_ge(thr)
    need = k_sel - n_gt
    excess = n_ge > k_sel

    def key_index(s, base):
        return base + lax.broadcasted_iota(jnp.int32, s.shape, 0)

    def cut_search(_):
        def count_tied_upto(j):
            return fold(lambda s, base: jnp.where((s == thr) & (key_index(s, base) <= j), 1.0, 0.0),
                        jnp.add, jnp.sum, 0.0)

        def step(_, jj):
            jlo, jhi = jj
            mid = (jlo + jhi) >> 1
            ok = count_tied_upto(mid) >= need
            return jnp.where(ok, jlo, mid), jnp.where(ok, mid, jhi)

        steps = int(math.ceil(math.log2(l_len))) + 1
        _, jhi = lax.fori_loop(0, steps, step,
                               (jnp.full((1, tq), -1, jnp.int32), jnp.full((1, tq), l_len - 1, jnp.int32)))
        return jnp.where(excess, jhi, l_len)

    cut = lax.cond(jnp.max(jnp.where(excess, 1, 0)) > 0, cut_search,
                   lambda _: jnp.full((1, tq), l_len, jnp.int32), 0)

    def bias_of(s, base):
        return jnp.where((s > thr) | ((s == thr) & (key_index(s, base) <= cut)), 0.0, NEG_INF)

    qs = []
    for pair in range(H_B // 2):
        blk = q_ref[:, LANES * pair:LANES * (pair + 1)]
        qs += [blk * sel_lo, blk * sel_hi]

    def attend(r0, n, bias_t, carries):
        bias = jnp.transpose(bias_t)
        kk, vv = kk_scr[pl.ds(r0, n), :], vv_scr[pl.ds(r0, n), :]
        return tuple(_softmax_step(_dot_nt(qs[h], kk) + bias, vv, carries[h]) for h in range(H_B))

    carries = lax.fori_loop(
        0, n_full, lambda kb, c: attend(pl.multiple_of(kb * tk, tk), tk, bias_of(sc_scr[kb], kb * tk), c),
        tuple(_softmax_init(tq) for _ in range(H_B)))
    carries = attend(d0, tq, bias_of(scd_scr[...], d0), carries)
    outs = [acc / l for _, l, acc in carries]
    for pair in range(H_B // 2):
        o_ref[:, LANES * pair:LANES * (pair + 1)] = jnp.where(low, outs[2 * pair], outs[2 * pair + 1]).astype(BF16)


def _dsa(q, qidx, wi, rows, past, layer, tq, tk):
    b, t, _ = q.shape
    p_len = 0 if past is None else past.shape[2]
    l_len = p_len + t
    n_sel = min(TOPK_MAX, l_len // 4)
    n_blocks = max((l_len - tq) // tk, 1)
    args = [q, qidx, wi, rows] + ([past] if past is not None else [])
    transposed = tq % LANES == 0
    body = _dsat_kernel if transposed else _dsa_kernel
    return pl.pallas_call(
        functools.partial(body, p_len=p_len, t_len=t, tq=tq, tk=tk, n_sel=n_sel),
        grid=(b, t // tq),
        in_specs=_attn_specs(tq, [256, 512, LANES], rows, past, layer),
        out_specs=_out_spec(tq),
        out_shape=jax.ShapeDtypeStruct((b, t, 256), BF16),
        scratch_shapes=[pltpu.VMEM((l_len, LANES), BF16)] * 3
                       + [pltpu.VMEM((n_blocks, tk, tq) if transposed else (n_blocks, tq, tk), F32),
                          pltpu.VMEM((tq, tq), F32)],
        compiler_params=_params("parallel", "arbitrary"),
        name="dsa",
    )(*args)


def _gather_cols(w, src, scale):
    src = [int(s) for s in src]
    scale = [float(c) if s >= 0 else 0.0 for s, c in zip(src, scale)]
    pieces, i = [], 0
    while i < len(src):
        j = i + 1
        while j < len(src) and scale[j] == scale[i] and (src[j] == src[j - 1] + 1 if src[i] >= 0 else src[j] < 0):
            j += 1
        if src[i] < 0:
            pieces.append(jnp.zeros((w.shape[0], j - i), w.dtype))
        else:
            run = w[:, src[i]:src[i] + j - i]
            pieces.append(run if scale[i] == 1.0 else run * scale[i])
        i = j
    return jnp.concatenate(pieces, axis=1)


class _Cols:
    def __init__(self):
        self.src, self.scale, self.rope = [], [], []

    def plain(self, start, n, scale=1.0):
        for i in range(n):
            self._add(start + i, scale, None)

    def pad(self, n):
        for _ in range(n):
            self._add(-1, 0.0, None)

    def roped(self, start, d, scale=1.0):
        half = d // 2
        for i in range(d):
            self._add(start + i, scale, (d, i % half, -1.0 if i < half else 1.0))

    def _add(self, src, scale, rope):
        self.src.append(src)
        self.scale.append(scale)
        self.rope.append(rope)

    def weights(self, w):
        return _gather_cols(w, self.src, self.scale).astype(BF16)

    def tables(self, pos):
        cs = {}
        for d in sorted({r[0] for r in self.rope if r is not None}):
            freqs = ROPE_THETA ** (-jnp.arange(d // 2, dtype=F32) * 2.0 / d)
            ang = pos.astype(F32)[:, None] * freqs[None, :]
            cs[d] = (jnp.cos(ang), jnp.sin(ang))
        cos, sin, i, n = [], [], 0, pos.shape[0]
        while i < len(self.rope):
            r, j = self.rope[i], i + 1
            if r is None:
                while j < len(self.rope) and self.rope[j] is None:
                    j += 1
                cos.append(jnp.ones((n, j - i), F32))
                sin.append(jnp.zeros((n, j - i), F32))
            else:
                while j < len(self.rope) and self.rope[j] == (r[0], r[1] + j - i, r[2]):
                    j += 1
                cos.append(cs[r[0]][0][:, r[1]:r[1] + j - i])
                sin.append(r[2] * cs[r[0]][1][:, r[1]:r[1] + j - i])
            i = j
        return jnp.concatenate(cos, axis=1), jnp.concatenate(sin, axis=1)


def _plans():
    p = _Cols()
    p.plain(O_CQ, Q_LORA)
    p.plain(O_CKV, KV_LORA)
    for h in range(H_C):
        p.plain(O_KC + DH_C * h, DH_C)
        p.plain(O_VC + DH_C * h, DH_C)
    for h in range(H_C):
        p.plain(O_QC + DH_C * h, DH_C, DH_C ** -0.5)
        p.pad(LANES - DH_C)
    p.plain(O_WIB, H_IDX)
    p.pad(LANES - H_IDX)
    assert len(p.src) == P_END

    r = _Cols()
    r.roped(O_KB, DH_B)
    r.plain(O_VB, DH_B)
    r.roped(O_KIB, D_IDX)
    r.pad(R_DIFF - (2 * DH_B + D_IDX))
    for h in range(H_D):
        r.roped(O_KD + 2 * DQK_D * h, DQK_D)
        r.roped(O_KD + 2 * DQK_D * h + DQK_D, DQK_D)
        r.plain(O_VD + DV_D * h, DV_D)
    r.roped(O_KR, ROPE_A)
    r.pad(LANES - ROPE_A)
    for h in range(H_B):
        r.roped(O_QB + DH_B * h, DH_B, DH_B ** -0.5)
    for h in range(H_IDX):
        r.roped(O_QIB + D_IDX * h, D_IDX)
    for h in range(H_D):
        r.roped(O_QD + 2 * DQK_D * h, DQK_D)
        r.roped(O_QD + 2 * DQK_D * h + DQK_D, DQK_D)
        r.pad(LANES - 2 * DQK_D)
    assert len(r.src) == R_END

    q = _Cols()
    for h in range(H_A):
        q.plain((NOPE_A + ROPE_A) * h, NOPE_A)
        q.roped((NOPE_A + ROPE_A) * h + NOPE_A, ROPE_A)
        q.pad(LANES - NOPE_A - ROPE_A)
    return p, r, q


def _layer_weights(l, prm, plans):
    p, r, q = plans
    w_in = prm["w_in"][l]
    wkvb = prm["mla_w_kvb"][l]
    k_src, v_src = [], []
    for h in range(H_A):
        k_src += list(range((NOPE_A + V_A) * h, (NOPE_A + V_A) * h + NOPE_A)) + [-1] * (LANES - NOPE_A)
        v_src += list(range((NOPE_A + V_A) * h + NOPE_A, (NOPE_A + V_A) * (h + 1)))
    place = np.zeros((ROPE_A, H_A * LANES), np.float32)
    for h in range(H_A):
        place[np.arange(ROPE_A), LANES * h + NOPE_A + np.arange(ROPE_A)] = 1.0
    dn = jnp.concatenate([jnp.zeros((2 * DQK_D,), F32), prm["diff_norm"][l]])[None, :]

    def ff(tag):
        return (prm[f"norm_{tag}"][l][None, :], prm[f"w_{tag}_gate"][l].astype(BF16),
                prm[f"w_{tag}_up"][l].astype(BF16), prm[f"w_{tag}_down"][l].astype(BF16))

    return {
        "ff1": ff("ff1"), "ff2": ff("ff2"),
        "nm": prm["norm_mix"][l][None, :],
        "wp": p.weights(w_in), "wr": r.weights(w_in),
        "qn": prm["mla_q_norm"][l][None, :], "wqb": q.weights(prm["mla_w_qb"][l]),
        "kvn": prm["mla_kv_norm"][l][None, :],
        "wk": _gather_cols(wkvb, k_src, np.ones(len(k_src))).astype(BF16),
        "pk": jnp.asarray(place).astype(BF16),
        "wv": _gather_cols(wkvb, v_src, np.ones(len(v_src))).astype(BF16),
        "lam": prm["diff_lambda"][l], "dn": dn,
        "wo": prm["w_out"][l].astype(BF16),
    }


def _trunk(x, caches, pos, layers, plans, fnorm, tq, tk, proj_fold):
    b, t, _ = x.shape
    depth = len(layers)
    tm = min(512, b * t)
    _, r_plan, q_plan = plans
    pb, pt = (b // proj_fold, t * proj_fold)
    pos_p = jnp.tile(pos, proj_fold)
    tabs = r_plan.tables(pos_p) + q_plan.tables(pos_p)
    tq_p = min(256, pt)
    xf = x.reshape(b * t, D_MODEL)
    stacked = [jnp.zeros((depth, pb, pt * r, w), F32)
               for r, w in ((1, KV_LORA + ROPE_A), (1, 2 * DH_B + D_IDX), (H_C, 2 * DH_C), (H_D, 2 * DQK_D + DV_D))]
    past = [None] * 4 if caches is None else caches
    for l, lw in enumerate(layers):
        lam_init = 0.8 - 0.6 * math.exp(-0.3 * l)
        xf = _ffn(xf, None, None, *lw["ff1"], None, tm)
        outs = _proj(xf.reshape(pb, pt, D_MODEL), lw, tabs, stacked, l, tq_p)
        stacked = list(outs[:4])
        r_sb, r_diff, q_mla, q_dsa, q_idx, w_idx, q_sb, q_diff = [o.reshape(b, t, o.shape[-1]) for o in outs[4:]]
        r_mla, r_dsa = (a.reshape(depth, b, t, a.shape[-1]) for a in stacked[:2])
        mix = (
            _mla(q_mla, r_mla, past[0], l, lw, tq, tk),
            _dsa(q_dsa, q_idx, w_idx, r_dsa, past[1], l, tq, tk),
            _sb(q_sb, r_sb, past[2], l, tq, tk),
            _diff(q_diff, r_diff, past[3], l, lw, lam_init, tq, tk),
        )
        mix = [m.reshape(b * t, 256) for m in mix]
        xf = _ffn(xf, mix, lw["wo"], *lw["ff2"], fnorm if l == depth - 1 else None, tm)
    y = xf.reshape(b, t, D_MODEL)
    return (y, stacked[0].reshape(depth, b, t, -1), stacked[1].reshape(depth, b, t, -1),
            stacked[2].reshape(depth, b, t, H_C, 2 * DH_C),
            stacked[3].reshape(depth, b, t, H_D, 2 * DQK_D + DV_D))


def kernel(x_prompt, x_sample, cache_mla, cache_dsa, cache_sb, cache_diff, norm_ff1, w_ff1_gate, w_ff1_up, w_ff1_down, norm_mix, w_in, mla_q_norm, mla_w_qb, mla_kv_norm, mla_w_kvb, diff_lambda, diff_norm, w_out, norm_ff2, w_ff2_gate, w_ff2_up, w_ff2_down, final_norm):
    prm = dict(norm_ff1=norm_ff1, w_ff1_gate=w_ff1_gate, w_ff1_up=w_ff1_up, w_ff1_down=w_ff1_down,
               norm_mix=norm_mix, w_in=w_in, mla_q_norm=mla_q_norm, mla_w_qb=mla_w_qb,
               mla_kv_norm=mla_kv_norm, mla_w_kvb=mla_w_kvb, diff_lambda=diff_lambda, diff_norm=diff_norm,
               w_out=w_out, norm_ff2=norm_ff2, w_ff2_gate=w_ff2_gate, w_ff2_up=w_ff2_up, w_ff2_down=w_ff2_down)
    depth = w_in.shape[0]
    plans = _plans()
    layers = [_layer_weights(l, prm, plans) for l in range(depth)]
    fnorm = final_norm[None, :]

    t_p = x_prompt.shape[1]
    assert t_p % CHUNK == 0
    tq_p = min(512, t_p)
    out_p = _trunk(x_prompt, None, jnp.arange(t_p, dtype=jnp.int32), layers, plans, fnorm, tq_p, tq_p, 1)


    b_s, t_s, _ = x_sample.shape
    past = cache_mla.shape[2]
    assert past % CHUNK == 0 and t_s <= CHUNK, "new sample frames must sit in one chunk after whole cached chunks"
    tk_s = math.gcd(past, 512)
    fold_heads = lambda c: c.reshape(c.shape[:2] + (c.shape[2] * c.shape[3], c.shape[4]))
    caches = (cache_mla, cache_dsa, fold_heads(cache_sb), fold_heads(cache_diff))
    fold = math.gcd(b_s, max(1, 256 // t_s))
    out_s = _trunk(x_sample, caches, past + jnp.arange(t_s, dtype=jnp.int32), layers, plans, fnorm, t_s, tk_s, fold)

    return (out_p[0], out_s[0]) + tuple(out_p[1:]) + tuple(out_s[1:])
```

```python
import functools
import math

import numpy as np
import jax
import jax.numpy as jnp
from jax import lax
from jax.experimental import pallas as pl
from jax.experimental.pallas import tpu as pltpu

F32 = jnp.float32
BF16 = jnp.bfloat16

D_MODEL = 1024
CHUNK = 64
ROPE_THETA = 10000.0
EPS = 1e-6
NEG_INF = -1e30
HEAD_DIM = D_MODEL // 16
D_FF = 11 * D_MODEL // 4
H_A = 4
Q_LORA = 4 * HEAD_DIM
KV_LORA = 2 * HEAD_DIM
NOPE_A = HEAD_DIM
ROPE_A = HEAD_DIM // 2
V_A = HEAD_DIM
H_B = 4
DH_B = HEAD_DIM
H_IDX = 8
D_IDX = HEAD_DIM
TOPK_MAX = 256
H_C = 4
DH_C = HEAD_DIM
H_D = 4
DQK_D = HEAD_DIM // 2
DV_D = HEAD_DIM
COL_SIZES = (Q_LORA, KV_LORA, ROPE_A,
             H_B * DH_B, DH_B, DH_B, H_IDX * D_IDX, D_IDX, H_IDX,
             H_C * DH_C, H_C * DH_C, H_C * DH_C,
             H_D * 2 * DQK_D, H_D * 2 * DQK_D, H_D * DV_D)
(O_CQ, O_CKV, O_KR, O_QB, O_KB, O_VB, O_QIB, O_KIB, O_WIB,
 O_QC, O_KC, O_VC, O_QD, O_KD, O_VD) = np.concatenate([[0], np.cumsum(COL_SIZES)[:-1]]).tolist()

LANES = 128
VMEM_LIMIT = 56 * 1024 * 1024
MXU_TILE = 256
FF_CHUNK = MXU_TILE
ROW_CHUNK = 512

P_CQ, P_CKV, P_SB, P_QSB, P_WI, P_END = 0, 256, 384, 896, 1408, 1536
R_DSA, R_DIFF, R_KR, R_QDSA, R_QI, R_QDIFF, R_END = 0, 256, 768, 896, 1152, 1664, 2176


def _dot(a, b):
    return jnp.dot(a, b, preferred_element_type=F32)


def _dot_nt(a, b):
    return lax.dot_general(a, b, (((1,), (1,)), ((), ())), preferred_element_type=F32)


def _rms(x, g):
    return x * lax.rsqrt(jnp.mean(x * x, axis=-1, keepdims=True) + EPS) * g


def _const_spec(shape):
    n = len(shape)
    return pl.BlockSpec(shape, lambda *_: (0,) * n, pipeline_mode=pl.Buffered(1))


def _params(*sem):
    return pltpu.CompilerParams(dimension_semantics=sem, vmem_limit_bytes=VMEM_LIMIT)


def _ffn_kernel(*refs, has_mix, has_final):
    it = iter(refs)
    x_ref = next(it)
    if has_mix:
        o_refs = [next(it) for _ in range(4)]
        wo_ref = next(it)
    g_ref, wg_ref, wu_ref, wd_ref = next(it), next(it), next(it), next(it)
    fn_ref = next(it) if has_final else None
    out_ref = next(it)

    x = x_ref[...]
    if has_mix:
        for i, o_ref in enumerate(o_refs):
            x = x + _dot(o_ref[...], wo_ref[256 * i:256 * (i + 1), :])
    xn = _rms(x, g_ref[...]).astype(BF16)
    acc = jnp.zeros_like(x)
    for c0 in range(0, D_FF, FF_CHUNK):
        gate = _dot(xn, wg_ref[:, c0:c0 + FF_CHUNK])
        up = _dot(xn, wu_ref[:, c0:c0 + FF_CHUNK])
        hid = (gate * jax.nn.sigmoid(gate) * up).astype(BF16)
        acc = acc + _dot(hid, wd_ref[c0:c0 + FF_CHUNK, :])
    y = x + 0.5 * acc
    if has_final:
        y = _rms(y, fn_ref[...])
    out_ref[...] = y


def _ffn(x, mix, wo, g, wg, wu, wd, fnorm, tm):
    n = x.shape[0]
    has_mix, has_final = mix is not None, fnorm is not None
    row = lambda w: pl.BlockSpec((tm, w), lambda i: (i, 0))
    args, specs = [x], [row(D_MODEL)]
    if has_mix:
        args += list(mix) + [wo]
        specs += [row(256)] * 4 + [_const_spec(wo.shape)]
    args += [g, wg, wu, wd]
    specs += [_const_spec(g.shape), _const_spec(wg.shape), _const_spec(wu.shape), _const_spec(wd.shape)]
    if has_final:
        args.append(fnorm)
        specs.append(_const_spec(fnorm.shape))
    return pl.pallas_call(
        functools.partial(_ffn_kernel, has_mix=has_mix, has_final=has_final),
        grid=(n // tm,),
        in_specs=specs,
        out_specs=row(D_MODEL),
        out_shape=jax.ShapeDtypeStruct((n, D_MODEL), F32),
        compiler_params=_params("parallel"),
        name="ffn",
    )(*args)


def _rope(h, cos, sin_signed, d):
    w = h.shape[1]
    first = lax.broadcasted_iota(jnp.int32, h.shape, 1) % d < d // 2
    partner = jnp.where(first, pltpu.roll(h, w - d // 2, axis=1), pltpu.roll(h, d // 2, axis=1))
    return h * cos + partner * sin_signed


def _proj_kernel(x_ref, g_ref, wp_ref, wr_ref, cr_ref, sr_ref,
                 qn_ref, wqb_ref, cq_ref, sq_ref, kvn_ref,
                 _amla, _adsa, _asb, _adiff,
                 rmla_ref, rdsa_ref, rsbh_ref, rdiffh_ref, rsb_ref, rdiff_ref,
                 qmla_ref, qdsa_ref, qi_ref, wi_ref, qsb_ref, qdiff_ref):
    xn = _rms(x_ref[...], g_ref[...]).astype(BF16)
    tq = x_ref.shape[0]

    def plain(a, b):
        return _dot(xn, wp_ref[:, a:b])

    def roped(a, b, d):
        return _rope(_dot(xn, wr_ref[:, a:b]), cr_ref[:, a:b], sr_ref[:, a:b], d)

    cqn = _rms(plain(P_CQ, P_CKV), qn_ref[...]).astype(BF16)
    qmla_ref[...] = _rope(_dot(cqn, wqb_ref[...]), cq_ref[...], sq_ref[...], ROPE_A).astype(BF16)
    rmla_ref[:, :KV_LORA] = _rms(plain(P_CKV, P_SB), kvn_ref[...])
    rmla_ref[:, KV_LORA:] = roped(R_KR, R_QDSA, ROPE_A)[:, :ROPE_A]
    rdsa_ref[...] = roped(R_DSA, R_DIFF, DH_B)[:, :2 * DH_B + D_IDX]
    qdsa_ref[...] = roped(R_QDSA, R_QI, DH_B).astype(BF16)
    qi_ref[...] = roped(R_QI, R_QDIFF, D_IDX).astype(BF16)
    wi_ref[...] = plain(P_WI, P_END) * (H_IDX ** -0.5)
    rsb = plain(P_SB, P_QSB)
    rsb_ref[...] = rsb
    for h in range(H_C):
        rsbh_ref[pl.ds(h, tq, stride=H_C), :] = rsb[:, LANES * h:LANES * (h + 1)]
    qsb_ref[...] = plain(P_QSB, P_WI).astype(BF16)
    rdiff = roped(R_DIFF, R_KR, DQK_D)
    rdiff_ref[...] = rdiff
    for h in range(H_D):
        rdiffh_ref[pl.ds(h, tq, stride=H_D), :] = rdiff[:, LANES * h:LANES * (h + 1)]
    qdiff_ref[...] = roped(R_QDIFF, R_END, DQK_D).astype(BF16)


def _proj(x, lw, tabs, stacked, layer, tq):
    b, t, _ = x.shape
    cr, sr, cq, sq = tabs
    tok = lambda w: pl.BlockSpec((None, tq, w), lambda bi, i: (bi, i, 0))
    lay = lambda r, w: pl.BlockSpec((None, None, r, w), lambda bi, i: (layer, bi, i, 0))
    tab = lambda w: pl.BlockSpec((tq, w), lambda bi, i: (i, 0))
    consts = [lw["nm"], lw["wp"], lw["wr"]]
    consts2 = [lw["qn"], lw["wqb"]]
    in_specs = ([tok(D_MODEL)] + [_const_spec(a.shape) for a in consts] + [tab(R_END), tab(R_END)]
                + [_const_spec(a.shape) for a in consts2] + [tab(512), tab(512), _const_spec(lw["kvn"].shape)]
                + [pl.BlockSpec(memory_space=pl.ANY)] * 4)
    widths = [(512, F32), (512, F32),
              (512, BF16), (256, BF16), (512, BF16), (LANES, F32), (512, BF16), (512, BF16)]
    n_in = len(in_specs)
    return pl.pallas_call(
        _proj_kernel,
        grid=(b, t // tq),
        in_specs=in_specs,
        out_specs=[lay(tq, 160), lay(tq, 192), lay(tq * H_C, LANES), lay(tq * H_D, LANES)]
                  + [tok(w) for w, _ in widths],
        out_shape=[jax.ShapeDtypeStruct(a.shape, a.dtype) for a in stacked]
                  + [jax.ShapeDtypeStruct((b, t, w), dt) for w, dt in widths],
        input_output_aliases={n_in - 4 + k: k for k in range(4)},
        compiler_params=_params("parallel", "parallel"),
        name="proj",
    )(x, *consts, cr, sr, *consts2, cq, sq, lw["kvn"], *stacked)


def _chunk_mask(tq):
    ri = lax.broadcasted_iota(jnp.int32, (tq, tq), 0)
    ci = lax.broadcasted_iota(jnp.int32, (tq, tq), 1)
    return (ci // CHUNK) <= (ri // CHUNK)


LOG2E = math.log2(math.e)


def _softmax_step(s, v, carry, scale=1.0):
    m, l, acc = carry
    c = scale * LOG2E
    m_new = jnp.maximum(m, jnp.max(s, axis=-1, keepdims=True))
    alpha = jnp.exp2((m - m_new) * c)
    p = jnp.exp2((s - m_new) * c)
    l = alpha * l + jnp.sum(p, axis=-1, keepdims=True)
    acc = alpha * acc + _dot(p.astype(BF16), v)
    return m_new, l, acc


def _softmax_init(tq):
    return (jnp.full((tq, 1), NEG_INF, F32), jnp.zeros((tq, 1), F32), jnp.zeros((tq, LANES), F32))


def _lane_lt(tq, n):
    return lax.broadcasted_iota(jnp.int32, (tq, LANES), 1) < n


def _attn_specs(tq, q_widths, rows, past, layer):
    def whole(a):
        if a.ndim == 3:
            return pl.BlockSpec((None,) + a.shape[1:], lambda bi, i: (bi, 0, 0))
        return pl.BlockSpec((None, None) + a.shape[2:], lambda bi, i: (layer, bi, 0, 0))
    specs = [pl.BlockSpec((None, tq, w), lambda bi, i: (bi, i, 0)) for w in q_widths]
    return specs + [whole(rows)] + ([whole(past)] if past is not None else [])


def _out_spec(tq):
    return pl.BlockSpec((None, tq, 256), lambda bi, i: (bi, i, 0))


def _mla_kernel(*refs, p_len, t_len, tq, tk):
    if p_len:
        q_ref, new_ref, past_ref, wk_ref, pk_ref, wv_ref, o_ref, k_scr, v_scr = refs
    else:
        q_ref, new_ref, wk_ref, pk_ref, wv_ref, o_ref, k_scr, v_scr = refs
        past_ref = None
    qi = pl.program_id(1)
    scale = (NOPE_A + ROPE_A) ** -0.5

    @pl.when(qi == 0)
    def _fill():
        def fill(src_ref, dst, n):
            for r0 in range(0, n, ROW_CHUNK):
                rc = min(ROW_CHUNK, n - r0)
                lat = src_ref[r0:r0 + rc, :KV_LORA].astype(BF16)
                rope = src_ref[r0:r0 + rc, KV_LORA:].astype(BF16)
                k_scr[dst + r0:dst + r0 + rc, :] = (_dot(lat, wk_ref[...]) + _dot(rope, pk_ref[...])).astype(BF16)
                v_scr[dst + r0:dst + r0 + rc, :] = _dot(lat, wv_ref[...]).astype(BF16)
        if p_len:
            fill(past_ref, 0, p_len)
        fill(new_ref, p_len, t_len)

    d0 = pl.multiple_of(p_len + qi * tq, tq)
    n_full = (p_len + qi * tq) // tk
    dmask = _chunk_mask(tq)
    low = _lane_lt(tq, V_A)
    qs = [q_ref[:, LANES * h:LANES * (h + 1)] for h in range(H_A)]

    def step(r0, n, mask, carries):
        out = []
        for h in range(H_A):
            s = _dot_nt(qs[h], k_scr[pl.ds(r0, n), LANES * h:LANES * (h + 1)])
            if mask is not None:
                s = jnp.where(mask, s, NEG_INF)
            out.append(_softmax_step(s, v_scr[pl.ds(r0, n), LANES * (h // 2):LANES * (h // 2 + 1)], carries[h],
                                     scale))
        return tuple(out)

    carries = lax.fori_loop(0, n_full, lambda kb, c: step(pl.multiple_of(kb * tk, tk), tk, None, c),
                            tuple(_softmax_init(tq) for _ in range(H_A)))
    carries = step(d0, tq, dmask, carries)
    outs = [acc / l for _, l, acc in carries]
    for pair in range(H_A // 2):
        o_ref[:, LANES * pair:LANES * (pair + 1)] = jnp.where(low, outs[2 * pair], outs[2 * pair + 1]).astype(BF16)


def _mla(q, rows, past, layer, lw, tq, tk):
    b, t, _ = q.shape
    p_len = 0 if past is None else past.shape[2]
    consts = [lw["wk"], lw["pk"], lw["wv"]]
    args = [q, rows] + ([past] if past is not None else []) + consts
    return pl.pallas_call(
        functools.partial(_mla_kernel, p_len=p_len, t_len=t, tq=tq, tk=tk),
        grid=(b, t // tq),
        in_specs=_attn_specs(tq, [512], rows, past, layer) + [_const_spec(a.shape) for a in consts],
        out_specs=_out_spec(tq),
        out_shape=jax.ShapeDtypeStruct((b, t, 256), BF16),
        scratch_shapes=[pltpu.VMEM((p_len + t, 512), BF16), pltpu.VMEM((p_len + t, 256), BF16)],
        compiler_params=_params("parallel", "arbitrary"),
        name="mla",
    )(*args)


def _diff_kernel(*refs, p_len, t_len, tq, tk, lam_init):
    if p_len:
        q_ref, new_ref, past_ref, lam_ref, dn_ref, o_ref = refs
    else:
        q_ref, new_ref, lam_ref, dn_ref, o_ref = refs
        past_ref = None
    qi = pl.program_id(1)
    scale = DQK_D ** -0.5
    lam = lam_ref[...]
    lam_full = (jnp.exp(jnp.sum(lam[0:1] * lam[1:2], axis=-1, keepdims=True))
                - jnp.exp(jnp.sum(lam[2:3] * lam[3:4], axis=-1, keepdims=True)) + lam_init)
    dmask = _chunk_mask(tq)
    lane = lax.broadcasted_iota(jnp.int32, (tq, LANES), 1)
    sel1 = jnp.where(lane < DQK_D, 1.0, 0.0).astype(BF16)
    sel2 = jnp.where((lane >= DQK_D) & (lane < 2 * DQK_D), 1.0, 0.0).astype(BF16)
    is_v = lane >= 2 * DQK_D
    n_past = p_len // tk
    n_new = (qi * tq) // tk

    q12 = []
    for h in range(H_D):
        qh = q_ref[:, LANES * h:LANES * (h + 1)]
        q12 += [qh * sel1, qh * sel2]

    def step(load, mask, carries):
        out = []
        for h in range(H_D):
            kv = load(h)
            for m in range(2):
                s = _dot_nt(q12[2 * h + m], kv)
                if mask is not None:
                    s = jnp.where(mask, s, NEG_INF)
                out.append(_softmax_step(s, kv, carries[2 * h + m], scale))
        return tuple(out)

    def new_rows(r0, n):
        return lambda h: new_ref[pl.ds(r0, n), LANES * h:LANES * (h + 1)].astype(BF16)

    def past_rows(kb):
        return lambda h: past_ref[pl.ds(H_D * kb * tk + h, tk, stride=H_D), :].astype(BF16)

    carries = tuple(_softmax_init(tq) for _ in range(2 * H_D))
    for kb in range(n_past):
        carries = step(past_rows(kb), None, carries)
    carries = lax.fori_loop(0, n_new, lambda kb, c: step(new_rows(pl.multiple_of(kb * tk, tk), tk), None, c),
                            carries)
    carries = step(new_rows(pl.multiple_of(qi * tq, tq), tq), dmask, carries)
    outs = []
    for h in range(H_D):
        (_, l1, a1), (_, l2, a2) = carries[2 * h], carries[2 * h + 1]
        o = a1 / l1 - lam_full * (a2 / l2)
        ms = jnp.sum(jnp.where(is_v, o * o, 0.0), axis=-1, keepdims=True) * (1.0 / DV_D)
        outs.append(o * lax.rsqrt(ms + EPS) * dn_ref[...] * (1.0 - lam_init))
    for pair in range(H_D // 2):
        o_ref[:, LANES * pair:LANES * (pair + 1)] = jnp.where(
            is_v, outs[2 * pair + 1], pltpu.roll(outs[2 * pair], 2 * DQK_D, axis=1)).astype(BF16)


def _diff(q, rows, past, layer, lw, lam_init, tq, tk):
    b, t, _ = q.shape
    p_len = 0 if past is None else past.shape[2] // H_D
    consts = [lw["lam"], lw["dn"]]
    args = [q, rows] + ([past] if past is not None else []) + consts
    return pl.pallas_call(
        functools.partial(_diff_kernel, p_len=p_len, t_len=t, tq=tq, tk=tk, lam_init=lam_init),
        grid=(b, t // tq),
        in_specs=_attn_specs(tq, [512], rows, past, layer) + [_const_spec(a.shape) for a in consts],
        out_specs=_out_spec(tq),
        out_shape=jax.ShapeDtypeStruct((b, t, 256), BF16),
        compiler_params=_params("parallel", "parallel"),
        name="diff",
    )(*args)


def _sb_kernel(*refs, p_len, t_len, tq, tk):
    if p_len:
        q_ref, new_ref, past_ref, o_ref = refs
    else:
        q_ref, new_ref, o_ref = refs
        past_ref = None
    qi = pl.program_id(1)
    n_past = p_len // tk
    n_new = (qi * tq) // tk

    def later(n):
        n = min(n, MXU_TILE)
        ri = lax.broadcasted_iota(jnp.int32, (n, n), 0)
        ci = lax.broadcasted_iota(jnp.int32, (n, n), 1)
        return jnp.where(ri > ci, 1.0, 0.0).astype(BF16)

    def later_sums(x, tri):
        w = tri.shape[0]
        pieces, after = [], None
        for j in reversed(range(x.shape[1] // w)):
            xj = x[:, w * j:w * (j + 1)]
            hi = xj.astype(BF16)
            lo = (xj - hi.astype(F32)).astype(BF16)
            inside = _dot(hi, tri) + _dot(lo, tri)
            total = inside[:, :1] + xj[:, :1]
            if after is not None:
                inside, total = inside + after, total + after
            pieces.append(inside)
            after = total
        return (pieces[0] if len(pieces) == 1 else jnp.concatenate(pieces[::-1], axis=1)), after

    later_k = later(tk)
    later_q = later_k if min(tq, MXU_TILE) == min(tk, MXU_TILE) else later(tq)
    ri = lax.broadcasted_iota(jnp.int32, (tq, tq), 0)
    ci = lax.broadcasted_iota(jnp.int32, (tq, tq), 1)
    causal = ci < ri
    is_v = lax.broadcasted_iota(jnp.int32, (tq, LANES), 1) >= DH_C

    qs = [q_ref[:, LANES * h:LANES * (h + 1)] for h in range(H_C)]

    def step(load, mask, tri, carries):
        out = []
        for h in range(H_C):
            run, acc = carries[h]
            kv = load(h)
            z = _dot_nt(qs[h], kv)
            log_take = jnp.minimum(z, 0.0) - jnp.log(1.0 + jnp.exp2(jnp.abs(z) * -LOG2E))
            log_keep = log_take - z
            if mask is not None:
                log_keep = jnp.where(mask, log_keep, 0.0)
            later_sum, block_sum = later_sums(log_keep, tri)
            a = jnp.exp2((log_take + (later_sum + run)) * LOG2E)
            if mask is not None:
                a = jnp.where(mask, a, 0.0)
            acc = acc + _dot(a.astype(BF16), kv)
            run = run + block_sum
            out.append((run, acc))
        return tuple(out)

    def new_rows(r0, n):
        return lambda h: new_ref[pl.ds(r0, n), LANES * h:LANES * (h + 1)].astype(BF16)

    def past_rows(kb):
        return lambda h: past_ref[pl.ds(H_C * kb * tk + h, tk, stride=H_C), :].astype(BF16)

    carries = tuple((jnp.zeros((tq, 1), F32), jnp.zeros((tq, LANES), F32)) for _ in range(H_C))
    carries = step(new_rows(pl.multiple_of(qi * tq, tq), tq), causal, later_q, carries)
    carries = lax.fori_loop(
        0, n_new,
        lambda i, c: step(new_rows(pl.multiple_of((n_new - 1 - i) * tk, tk), tk), None, later_k, c), carries)
    for kb in reversed(range(n_past)):
        carries = step(past_rows(kb), None, later_k, carries)
    for pair in range(H_C // 2):
        o_ref[:, LANES * pair:LANES * (pair + 1)] = jnp.where(
            is_v, carries[2 * pair + 1][1], pltpu.roll(carries[2 * pair][1], DH_C, axis=1)).astype(BF16)


def _sb(q, rows, past, layer, tq, tk):
    b, t, _ = q.shape
    p_len = 0 if past is None else past.shape[2] // H_C
    args = [q, rows] + ([past] if past is not None else [])
    return pl.pallas_call(
        functools.partial(_sb_kernel, p_len=p_len, t_len=t, tq=tq, tk=tk),
        grid=(b, t // tq),
        in_specs=_attn_specs(tq, [512], rows, past, layer),
        out_specs=_out_spec(tq),
        out_shape=jax.ShapeDtypeStruct((b, t, 256), BF16),
        compiler_params=_params("parallel", "parallel"),
        name="sb",
    )(*args)


def _dsa_kernel(*refs, p_len, t_len, tq, tk, n_sel):
    if p_len:
        q_ref, qi_ref, wi_ref, new_ref, past_ref, o_ref, kk_scr, vv_scr, ki_scr, sc_scr, scd_scr = refs
    else:
        q_ref, qi_ref, wi_ref, new_ref, o_ref, kk_scr, vv_scr, ki_scr, sc_scr, scd_scr = refs
        past_ref = None
    qi = pl.program_id(1)
    l_len = p_len + t_len
    n_tiles = tk // LANES
    k_sel = float(n_sel)

    @pl.when(qi == 0)
    def _fill():
        r1 = lax.broadcasted_iota(jnp.int32, (LANES, 2 * LANES), 0)
        c1 = lax.broadcasted_iota(jnp.int32, (LANES, 2 * LANES), 1)
        dup_kv = jnp.where((c1 % DH_B == r1 % DH_B) & (c1 // LANES == r1 // DH_B), 1.0, 0.0).astype(BF16)
        r2 = lax.broadcasted_iota(jnp.int32, (D_IDX, LANES), 0)
        c2 = lax.broadcasted_iota(jnp.int32, (D_IDX, LANES), 1)
        dup_ki = jnp.where(c2 % D_IDX == r2, 1.0, 0.0).astype(BF16)

        def fill(src_ref, dst, n):
            for r0 in range(0, n, ROW_CHUNK):
                rc = min(ROW_CHUNK, n - r0)
                kv2 = _dot(src_ref[r0:r0 + rc, :LANES].astype(BF16), dup_kv)
                kk_scr[dst + r0:dst + r0 + rc, :] = kv2[:, :LANES].astype(BF16)
                vv_scr[dst + r0:dst + r0 + rc, :] = kv2[:, LANES:].astype(BF16)
                ki_scr[dst + r0:dst + r0 + rc, :] = _dot(src_ref[r0:r0 + rc, LANES:].astype(BF16), dup_ki).astype(BF16)
        if p_len:
            fill(past_ref, 0, p_len)
        fill(new_ref, p_len, t_len)

    d0 = pl.multiple_of(p_len + qi * tq, tq)
    n_full = (p_len + qi * tq) // tk
    lane = lax.broadcasted_iota(jnp.int32, (tq, LANES), 1)
    low = lane < DH_B
    sel_lo = jnp.where(low, 1.0, 0.0).astype(BF16)
    sel_hi = jnp.where(low, 0.0, 1.0).astype(BF16)
    dmask = _chunk_mask(tq)

    wi = wi_ref[...]
    wcol = [jnp.sum(jnp.where(lane == h, wi, 0.0), axis=-1, keepdims=True) for h in range(H_IDX)]
    q_idx = []
    for pr in range(H_IDX // 2):
        blk = qi_ref[:, LANES * pr:LANES * (pr + 1)]
        q_idx += [blk * sel_lo, blk * sel_hi]
    q_idx = jnp.concatenate(q_idx, axis=0)
    w_all = jnp.concatenate(wcol, axis=0)

    def idx_score(ki2):
        terms = jnp.maximum(_dot_nt(q_idx, ki2), 0.0) * w_all
        sc = terms[:tq]
        for h in range(1, H_IDX):
            sc = sc + terms[tq * h:tq * (h + 1)]
        return sc

    def score_body(kb, _):
        r0 = pl.multiple_of(kb * tk, tk)
        sc_scr[kb] = idx_score(ki_scr[pl.ds(r0, tk), :])
        return 0

    lax.fori_loop(0, n_full, score_body, 0)
    scd_scr[...] = jnp.where(dmask, idx_score(ki_scr[pl.ds(d0, tq), :]), -jnp.inf)

    def fold(tile_fn, diag_fn, comb, reduce, init):
        def body(kb, acc):
            for j in range(n_tiles):
                acc = comb(acc, tile_fn(sc_scr[kb, :, LANES * j:LANES * (j + 1)], kb * tk + LANES * j))
            return acc
        acc = lax.fori_loop(0, n_full, body, jnp.full((tq, LANES), init, F32))
        return comb(reduce(acc, axis=-1, keepdims=True), reduce(diag_fn(scd_scr[...]), axis=-1, keepdims=True))

    def wide(v):
        return jnp.broadcast_to(v, (tq, LANES))

    def count_ge(thr):
        thr_w = wide(thr)
        return fold(lambda s, _: jnp.where(s >= thr_w, 1.0, 0.0), lambda s: jnp.where(s >= thr, 1.0, 0.0),
                    jnp.add, jnp.sum, 0.0)

    def count_gt(thr):
        thr_w = wide(thr)
        return fold(lambda s, _: jnp.where(s > thr_w, 1.0, 0.0), lambda s: jnp.where(s > thr, 1.0, 0.0),
                    jnp.add, jnp.sum, 0.0)

    def min_ge(thr):
        thr_w = wide(thr)
        return fold(lambda s, _: jnp.where(s >= thr_w, s, jnp.inf), lambda s: jnp.where(s >= thr, s, jnp.inf),
                    jnp.minimum, jnp.min, jnp.inf)

    row_max = fold(lambda s, _: s, lambda s: s, jnp.maximum, jnp.max, -jnp.inf)
    row_min = fold(lambda s, _: s, lambda s: jnp.where(s > -jnp.inf, s, jnp.inf), jnp.minimum, jnp.min, jnp.inf)

    def bisect(_, lh):
        lo, hi = lh
        mid = lo + 0.5 * (hi - lo)
        ge = count_ge(mid) >= k_sel
        return jnp.where(ge, mid, lo), jnp.where(ge, hi, mid)

    def settle(lo):
        thr = min_ge(lo)
        n_gt = count_gt(thr)
        return thr, n_gt, jnp.max(jnp.where(n_gt >= k_sel, 1, 0))

    hi0 = row_max + jnp.abs(row_max) * 1e-6 + 1e-30
    lo, hi = lax.fori_loop(0, 20, bisect, (row_min, hi0))
    thr, n_gt, bad = settle(lo)

    def more(state):
        lo, hi, _, _, _, it = state
        lo, hi = lax.fori_loop(0, 4, bisect, (lo, hi))
        thr, n_gt, bad = settle(lo)
        return lo, hi, thr, n_gt, bad, it + 1

    _, _, thr, n_gt, _, _ = lax.while_loop(lambda st: (st[4] > 0) & (st[5] < 64), more,
                                           (lo, hi, thr, n_gt, bad, 0))

    n_ge = count_ge(thr)
    need = k_sel - n_gt
    excess = n_ge > k_sel
    thr_w = wide(thr)

    def cut_search(_):
        def count_tied_upto(j):
            j_w = wide(j)
            def tile(s, base):
                idx = base + lane
                return jnp.where((s == thr_w) & (idx <= j_w), 1.0, 0.0)
            def diag(s):
                idx = d0 + lax.broadcasted_iota(jnp.int32, (tq, tq), 1)
                return jnp.where((s == thr) & (idx <= j), 1.0, 0.0)
            return fold(tile, diag, jnp.add, jnp.sum, 0.0)

        def step(_, jj):
            jlo, jhi = jj
            mid = (jlo + jhi) >> 1
            ok = count_tied_upto(mid) >= need
            return jnp.where(ok, jlo, mid), jnp.where(ok, mid, jhi)

        steps = int(math.ceil(math.log2(l_len))) + 1
        _, jhi = lax.fori_loop(0, steps, step,
                               (jnp.full((tq, 1), -1, jnp.int32), jnp.full((tq, 1), l_len - 1, jnp.int32)))
        return jnp.where(excess, jhi, l_len)

    cut = lax.cond(jnp.max(jnp.where(excess, 1, 0)) > 0, cut_search,
                   lambda _: jnp.full((tq, 1), l_len, jnp.int32), 0)
    cut_w = wide(cut)

    def picked_tile(s, base):
        return (s > thr_w) | ((s == thr_w) & (base + lane <= cut_w))

    idx_d = d0 + lax.broadcasted_iota(jnp.int32, (tq, tq), 1)
    sd = scd_scr[...]
    picked_d = (sd > thr) | ((sd == thr) & (idx_d <= cut))

    qs = []
    for pair in range(H_B // 2):
        blk = q_ref[:, LANES * pair:LANES * (pair + 1)]
        qs += [blk * sel_lo, blk * sel_hi]
    qs = jnp.concatenate(qs, axis=0)

    def attend(r0, n, bias, carry):
        s = _dot_nt(qs, kk_scr[pl.ds(r0, n), :]) + jnp.concatenate([bias] * H_B, axis=0)
        return _softmax_step(s, vv_scr[pl.ds(r0, n), :], carry)

    def body(kb, carry):
        bias = jnp.concatenate(
            [jnp.where(picked_tile(sc_scr[kb, :, LANES * j:LANES * (j + 1)], kb * tk + LANES * j), 0.0, NEG_INF)
             for j in range(n_tiles)], axis=1)
        return attend(pl.multiple_of(kb * tk, tk), tk, bias, carry)

    carry = lax.fori_loop(0, n_full, body, _softmax_init(H_B * tq))
    _, l, acc = attend(d0, tq, jnp.where(picked_d, 0.0, NEG_INF), carry)
    out = acc / l
    for pair in range(H_B // 2):
        even, odd = out[tq * 2 * pair:tq * (2 * pair + 1)], out[tq * (2 * pair + 1):tq * (2 * pair + 2)]
        o_ref[:, LANES * pair:LANES * (pair + 1)] = jnp.where(low, even, odd).astype(BF16)


def _dsat_kernel(*refs, p_len, t_len, tq, tk, n_sel):
    if p_len:
        q_ref, qi_ref, wi_ref, new_ref, past_ref, o_ref, kk_scr, vv_scr, ki_scr, sc_scr, scd_scr = refs
    else:
        q_ref, qi_ref, wi_ref, new_ref, o_ref, kk_scr, vv_scr, ki_scr, sc_scr, scd_scr = refs
        past_ref = None
    qi = pl.program_id(1)
    l_len = p_len + t_len
    k_sel = float(n_sel)
    sub = 8

    @pl.when(qi == 0)
    def _fill():
        r1 = lax.broadcasted_iota(jnp.int32, (LANES, 2 * LANES), 0)
        c1 = lax.broadcasted_iota(jnp.int32, (LANES, 2 * LANES), 1)
        dup_kv = jnp.where((c1 % DH_B == r1 % DH_B) & (c1 // LANES == r1 // DH_B), 1.0, 0.0).astype(BF16)
        r2 = lax.broadcasted_iota(jnp.int32, (D_IDX, LANES), 0)
        c2 = lax.broadcasted_iota(jnp.int32, (D_IDX, LANES), 1)
        dup_ki = jnp.where(c2 % D_IDX == r2, 1.0, 0.0).astype(BF16)

        def fill(src_ref, dst, n):
            for r0 in range(0, n, ROW_CHUNK):
                rc = min(ROW_CHUNK, n - r0)
                kv2 = _dot(src_ref[r0:r0 + rc, :LANES].astype(BF16), dup_kv)
                kk_scr[dst + r0:dst + r0 + rc, :] = kv2[:, :LANES].astype(BF16)
                vv_scr[dst + r0:dst + r0 + rc, :] = kv2[:, LANES:].astype(BF16)
                ki_scr[dst + r0:dst + r0 + rc, :] = _dot(src_ref[r0:r0 + rc, LANES:].astype(BF16), dup_ki).astype(BF16)
        if p_len:
            fill(past_ref, 0, p_len)
        fill(new_ref, p_len, t_len)

    d0 = pl.multiple_of(p_len + qi * tq, tq)
    n_full = (p_len + qi * tq) // tk
    lane = lax.broadcasted_iota(jnp.int32, (tq, LANES), 1)
    low = lane < DH_B
    sel_lo = jnp.where(low, 1.0, 0.0).astype(BF16)
    sel_hi = jnp.where(low, 0.0, 1.0).astype(BF16)
    dmask = (lax.broadcasted_iota(jnp.int32, (tq, tq), 0) // CHUNK) <= (lax.broadcasted_iota(jnp.int32, (tq, tq), 1) // CHUNK)

    w_t = jnp.transpose(wi_ref[...])
    q_idx = []
    for pr in range(H_IDX // 2):
        blk = qi_ref[:, LANES * pr:LANES * (pr + 1)]
        q_idx += [blk * sel_lo, blk * sel_hi]

    def idx_score(ki2):
        sc = None
        for h in range(H_IDX):
            term = jnp.maximum(_dot_nt(ki2, q_idx[h]), 0.0) * w_t[h:h + 1, :]
            sc = term if sc is None else sc + term
        return sc

    def score_body(kb, _):
        sc_scr[kb] = idx_score(ki_scr[pl.ds(pl.multiple_of(kb * tk, tk), tk), :])
        return 0

    lax.fori_loop(0, n_full, score_body, 0)
    scd_scr[...] = jnp.where(dmask, idx_score(ki_scr[pl.ds(d0, tq), :]), -jnp.inf)

    def fold(fn, comb, reduce, init):
        def part(x):
            return reduce(x.reshape(x.shape[0] // sub, sub, tq), axis=0)
        acc = lax.fori_loop(0, n_full, lambda kb, a: comb(a, part(fn(sc_scr[kb], kb * tk))),
                            jnp.full((sub, tq), init, F32))
        acc = comb(acc, part(fn(scd_scr[...], d0)))
        return reduce(acc, axis=0, keepdims=True)

    def count_ge(thr):
        return fold(lambda s, _: jnp.where(s >= thr, 1.0, 0.0), jnp.add, jnp.sum, 0.0)

    def count_gt(thr):
        return fold(lambda s, _: jnp.where(s > thr, 1.0, 0.0), jnp.add, jnp.sum, 0.0)

    def min_ge(thr):
        return fold(lambda s, _: jnp.where(s >= thr, s, jnp.inf), jnp.minimum, jnp.min, jnp.inf)

    row_max = fold(lambda s, _: s, jnp.maximum, jnp.max, -jnp.inf)
    row_min = fold(lambda s, _: jnp.where(s > -jnp.inf, s, jnp.inf), jnp.minimum, jnp.min, jnp.inf)

    def bisect(_, lh):
        lo, hi = lh
        mid = lo + 0.5 * (hi - lo)
        ge = count_ge(mid) >= k_sel
        return jnp.where(ge, mid, lo), jnp.where(ge, hi, mid)

    def settle(lo):
        thr = min_ge(lo)
        n_gt = count_gt(thr)
        return thr, n_gt, jnp.max(jnp.where(n_gt >= k_sel, 1, 0))

    hi0 = row_max + jnp.abs(row_max) * 1e-6 + 1e-30
    lo, hi = lax.fori_loop(0, 20, bisect, (row_min, hi0))
    thr, n_gt, bad = settle(lo)

    def more(state):
        lo, hi, _, _, _, it = state
        lo, hi = lax.fori_loop(0, 4, bisect, (lo, hi))
        thr, n_gt, bad = settle(lo)
        return lo, hi, thr, n_gt, bad, it + 1

    _, _, thr, n_gt, _, _ = lax.while_loop(lambda st: (st[4] > 0) & (st[5] < 64), more,
                                           (lo, hi, thr, n_gt, bad, 0))

    n_ge = count_ge(thr)
    need = k_sel - n_gt
    excess = n_ge > k_sel

    def key_index(s, base):
        return base + lax.broadcasted_iota(jnp.int32, s.shape, 0)

    def cut_search(_):
        def count_tied_upto(j):
            return fold(lambda s, base: jnp.where((s == thr) & (key_index(s, base) <= j), 1.0, 0.0),
                        jnp.add, jnp.sum, 0.0)

        def step(_, jj):
            jlo, jhi = jj
            mid = (jlo + jhi) >> 1
            ok = count_tied_upto(mid) >= need
            return jnp.where(ok, jlo, mid), jnp.where(ok, mid, jhi)

        steps = int(math.ceil(math.log2(l_len))) + 1
        _, jhi = lax.fori_loop(0, steps, step,
                               (jnp.full((1, tq), -1, jnp.int32), jnp.full((1, tq), l_len - 1, jnp.int32)))
        return jnp.where(excess, jhi, l_len)

    cut = lax.cond(jnp.max(jnp.where(excess, 1, 0)) > 0, cut_search,
                   lambda _: jnp.full((1, tq), l_len, jnp.int32), 0)

    def bias_of(s, base):
        return jnp.where((s > thr) | ((s == thr) & (key_index(s, base) <= cut)), 0.0, NEG_INF)

    qs = []
    for pair in range(H_B // 2):
        blk = q_ref[:, LANES * pair:LANES * (pair + 1)]
        qs += [blk * sel_lo, blk * sel_hi]

    def attend(r0, n, bias_t, carries):
        bias = jnp.transpose(bias_t)
        kk, vv = kk_scr[pl.ds(r0, n), :], vv_scr[pl.ds(r0, n), :]
        return tuple(_softmax_step(_dot_nt(qs[h], kk) + bias, vv, carries[h]) for h in range(H_B))

    carries = lax.fori_loop(
        0, n_full, lambda kb, c: attend(pl.multiple_of(kb * tk, tk), tk, bias_of(sc_scr[kb], kb * tk), c),
        tuple(_softmax_init(tq) for _ in range(H_B)))
    carries = attend(d0, tq, bias_of(scd_scr[...], d0), carries)
    outs = [acc / l for _, l, acc in carries]
    for pair in range(H_B // 2):
        o_ref[:, LANES * pair:LANES * (pair + 1)] = jnp.where(low, outs[2 * pair], outs[2 * pair + 1]).astype(BF16)


def _dsa(q, qidx, wi, rows, past, layer, tq, tk):
    b, t, _ = q.shape
    p_len = 0 if past is None else past.shape[2]
    l_len = p_len + t
    n_sel = min(TOPK_MAX, l_len // 4)
    n_blocks = max((l_len - tq) // tk, 1)
    args = [q, qidx, wi, rows] + ([past] if past is not None else [])
    transposed = tq % LANES == 0
    body = _dsat_kernel if transposed else _dsa_kernel
    return pl.pallas_call(
        functools.partial(body, p_len=p_len, t_len=t, tq=tq, tk=tk, n_sel=n_sel),
        grid=(b, t // tq),
        in_specs=_attn_specs(tq, [256, 512, LANES], rows, past, layer),
        out_specs=_out_spec(tq),
        out_shape=jax.ShapeDtypeStruct((b, t, 256), BF16),
        scratch_shapes=[pltpu.VMEM((l_len, LANES), BF16)] * 3
                       + [pltpu.VMEM((n_blocks, tk, tq) if transposed else (n_blocks, tq, tk), F32),
                          pltpu.VMEM((tq, tq), F32)],
        compiler_params=_params("parallel", "arbitrary"),
        name="dsa",
    )(*args)


def _gather_cols(w, src, scale):
    src = [int(s) for s in src]
    scale = [float(c) if s >= 0 else 0.0 for s, c in zip(src, scale)]
    pieces, i = [], 0
    while i < len(src):
        j = i + 1
        while j < len(src) and scale[j] == scale[i] and (src[j] == src[j - 1] + 1 if src[i] >= 0 else src[j] < 0):
            j += 1
        if src[i] < 0:
            pieces.append(jnp.zeros((w.shape[0], j - i), w.dtype))
        else:
            run = w[:, src[i]:src[i] + j - i]
            pieces.append(run if scale[i] == 1.0 else run * scale[i])
        i = j
    return jnp.concatenate(pieces, axis=1)


class _Cols:
    def __init__(self):
        self.src, self.scale, self.rope = [], [], []

    def plain(self, start, n, scale=1.0):
        for i in range(n):
            self._add(start + i, scale, None)

    def pad(self, n):
        for _ in range(n):
            self._add(-1, 0.0, None)

    def roped(self, start, d, scale=1.0):
        half = d // 2
        for i in range(d):
            self._add(start + i, scale, (d, i % half, -1.0 if i < half else 1.0))

    def _add(self, src, scale, rope):
        self.src.append(src)
        self.scale.append(scale)
        self.rope.append(rope)

    def weights(self, w):
        return _gather_cols(w, self.src, self.scale).astype(BF16)

    def tables(self, pos):
        cs = {}
        for d in sorted({r[0] for r in self.rope if r is not None}):
            freqs = ROPE_THETA ** (-jnp.arange(d // 2, dtype=F32) * 2.0 / d)
            ang = pos.astype(F32)[:, None] * freqs[None, :]
            cs[d] = (jnp.cos(ang), jnp.sin(ang))
        cos, sin, i, n = [], [], 0, pos.shape[0]
        while i < len(self.rope):
            r, j = self.rope[i], i + 1
            if r is None:
                while j < len(self.rope) and self.rope[j] is None:
                    j += 1
                cos.append(jnp.ones((n, j - i), F32))
                sin.append(jnp.zeros((n, j - i), F32))
            else:
                while j < len(self.rope) and self.rope[j] == (r[0], r[1] + j - i, r[2]):
                    j += 1
                cos.append(cs[r[0]][0][:, r[1]:r[1] + j - i])
                sin.append(r[2] * cs[r[0]][1][:, r[1]:r[1] + j - i])
            i = j
        return jnp.concatenate(cos, axis=1), jnp.concatenate(sin, axis=1)


def _plans():
    p = _Cols()
    p.plain(O_CQ, Q_LORA)
    p.plain(O_CKV, KV_LORA)
    for h in range(H_C):
        p.plain(O_KC + DH_C * h, DH_C)
        p.plain(O_VC + DH_C * h, DH_C)
    for h in range(H_C):
        p.plain(O_QC + DH_C * h, DH_C, DH_C ** -0.5)
        p.pad(LANES - DH_C)
    p.plain(O_WIB, H_IDX)
    p.pad(LANES - H_IDX)
    assert len(p.src) == P_END

    r = _Cols()
    r.roped(O_KB, DH_B)
    r.plain(O_VB, DH_B)
    r.roped(O_KIB, D_IDX)
    r.pad(R_DIFF - (2 * DH_B + D_IDX))
    for h in range(H_D):
        r.roped(O_KD + 2 * DQK_D * h, DQK_D)
        r.roped(O_KD + 2 * DQK_D * h + DQK_D, DQK_D)
        r.plain(O_VD + DV_D * h, DV_D)
    r.roped(O_KR, ROPE_A)
    r.pad(LANES - ROPE_A)
    for h in range(H_B):
        r.roped(O_QB + DH_B * h, DH_B, DH_B ** -0.5)
    for h in range(H_IDX):
        r.roped(O_QIB + D_IDX * h, D_IDX)
    for h in range(H_D):
        r.roped(O_QD + 2 * DQK_D * h, DQK_D)
        r.roped(O_QD + 2 * DQK_D * h + DQK_D, DQK_D)
        r.pad(LANES - 2 * DQK_D)
    assert len(r.src) == R_END

    q = _Cols()
    for h in range(H_A):
        q.plain((NOPE_A + ROPE_A) * h, NOPE_A)
        q.roped((NOPE_A + ROPE_A) * h + NOPE_A, ROPE_A)
        q.pad(LANES - NOPE_A - ROPE_A)
    return p, r, q


def _layer_weights(l, prm, plans):
    p, r, q = plans
    w_in = prm["w_in"][l]
    wkvb = prm["mla_w_kvb"][l]
    k_src, v_src = [], []
    for h in range(H_A):
        k_src += list(range((NOPE_A + V_A) * h, (NOPE_A + V_A) * h + NOPE_A)) + [-1] * (LANES - NOPE_A)
        v_src += list(range((NOPE_A + V_A) * h + NOPE_A, (NOPE_A + V_A) * (h + 1)))
    place = np.zeros((ROPE_A, H_A * LANES), np.float32)
    for h in range(H_A):
        place[np.arange(ROPE_A), LANES * h + NOPE_A + np.arange(ROPE_A)] = 1.0
    dn = jnp.concatenate([jnp.zeros((2 * DQK_D,), F32), prm["diff_norm"][l]])[None, :]

    def ff(tag):
        return (prm[f"norm_{tag}"][l][None, :], prm[f"w_{tag}_gate"][l].astype(BF16),
                prm[f"w_{tag}_up"][l].astype(BF16), prm[f"w_{tag}_down"][l].astype(BF16))

    return {
        "ff1": ff("ff1"), "ff2": ff("ff2"),
        "nm": prm["norm_mix"][l][None, :],
        "wp": p.weights(w_in), "wr": r.weights(w_in),
        "qn": prm["mla_q_norm"][l][None, :], "wqb": q.weights(prm["mla_w_qb"][l]),
        "kvn": prm["mla_kv_norm"][l][None, :],
        "wk": _gather_cols(wkvb, k_src, np.ones(len(k_src))).astype(BF16),
        "pk": jnp.asarray(place).astype(BF16),
        "wv": _gather_cols(wkvb, v_src, np.ones(len(v_src))).astype(BF16),
        "lam": prm["diff_lambda"][l], "dn": dn,
        "wo": prm["w_out"][l].astype(BF16),
    }


def _trunk(x, caches, pos, layers, plans, fnorm, tq, tk, proj_fold):
    b, t, _ = x.shape
    depth = len(layers)
    tm = min(512, b * t)
    _, r_plan, q_plan = plans
    pb, pt = (b // proj_fold, t * proj_fold)
    pos_p = jnp.tile(pos, proj_fold)
    tabs = r_plan.tables(pos_p) + q_plan.tables(pos_p)
    tq_p = min(256, pt)
    xf = x.reshape(b * t, D_MODEL)
    stacked = [jnp.zeros((depth, pb, pt * r, w), F32)
               for r, w in ((1, KV_LORA + ROPE_A), (1, 2 * DH_B + D_IDX), (H_C, 2 * DH_C), (H_D, 2 * DQK_D + DV_D))]
    past = [None] * 4 if caches is None else caches
    for l, lw in enumerate(layers):
        lam_init = 0.8 - 0.6 * math.exp(-0.3 * l)
        xf = _ffn(xf, None, None, *lw["ff1"], None, tm)
        outs = _proj(xf.reshape(pb, pt, D_MODEL), lw, tabs, stacked, l, tq_p)
        stacked = list(outs[:4])
        r_sb, r_diff, q_mla, q_dsa, q_idx, w_idx, q_sb, q_diff = [o.reshape(b, t, o.shape[-1]) for o in outs[4:]]
        r_mla, r_dsa = (a.reshape(depth, b, t, a.shape[-1]) for a in stacked[:2])
        mix = (
            _mla(q_mla, r_mla, past[0], l, lw, tq, tk),
            _dsa(q_dsa, q_idx, w_idx, r_dsa, past[1], l, tq, tk),
            _sb(q_sb, r_sb, past[2], l, tq, tk),
            _diff(q_diff, r_diff, past[3], l, lw, lam_init, tq, tk),
        )
        mix = [m.reshape(b * t, 256) for m in mix]
        xf = _ffn(xf, mix, lw["wo"], *lw["ff2"], fnorm if l == depth - 1 else None, tm)
    y = xf.reshape(b, t, D_MODEL)
    return (y, stacked[0].reshape(depth, b, t, -1), stacked[1].reshape(depth, b, t, -1),
            stacked[2].reshape(depth, b, t, H_C, 2 * DH_C),
            stacked[3].reshape(depth, b, t, H_D, 2 * DQK_D + DV_D))


def kernel(x_prompt, x_sample, cache_mla, cache_dsa, cache_sb, cache_diff, norm_ff1, w_ff1_gate, w_ff1_up, w_ff1_down, norm_mix, w_in, mla_q_norm, mla_w_qb, mla_kv_norm, mla_w_kvb, diff_lambda, diff_norm, w_out, norm_ff2, w_ff2_gate, w_ff2_up, w_ff2_down, final_norm):
    prm = dict(norm_ff1=norm_ff1, w_ff1_gate=w_ff1_gate, w_ff1_up=w_ff1_up, w_ff1_down=w_ff1_down,
               norm_mix=norm_mix, w_in=w_in, mla_q_norm=mla_q_norm, mla_w_qb=mla_w_qb,
               mla_kv_norm=mla_kv_norm, mla_w_kvb=mla_w_kvb, diff_lambda=diff_lambda, diff_norm=diff_norm,
               w_out=w_out, norm_ff2=norm_ff2, w_ff2_gate=w_ff2_gate, w_ff2_up=w_ff2_up, w_ff2_down=w_ff2_down)
    depth = w_in.shape[0]
    plans = _plans()
    layers = [_layer_weights(l, prm, plans) for l in range(depth)]
    fnorm = final_norm[None, :]

    t_p = x_prompt.shape[1]
    assert t_p % CHUNK == 0
    tq_p = min(512, t_p)
    out_p = _trunk(x_prompt, None, jnp.arange(t_p, dtype=jnp.int32), layers, plans, fnorm, tq_p, tq_p, 1)


    b_s, t_s, _ = x_sample.shape
    past = cache_mla.shape[2]
    assert past % CHUNK == 0 and t_s <= CHUNK, "new sample frames must sit in one chunk after whole cached chunks"
    tk_s = math.gcd(past, 512)
    fold_heads = lambda c: c.reshape(c.shape[:2] + (c.shape[2] * c.shape[3], c.shape[4]))
    caches = (cache_mla, cache_dsa, fold_heads(cache_sb), fold_heads(cache_diff))
    fold = math.gcd(b_s, max(1, 256 // t_s))
    out_s = _trunk(x_sample, caches, past + jnp.arange(t_s, dtype=jnp.int32), layers, plans, fnorm, t_s, tk_s, fold)

    return (out_p[0], out_s[0]) + tuple(out_p[1:]) + tuple(out_s[1:])
```

```python
import functools
import math

import numpy as np
import jax
import jax.numpy as jnp
from jax import lax
from jax.experimental import pallas as pl
from jax.experimental.pallas import tpu as pltpu

F32 = jnp.float32
BF16 = jnp.bfloat16

D_MODEL = 1024
CHUNK = 64
ROPE_THETA = 10000.0
EPS = 1e-6
NEG_INF = -1e30
HEAD_DIM = D_MODEL // 16
D_FF = 11 * D_MODEL // 4
H_A = 4
Q_LORA = 4 * HEAD_DIM
KV_LORA = 2 * HEAD_DIM
NOPE_A = HEAD_DIM
ROPE_A = HEAD_DIM // 2
V_A = HEAD_DIM
H_B = 4
DH_B = HEAD_DIM
H_IDX = 8
D_IDX = HEAD_DIM
TOPK_MAX = 256
H_C = 4
DH_C = HEAD_DIM
H_D = 4
DQK_D = HEAD_DIM // 2
DV_D = HEAD_DIM
COL_SIZES = (Q_LORA, KV_LORA, ROPE_A,
             H_B * DH_B, DH_B, DH_B, H_IDX * D_IDX, D_IDX, H_IDX,
             H_C * DH_C, H_C * DH_C, H_C * DH_C,
             H_D * 2 * DQK_D, H_D * 2 * DQK_D, H_D * DV_D)
(O_CQ, O_CKV, O_KR, O_QB, O_KB, O_VB, O_QIB, O_KIB, O_WIB,
 O_QC, O_KC, O_VC, O_QD, O_KD, O_VD) = np.concatenate([[0], np.cumsum(COL_SIZES)[:-1]]).tolist()

LANES = 128
VMEM_LIMIT = 56 * 1024 * 1024
MXU_TILE = 256
FF_CHUNK = MXU_TILE
ROW_CHUNK = 512

P_CQ, P_CKV, P_SB, P_QSB, P_WI, P_END = 0, 256, 384, 896, 1408, 1536
R_DSA, R_DIFF, R_KR, R_QDSA, R_QI, R_QDIFF, R_END = 0, 256, 768, 896, 1152, 1664, 2176


def _dot(a, b):
    return jnp.dot(a, b, preferred_element_type=F32)


def _dot_nt(a, b):
    return lax.dot_general(a, b, (((1,), (1,)), ((), ())), preferred_element_type=F32)


def _rms(x, g):
    return x * lax.rsqrt(jnp.mean(x * x, axis=-1, keepdims=True) + EPS) * g


def _const_spec(shape):
    n = len(shape)
    return pl.BlockSpec(shape, lambda *_: (0,) * n, pipeline_mode=pl.Buffered(1))


def _params(*sem):
    return pltpu.CompilerParams(dimension_semantics=sem, vmem_limit_bytes=VMEM_LIMIT)


def _ffn_kernel(*refs, has_mix, has_final):
    it = iter(refs)
    x_ref = next(it)
    if has_mix:
        o_refs = [next(it) for _ in range(4)]
        wo_ref = next(it)
    g_ref, wg_ref, wu_ref, wd_ref = next(it), next(it), next(it), next(it)
    fn_ref = next(it) if has_final else None
    out_ref = next(it)

    x = x_ref[...]
    if has_mix:
        for i, o_ref in enumerate(o_refs):
            x = x + _dot(o_ref[...], wo_ref[256 * i:256 * (i + 1), :])
    xn = _rms(x, g_ref[...]).astype(BF16)
    acc = jnp.zeros_like(x)
    for c0 in range(0, D_FF, FF_CHUNK):
        gate = _dot(xn, wg_ref[:, c0:c0 + FF_CHUNK])
        up = _dot(xn, wu_ref[:, c0:c0 + FF_CHUNK])
        hid = (gate * jax.nn.sigmoid(gate) * up).astype(BF16)
        acc = acc + _dot(hid, wd_ref[c0:c0 + FF_CHUNK, :])
    y = x + 0.5 * acc
    if has_final:
        y = _rms(y, fn_ref[...])
    out_ref[...] = y


def _ffn(x, mix, wo, g, wg, wu, wd, fnorm, tm):
    n = x.shape[0]
    has_mix, has_final = mix is not None, fnorm is not None
    row = lambda w: pl.BlockSpec((tm, w), lambda i: (i, 0))
    args, specs = [x], [row(D_MODEL)]
    if has_mix:
        args += list(mix) + [wo]
        specs += [row(256)] * 4 + [_const_spec(wo.shape)]
    args += [g, wg, wu, wd]
    specs += [_const_spec(g.shape), _const_spec(wg.shape), _const_spec(wu.shape), _const_spec(wd.shape)]
    if has_final:
        args.append(fnorm)
        specs.append(_const_spec(fnorm.shape))
    return pl.pallas_call(
        functools.partial(_ffn_kernel, has_mix=has_mix, has_final=has_final),
        grid=(n // tm,),
        in_specs=specs,
        out_specs=row(D_MODEL),
        out_shape=jax.ShapeDtypeStruct((n, D_MODEL), F32),
        compiler_params=_params("parallel"),
        name="ffn",
    )(*args)


def _rope(h, cos, sin_signed, d):
    w = h.shape[1]
    first = lax.broadcasted_iota(jnp.int32, h.shape, 1) % d < d // 2
    partner = jnp.where(first, pltpu.roll(h, w - d // 2, axis=1), pltpu.roll(h, d // 2, axis=1))
    return h * cos + partner * sin_signed


def _proj_kernel(x_ref, g_ref, wp_ref, wr_ref, cr_ref, sr_ref,
                 qn_ref, wqb_ref, cq_ref, sq_ref, kvn_ref,
                 _amla, _adsa, _asb, _adiff,
                 rmla_ref, rdsa_ref, rsbh_ref, rdiffh_ref, rsb_ref, rdiff_ref,
                 qmla_ref, qdsa_ref, qi_ref, wi_ref, qsb_ref, qdiff_ref):
    xn = _rms(x_ref[...], g_ref[...]).astype(BF16)
    tq = x_ref.shape[0]

    def plain(a, b):
        return _dot(xn, wp_ref[:, a:b])

    def roped(a, b, d):
        return _rope(_dot(xn, wr_ref[:, a:b]), cr_ref[:, a:b], sr_ref[:, a:b], d)

    cqn = _rms(plain(P_CQ, P_CKV), qn_ref[...]).astype(BF16)
    qmla_ref[...] = _rope(_dot(cqn, wqb_ref[...]), cq_ref[...], sq_ref[...], ROPE_A).astype(BF16)
    rmla_ref[:, :KV_LORA] = _rms(plain(P_CKV, P_SB), kvn_ref[...])
    rmla_ref[:, KV_LORA:] = roped(R_KR, R_QDSA, ROPE_A)[:, :ROPE_A]
    rdsa_ref[...] = roped(R_DSA, R_DIFF, DH_B)[:, :2 * DH_B + D_IDX]
    qdsa_ref[...] = roped(R_QDSA, R_QI, DH_B).astype(BF16)
    qi_ref[...] = roped(R_QI, R_QDIFF, D_IDX).astype(BF16)
    wi_ref[...] = plain(P_WI, P_END) * (H_IDX ** -0.5)
    rsb = plain(P_SB, P_QSB)
    rsb_ref[...] = rsb
    for h in range(H_C):
        rsbh_ref[pl.ds(h, tq, stride=H_C), :] = rsb[:, LANES * h:LANES * (h + 1)]
    qsb_ref[...] = plain(P_QSB, P_WI).astype(BF16)
    rdiff = roped(R_DIFF, R_KR, DQK_D)
    rdiff_ref[...] = rdiff
    for h in range(H_D):
        rdiffh_ref[pl.ds(h, tq, stride=H_D), :] = rdiff[:, LANES * h:LANES * (h + 1)]
    qdiff_ref[...] = roped(R_QDIFF, R_END, DQK_D).astype(BF16)


def _proj(x, lw, tabs, stacked, layer, tq):
    b, t, _ = x.shape
    cr, sr, cq, sq = tabs
    tok = lambda w: pl.BlockSpec((None, tq, w), lambda bi, i: (bi, i, 0))
    lay = lambda r, w: pl.BlockSpec((None, None, r, w), lambda bi, i: (layer, bi, i, 0))
    tab = lambda w: pl.BlockSpec((tq, w), lambda bi, i: (i, 0))
    consts = [lw["nm"], lw["wp"], lw["wr"]]
    consts2 = [lw["qn"], lw["wqb"]]
    in_specs = ([tok(D_MODEL)] + [_const_spec(a.shape) for a in consts] + [tab(R_END), tab(R_END)]
                + [_const_spec(a.shape) for a in consts2] + [tab(512), tab(512), _const_spec(lw["kvn"].shape)]
                + [pl.BlockSpec(memory_space=pl.ANY)] * 4)
    widths = [(512, F32), (512, F32),
              (512, BF16), (256, BF16), (512, BF16), (LANES, F32), (512, BF16), (512, BF16)]
    n_in = len(in_specs)
    return pl.pallas_call(
        _proj_kernel,
        grid=(b, t // tq),
        in_specs=in_specs,
        out_specs=[lay(tq, 160), lay(tq, 192), lay(tq * H_C, LANES), lay(tq * H_D, LANES)]
                  + [tok(w) for w, _ in widths],
        out_shape=[jax.ShapeDtypeStruct(a.shape, a.dtype) for a in stacked]
                  + [jax.ShapeDtypeStruct((b, t, w), dt) for w, dt in widths],
        input_output_aliases={n_in - 4 + k: k for k in range(4)},
        compiler_params=_params("parallel", "parallel"),
        name="proj",
    )(x, *consts, cr, sr, *consts2, cq, sq, lw["kvn"], *stacked)


def _chunk_mask(tq):
    ri = lax.broadcasted_iota(jnp.int32, (tq, tq), 0)
    ci = lax.broadcasted_iota(jnp.int32, (tq, tq), 1)
    return (ci // CHUNK) <= (ri // CHUNK)


LOG2E = math.log2(math.e)


def _softmax_step(s, v, carry, scale=1.0):
    m, l, acc = carry
    c = scale * LOG2E
    m_new = jnp.maximum(m, jnp.max(s, axis=-1, keepdims=True))
    alpha = jnp.exp2((m - m_new) * c)
    p = jnp.exp2((s - m_new) * c)
    l = alpha * l + jnp.sum(p, axis=-1, keepdims=True)
    acc = alpha * acc + _dot(p.astype(BF16), v)
    return m_new, l, acc


def _softmax_init(tq):
    return (jnp.full((tq, 1), NEG_INF, F32), jnp.zeros((tq, 1), F32), jnp.zeros((tq, LANES), F32))


def _lane_lt(tq, n):
    return lax.broadcasted_iota(jnp.int32, (tq, LANES), 1) < n


def _attn_specs(tq, q_widths, rows, past, layer):
    def whole(a):
        if a.ndim == 3:
            return pl.BlockSpec((None,) + a.shape[1:], lambda bi, i: (bi, 0, 0))
        return pl.BlockSpec((None, None) + a.shape[2:], lambda bi, i: (layer, bi, 0, 0))
    specs = [pl.BlockSpec((None, tq, w), lambda bi, i: (bi, i, 0)) for w in q_widths]
    return specs + [whole(rows)] + ([whole(past)] if past is not None else [])


def _out_spec(tq):
    return pl.BlockSpec((None, tq, 256), lambda bi, i: (bi, i, 0))


def _mla_kernel(*refs, p_len, t_len, tq, tk):
    if p_len:
        q_ref, new_ref, past_ref, wk_ref, pk_ref, wv_ref, o_ref, k_scr, v_scr = refs
    else:
        q_ref, new_ref, wk_ref, pk_ref, wv_ref, o_ref, k_scr, v_scr = refs
        past_ref = None
    qi = pl.program_id(1)
    scale = (NOPE_A + ROPE_A) ** -0.5

    @pl.when(qi == 0)
    def _fill():
        def fill(src_ref, dst, n):
            for r0 in range(0, n, ROW_CHUNK):
                rc = min(ROW_CHUNK, n - r0)
                lat = src_ref[r0:r0 + rc, :KV_LORA].astype(BF16)
                rope = src_ref[r0:r0 + rc, KV_LORA:].astype(BF16)
                k_scr[dst + r0:dst + r0 + rc, :] = (_dot(lat, wk_ref[...]) + _dot(rope, pk_ref[...])).astype(BF16)
                v_scr[dst + r0:dst + r0 + rc, :] = _dot(lat, wv_ref[...]).astype(BF16)
        if p_len:
            fill(past_ref, 0, p_len)
        fill(new_ref, p_len, t_len)

    d0 = pl.multiple_of(p_len + qi * tq, tq)
    n_full = (p_len + qi * tq) // tk
    dmask = _chunk_mask(tq)
    low = _lane_lt(tq, V_A)
    qs = [q_ref[:, LANES * h:LANES * (h + 1)] for h in range(H_A)]

    def step(r0, n, mask, carries):
        out = []
        for h in range(H_A):
            s = _dot_nt(qs[h], k_scr[pl.ds(r0, n), LANES * h:LANES * (h + 1)])
            if mask is not None:
                s = jnp.where(mask, s, NEG_INF)
            out.append(_softmax_step(s, v_scr[pl.ds(r0, n), LANES * (h // 2):LANES * (h // 2 + 1)], carries[h],
                                     scale))
        return tuple(out)

    carries = lax.fori_loop(0, n_full, lambda kb, c: step(pl.multiple_of(kb * tk, tk), tk, None, c),
                            tuple(_softmax_init(tq) for _ in range(H_A)))
    carries = step(d0, tq, dmask, carries)
    outs = [acc / l for _, l, acc in carries]
    for pair in range(H_A // 2):
        o_ref[:, LANES * pair:LANES * (pair + 1)] = jnp.where(low, outs[2 * pair], outs[2 * pair + 1]).astype(BF16)


def _mla(q, rows, past, layer, lw, tq, tk):
    b, t, _ = q.shape
    p_len = 0 if past is None else past.shape[2]
    consts = [lw["wk"], lw["pk"], lw["wv"]]
    args = [q, rows] + ([past] if past is not None else []) + consts
    return pl.pallas_call(
        functools.partial(_mla_kernel, p_len=p_len, t_len=t, tq=tq, tk=tk),
        grid=(b, t // tq),
        in_specs=_attn_specs(tq, [512], rows, past, layer) + [_const_spec(a.shape) for a in consts],
        out_specs=_out_spec(tq),
        out_shape=jax.ShapeDtypeStruct((b, t, 256), BF16),
        scratch_shapes=[pltpu.VMEM((p_len + t, 512), BF16), pltpu.VMEM((p_len + t, 256), BF16)],
        compiler_params=_params("parallel", "arbitrary"),
        name="mla",
    )(*args)


def _diff_kernel(*refs, p_len, t_len, tq, tk, lam_init):
    if p_len:
        q_ref, new_ref, past_ref, lam_ref, dn_ref, o_ref = refs
    else:
        q_ref, new_ref, lam_ref, dn_ref, o_ref = refs
        past_ref = None
    qi = pl.program_id(1)
    scale = DQK_D ** -0.5
    lam = lam_ref[...]
    lam_full = (jnp.exp(jnp.sum(lam[0:1] * lam[1:2], axis=-1, keepdims=True))
                - jnp.exp(jnp.sum(lam[2:3] * lam[3:4], axis=-1, keepdims=True)) + lam_init)
    dmask = _chunk_mask(tq)
    lane = lax.broadcasted_iota(jnp.int32, (tq, LANES), 1)
    sel1 = jnp.where(lane < DQK_D, 1.0, 0.0).astype(BF16)
    sel2 = jnp.where((lane >= DQK_D) & (lane < 2 * DQK_D), 1.0, 0.0).astype(BF16)
    is_v = lane >= 2 * DQK_D
    n_past = p_len // tk
    n_new = (qi * tq) // tk

    q12 = []
    for h in range(H_D):
        qh = q_ref[:, LANES * h:LANES * (h + 1)]
        q12.append(jnp.concatenate([qh * sel1, qh * sel2], axis=0))
    dmask2 = jnp.concatenate([dmask, dmask], axis=0)

    def step(load, masked, carries):
        out = []
        for h in range(H_D):
            kv = load(h)
            s = _dot_nt(q12[h], kv)
            if masked:
                s = jnp.where(dmask2, s, NEG_INF)
            out.append(_softmax_step(s, kv, carries[h], scale))
        return tuple(out)

    def new_rows(r0, n):
        return lambda h: new_ref[pl.ds(r0, n), LANES * h:LANES * (h + 1)].astype(BF16)

    def past_rows(kb):
        return lambda h: past_ref[pl.ds(H_D * kb * tk + h, tk, stride=H_D), :].astype(BF16)

    carries = tuple(_softmax_init(2 * tq) for _ in range(H_D))
    for kb in range(n_past):
        carries = step(past_rows(kb), False, carries)
    carries = lax.fori_loop(0, n_new, lambda kb, c: step(new_rows(pl.multiple_of(kb * tk, tk), tk), False, c),
                            carries)
    carries = step(new_rows(pl.multiple_of(qi * tq, tq), tq), True, carries)
    outs = []
    for h in range(H_D):
        _, l, acc = carries[h]
        p = acc / l
        o = p[:tq] - lam_full * p[tq:]
        ms = jnp.sum(jnp.where(is_v, o * o, 0.0), axis=-1, keepdims=True) * (1.0 / DV_D)
        outs.append(o * lax.rsqrt(ms + EPS) * dn_ref[...] * (1.0 - lam_init))
    for pair in range(H_D // 2):
        o_ref[:, LANES * pair:LANES * (pair + 1)] = jnp.where(
            is_v, outs[2 * pair + 1], pltpu.roll(outs[2 * pair], 2 * DQK_D, axis=1)).astype(BF16)


def _diff(q, rows, past, layer, lw, lam_init, tq, tk):
    b, t, _ = q.shape
    p_len = 0 if past is None else past.shape[2] // H_D
    consts = [lw["lam"], lw["dn"]]
    args = [q, rows] + ([past] if past is not None else []) + consts
    return pl.pallas_call(
        functools.partial(_diff_kernel, p_len=p_len, t_len=t, tq=tq, tk=tk, lam_init=lam_init),
        grid=(b, t // tq),
        in_specs=_attn_specs(tq, [512], rows, past, layer) + [_const_spec(a.shape) for a in consts],
        out_specs=_out_spec(tq),
        out_shape=jax.ShapeDtypeStruct((b, t, 256), BF16),
        compiler_params=_params("parallel", "parallel"),
        name="diff",
    )(*args)


def _sb_kernel(*refs, p_len, t_len, tq, tk):
    if p_len:
        q_ref, new_ref, past_ref, o_ref = refs
    else:
        q_ref, new_ref, o_ref = refs
        past_ref = None
    qi = pl.program_id(1)
    n_past = p_len // tk
    n_new = (qi * tq) // tk

    def later(n):
        n = min(n, MXU_TILE)
        ri = lax.broadcasted_iota(jnp.int32, (n, n), 0)
        ci = lax.broadcasted_iota(jnp.int32, (n, n), 1)
        return jnp.where(ri > ci, 1.0, 0.0).astype(BF16)

    def later_sums(x, tri):
        w = tri.shape[0]
        pieces, after = [], None
        for j in reversed(range(x.shape[1] // w)):
            xj = x[:, w * j:w * (j + 1)]
            hi = xj.astype(BF16)
            lo = (xj - hi.astype(F32)).astype(BF16)
            inside = _dot(hi, tri) + _dot(lo, tri)
            total = inside[:, :1] + xj[:, :1]
            if after is not None:
                inside, total = inside + after, total + after
            pieces.append(inside)
            after = total
        return (pieces[0] if len(pieces) == 1 else jnp.concatenate(pieces[::-1], axis=1)), after

    later_k = later(tk)
    later_q = later_k if min(tq, MXU_TILE) == min(tk, MXU_TILE) else later(tq)
    ri = lax.broadcasted_iota(jnp.int32, (tq, tq), 0)
    ci = lax.broadcasted_iota(jnp.int32, (tq, tq), 1)
    causal = ci < ri
    is_v = lax.broadcasted_iota(jnp.int32, (tq, LANES), 1) >= DH_C

    qs = [q_ref[:, LANES * h:LANES * (h + 1)] for h in range(H_C)]

    groups = [list(range(H_C))] if tq < LANES else [[h] for h in range(H_C)]

    def step(load, mask, tri, carries):
        out = [None] * H_C
        for group in groups:
            kvs, takes, keeps = [], [], []
            for h in group:
                kv = load(h)
                z = _dot_nt(qs[h], kv)
                log_take = jnp.minimum(z, 0.0) - jnp.log(1.0 + jnp.exp2(jnp.abs(z) * -LOG2E))
                log_keep = log_take - z
                if mask is not None:
                    log_keep = jnp.where(mask, log_keep, 0.0)
                kvs.append(kv)
                takes.append(log_take)
                keeps.append(log_keep)
            later_sum, block_sum = later_sums(keeps[0] if len(group) == 1 else jnp.concatenate(keeps, axis=0), tri)
            for i, h in enumerate(group):
                run, acc = carries[h]
                rows = slice(tq * i, tq * (i + 1))
                a = jnp.exp2((takes[i] + (later_sum[rows] + run)) * LOG2E)
                if mask is not None:
                    a = jnp.where(mask, a, 0.0)
                out[h] = (run + block_sum[rows], acc + _dot(a.astype(BF16), kvs[i]))
        return tuple(out)

    def new_rows(r0, n):
        return lambda h: new_ref[pl.ds(r0, n), LANES * h:LANES * (h + 1)].astype(BF16)

    def past_rows(kb):
        return lambda h: past_ref[pl.ds(H_C * kb * tk + h, tk, stride=H_C), :].astype(BF16)

    carries = tuple((jnp.zeros((tq, 1), F32), jnp.zeros((tq, LANES), F32)) for _ in range(H_C))
    carries = step(new_rows(pl.multiple_of(qi * tq, tq), tq), causal, later_q, carries)
    carries = lax.fori_loop(
        0, n_new,
        lambda i, c: step(new_rows(pl.multiple_of((n_new - 1 - i) * tk, tk), tk), None, later_k, c), carries)
    for kb in reversed(range(n_past)):
        carries = step(past_rows(kb), None, later_k, carries)
    for pair in range(H_C // 2):
        o_ref[:, LANES * pair:LANES * (pair + 1)] = jnp.where(
            is_v, carries[2 * pair + 1][1], pltpu.roll(carries[2 * pair][1], DH_C, axis=1)).astype(BF16)


def _sb(q, rows, past, layer, tq, tk):
    b, t, _ = q.shape
    p_len = 0 if past is None else past.shape[2] // H_C
    args = [q, rows] + ([past] if past is not None else [])
    return pl.pallas_call(
        functools.partial(_sb_kernel, p_len=p_len, t_len=t, tq=tq, tk=tk),
        grid=(b, t // tq),
        in_specs=_attn_specs(tq, [512], rows, past, layer),
        out_specs=_out_spec(tq),
        out_shape=jax.ShapeDtypeStruct((b, t, 256), BF16),
        compiler_params=_params("parallel", "parallel"),
        name="sb",
    )(*args)


def _dsa_kernel(*refs, p_len, t_len, tq, tk, n_sel):
    if p_len:
        q_ref, qi_ref, wi_ref, new_ref, past_ref, o_ref, kk_scr, vv_scr, ki_scr, sc_scr, scd_scr = refs
    else:
        q_ref, qi_ref, wi_ref, new_ref, o_ref, kk_scr, vv_scr, ki_scr, sc_scr, scd_scr = refs
        past_ref = None
    qi = pl.program_id(1)
    l_len = p_len + t_len
    n_tiles = tk // LANES
    k_sel = float(n_sel)

    @pl.when(qi == 0)
    def _fill():
        r1 = lax.broadcasted_iota(jnp.int32, (LANES, 2 * LANES), 0)
        c1 = lax.broadcasted_iota(jnp.int32, (LANES, 2 * LANES), 1)
        dup_kv = jnp.where((c1 % DH_B == r1 % DH_B) & (c1 // LANES == r1 // DH_B), 1.0, 0.0).astype(BF16)
        r2 = lax.broadcasted_iota(jnp.int32, (D_IDX, LANES), 0)
        c2 = lax.broadcasted_iota(jnp.int32, (D_IDX, LANES), 1)
        dup_ki = jnp.where(c2 % D_IDX == r2, 1.0, 0.0).astype(BF16)

        def fill(src_ref, dst, n):
            for r0 in range(0, n, ROW_CHUNK):
                rc = min(ROW_CHUNK, n - r0)
                kv2 = _dot(src_ref[r0:r0 + rc, :LANES].astype(BF16), dup_kv)
                kk_scr[dst + r0:dst + r0 + rc, :] = kv2[:, :LANES].astype(BF16)
                vv_scr[dst + r0:dst + r0 + rc, :] = kv2[:, LANES:].astype(BF16)
                ki_scr[dst + r0:dst + r0 + rc, :] = _dot(src_ref[r0:r0 + rc, LANES:].astype(BF16), dup_ki).astype(BF16)
        if p_len:
            fill(past_ref, 0, p_len)
        fill(new_ref, p_len, t_len)

    d0 = pl.multiple_of(p_len + qi * tq, tq)
    n_full = (p_len + qi * tq) // tk
    lane = lax.broadcasted_iota(jnp.int32, (tq, LANES), 1)
    low = lane < DH_B
    sel_lo = jnp.where(low, 1.0, 0.0).astype(BF16)
    sel_hi = jnp.where(low, 0.0, 1.0).astype(BF16)
    dmask = _chunk_mask(tq)

    wi = wi_ref[...]
    wcol = [jnp.sum(jnp.where(lane == h, wi, 0.0), axis=-1, keepdims=True) for h in range(H_IDX)]
    q_idx = []
    for pr in range(H_IDX // 2):
        blk = qi_ref[:, LANES * pr:LANES * (pr + 1)]
        q_idx += [blk * sel_lo, blk * sel_hi]
    q_idx = jnp.concatenate(q_idx, axis=0)
    w_all = jnp.concatenate(wcol, axis=0)

    def idx_score(ki2):
        terms = jnp.maximum(_dot_nt(q_idx, ki2), 0.0) * w_all
        sc = terms[:tq]
        for h in range(1, H_IDX):
            sc = sc + terms[tq * h:tq * (h + 1)]
        return sc

    def score_body(kb, _):
        r0 = pl.multiple_of(kb * tk, tk)
        sc_scr[kb] = idx_score(ki_scr[pl.ds(r0, tk), :])
        return 0

    lax.fori_loop(0, n_full, score_body, 0)
    scd_scr[...] = jnp.where(dmask, idx_score(ki_scr[pl.ds(d0, tq), :]), -jnp.inf)

    def fold(tile_fn, diag_fn, comb, reduce, init):
        def body(kb, acc):
            for j in range(n_tiles):
                acc = comb(acc, tile_fn(sc_scr[kb, :, LANES * j:LANES * (j + 1)], kb * tk + LANES * j))
            return acc
        acc = lax.fori_loop(0, n_full, body, jnp.full((tq, LANES), init, F32))
        return comb(reduce(acc, axis=-1, keepdims=True), reduce(diag_fn(scd_scr[...]), axis=-1, keepdims=True))

    def wide(v):
        return jnp.broadcast_to(v, (tq, LANES))

    def count_ge(thr):
        thr_w = wide(thr)
        return fold(lambda s, _: jnp.where(s >= thr_w, 1.0, 0.0), lambda s: jnp.where(s >= thr, 1.0, 0.0),
                    jnp.add, jnp.sum, 0.0)

    def count_gt(thr):
        thr_w = wide(thr)
        return fold(lambda s, _: jnp.where(s > thr_w, 1.0, 0.0), lambda s: jnp.where(s > thr, 1.0, 0.0),
                    jnp.add, jnp.sum, 0.0)

    def min_ge(thr):
        thr_w = wide(thr)
        return fold(lambda s, _: jnp.where(s >= thr_w, s, jnp.inf), lambda s: jnp.where(s >= thr, s, jnp.inf),
                    jnp.minimum, jnp.min, jnp.inf)

    row_max = fold(lambda s, _: s, lambda s: s, jnp.maximum, jnp.max, -jnp.inf)
    row_min = fold(lambda s, _: s, lambda s: jnp.where(s > -jnp.inf, s, jnp.inf), jnp.minimum, jnp.min, jnp.inf)

    def bisect(_, lh):
        lo, hi = lh
        mid = lo + 0.5 * (hi - lo)
        ge = count_ge(mid) >= k_sel
        return jnp.where(ge, mid, lo), jnp.where(ge, hi, mid)

    def settle(lo):
        thr = min_ge(lo)
        n_gt = count_gt(thr)
        return thr, n_gt, jnp.max(jnp.where(n_gt >= k_sel, 1, 0))

    hi0 = row_max + jnp.abs(row_max) * 1e-6 + 1e-30
    lo, hi = lax.fori_loop(0, 20, bisect, (row_min, hi0))
    thr, n_gt, bad = settle(lo)

    def more(state):
        lo, hi, _, _, _, it = state
        lo, hi = lax.fori_loop(0, 4, bisect, (lo, hi))
        thr, n_gt, bad = settle(lo)
        return lo, hi, thr, n_gt, bad, it + 1

    _, _, thr, n_gt, _, _ = lax.while_loop(lambda st: (st[4] > 0) & (st[5] < 64), more,
                                           (lo, hi, thr, n_gt, bad, 0))

    n_ge = count_ge(thr)
    need = k_sel - n_gt
    excess = n_ge > k_sel
    thr_w = wide(thr)

    def cut_search(_):
        def count_tied_upto(j):
            j_w = wide(j)
            def tile(s, base):
                idx = base + lane
                return jnp.where((s == thr_w) & (idx <= j_w), 1.0, 0.0)
            def diag(s):
                idx = d0 + lax.broadcasted_iota(jnp.int32, (tq, tq), 1)
                return jnp.where((s == thr) & (idx <= j), 1.0, 0.0)
            return fold(tile, diag, jnp.add, jnp.sum, 0.0)

        def step(_, jj):
            jlo, jhi = jj
            mid = (jlo + jhi) >> 1
            ok = count_tied_upto(mid) >= need
            return jnp.where(ok, jlo, mid), jnp.where(ok, mid, jhi)

        steps = int(math.ceil(math.log2(l_len))) + 1
        _, jhi = lax.fori_loop(0, steps, step,
                               (jnp.full((tq, 1), -1, jnp.int32), jnp.full((tq, 1), l_len - 1, jnp.int32)))
        return jnp.where(excess, jhi, l_len)

    cut = lax.cond(jnp.max(jnp.where(excess, 1, 0)) > 0, cut_search,
                   lambda _: jnp.full((tq, 1), l_len, jnp.int32), 0)
    cut_w = wide(cut)

    def picked_tile(s, base):
        return (s > thr_w) | ((s == thr_w) & (base + lane <= cut_w))

    idx_d = d0 + lax.broadcasted_iota(jnp.int32, (tq, tq), 1)
    sd = scd_scr[...]
    picked_d = (sd > thr) | ((sd == thr) & (idx_d <= cut))

    qs = []
    for pair in range(H_B // 2):
        blk = q_ref[:, LANES * pair:LANES * (pair + 1)]
        qs += [blk * sel_lo, blk * sel_hi]
    qs = jnp.concatenate(qs, axis=0)

    def attend(r0, n, bias, carry):
        s = _dot_nt(qs, kk_scr[pl.ds(r0, n), :]) + jnp.concatenate([bias] * H_B, axis=0)
        return _softmax_step(s, vv_scr[pl.ds(r0, n), :], carry)

    def body(kb, carry):
        bias = jnp.concatenate(
            [jnp.where(picked_tile(sc_scr[kb, :, LANES * j:LANES * (j + 1)], kb * tk + LANES * j), 0.0, NEG_INF)
             for j in range(n_tiles)], axis=1)
        return attend(pl.multiple_of(kb * tk, tk), tk, bias, carry)

    carry = lax.fori_loop(0, n_full, body, _softmax_init(H_B * tq))
    _, l, acc = attend(d0, tq, jnp.where(picked_d, 0.0, NEG_INF), carry)
    out = acc / l
    for pair in range(H_B // 2):
        even, odd = out[tq * 2 * pair:tq * (2 * pair + 1)], out[tq * (2 * pair + 1):tq * (2 * pair + 2)]
        o_ref[:, LANES * pair:LANES * (pair + 1)] = jnp.where(low, even, odd).astype(BF16)


def _dsat_kernel(*refs, p_len, t_len, tq, tk, n_sel):
    if p_len:
        q_ref, qi_ref, wi_ref, new_ref, past_ref, o_ref, kk_scr, vv_scr, ki_scr, sc_scr, scd_scr = refs
    else:
        q_ref, qi_ref, wi_ref, new_ref, o_ref, kk_scr, vv_scr, ki_scr, sc_scr, scd_scr = refs
        past_ref = None
    qi = pl.program_id(1)
    l_len = p_len + t_len
    k_sel = float(n_sel)
    sub = 8

    @pl.when(qi == 0)
    def _fill():
        r1 = lax.broadcasted_iota(jnp.int32, (LANES, 2 * LANES), 0)
        c1 = lax.broadcasted_iota(jnp.int32, (LANES, 2 * LANES), 1)
        dup_kv = jnp.where((c1 % DH_B == r1 % DH_B) & (c1 // LANES == r1 // DH_B), 1.0, 0.0).astype(BF16)
        r2 = lax.broadcasted_iota(jnp.int32, (D_IDX, LANES), 0)
        c2 = lax.broadcasted_iota(jnp.int32, (D_IDX, LANES), 1)
        dup_ki = jnp.where(c2 % D_IDX == r2, 1.0, 0.0).astype(BF16)

        def fill(src_ref, dst, n):
            for r0 in range(0, n, ROW_CHUNK):
                rc = min(ROW_CHUNK, n - r0)
                kv2 = _dot(src_ref[r0:r0 + rc, :LANES].astype(BF16), dup_kv)
                kk_scr[dst + r0:dst + r0 + rc, :] = kv2[:, :LANES].astype(BF16)
                vv_scr[dst + r0:dst + r0 + rc, :] = kv2[:, LANES:].astype(BF16)
                ki_scr[dst + r0:dst + r0 + rc, :] = _dot(src_ref[r0:r0 + rc, LANES:].astype(BF16), dup_ki).astype(BF16)
        if p_len:
            fill(past_ref, 0, p_len)
        fill(new_ref, p_len, t_len)

    d0 = pl.multiple_of(p_len + qi * tq, tq)
    n_full = (p_len + qi * tq) // tk
    lane = lax.broadcasted_iota(jnp.int32, (tq, LANES), 1)
    low = lane < DH_B
    sel_lo = jnp.where(low, 1.0, 0.0).astype(BF16)
    sel_hi = jnp.where(low, 0.0, 1.0).astype(BF16)
    dmask = (lax.broadcasted_iota(jnp.int32, (tq, tq), 0) // CHUNK) <= (lax.broadcasted_iota(jnp.int32, (tq, tq), 1) // CHUNK)

    w_t = jnp.transpose(wi_ref[...])
    q_idx = []
    for pr in range(H_IDX // 2):
        blk = qi_ref[:, LANES * pr:LANES * (pr + 1)]
        q_idx += [blk * sel_lo, blk * sel_hi]

    def idx_score(ki2):
        sc = None
        for h in range(H_IDX):
            term = jnp.maximum(_dot_nt(ki2, q_idx[h]), 0.0) * w_t[h:h + 1, :]
            sc = term if sc is None else sc + term
        return sc

    def score_body(kb, _):
        sc_scr[kb] = idx_score(ki_scr[pl.ds(pl.multiple_of(kb * tk, tk), tk), :])
        return 0

    lax.fori_loop(0, n_full, score_body, 0)
    scd_scr[...] = jnp.where(dmask, idx_score(ki_scr[pl.ds(d0, tq), :]), -jnp.inf)

    def fold(fn, comb, reduce, init):
        def part(x):
            return reduce(x.reshape(x.shape[0] // sub, sub, tq), axis=0)
        acc = lax.fori_loop(0, n_full, lambda kb, a: comb(a, part(fn(sc_scr[kb], kb * tk))),
                            jnp.full((sub, tq), init, F32))
        acc = comb(acc, part(fn(scd_scr[...], d0)))
        return reduce(acc, axis=0, keepdims=True)

    def count_ge(thr):
        return fold(lambda s, _: jnp.where(s >= thr, 1.0, 0.0), jnp.add, jnp.sum, 0.0)

    def count_gt(thr):
        return fold(lambda s, _: jnp.where(s > thr, 1.0, 0.0), jnp.add, jnp.sum, 0.0)

    def min_ge(thr):
        return fold(lambda s, _: jnp.where(s >= thr, s, jnp.inf), jnp.minimum, jnp.min, jnp.inf)

    row_max = fold(lambda s, _: s, jnp.maximum, jnp.max, -jnp.inf)
    row_min = fold(lambda s, _: jnp.where(s > -jnp.inf, s, jnp.inf), jnp.minimum, jnp.min, jnp.inf)

    def bisect(_, lh):
        lo, hi = lh
        mid = lo + 0.5 * (hi - lo)
        ge = count_ge(mid) >= k_sel
        return jnp.where(ge, mid, lo), jnp.where(ge, hi, mid)

    def settle(lo):
        thr = min_ge(lo)
        n_gt = count_gt(thr)
        return thr, n_gt, jnp.max(jnp.where(n_gt >= k_sel, 1, 0))

    hi0 = row_max + jnp.abs(row_max) * 1e-6 + 1e-30
    lo, hi = lax.fori_loop(0, 20, bisect, (row_min, hi0))
    thr, n_gt, bad = settle(lo)

    def more(state):
        lo, hi, _, _, _, it = state
        lo, hi = lax.fori_loop(0, 4, bisect, (lo, hi))
        thr, n_gt, bad = settle(lo)
        return lo, hi, thr, n_gt, bad, it + 1

    _, _, thr, n_gt, _, _ = lax.while_loop(lambda st: (st[4] > 0) & (st[5] < 64), more,
                                           (lo, hi, thr, n_gt, bad, 0))

    n_ge = count_ge(thr)
    need = k_sel - n_gt
    excess = n_ge > k_sel

    def key_index(s, base):
        return base + lax.broadcasted_iota(jnp.int32, s.shape, 0)

    def cut_search(_):
        def count_tied_upto(j):
            return fold(lambda s, base: jnp.where((s == thr) & (key_index(s, base) <= j), 1.0, 0.0),
                        jnp.add, jnp.sum, 0.0)

        def step(_, jj):
            jlo, jhi = jj
            mid = (jlo + jhi) >> 1
            ok = count_tied_upto(mid) >= need
            return jnp.where(ok, jlo, mid), jnp.where(ok, mid, jhi)

        steps = int(math.ceil(math.log2(l_len))) + 1
        _, jhi = lax.fori_loop(0, steps, step,
                               (jnp.full((1, tq), -1, jnp.int32), jnp.full((1, tq), l_len - 1, jnp.int32)))
        return jnp.where(excess, jhi, l_len)

    cut = lax.cond(jnp.max(jnp.where(excess, 1, 0)) > 0, cut_search,
                   lambda _: jnp.full((1, tq), l_len, jnp.int32), 0)

    def bias_of(s, base):
        return jnp.where((s > thr) | ((s == thr) & (key_index(s, base) <= cut)), 0.0, NEG_INF)

    qs = []
    for pair in range(H_B // 2):
        blk = q_ref[:, LANES * pair:LANES * (pair + 1)]
        qs += [blk * sel_lo, blk * sel_hi]

    def attend(r0, n, bias_t, carries):
        bias = jnp.transpose(bias_t)
        kk, vv = kk_scr[pl.ds(r0, n), :], vv_scr[pl.ds(r0, n), :]
        return tuple(_softmax_step(_dot_nt(qs[h], kk) + bias, vv, carries[h]) for h in range(H_B))

    carries = lax.fori_loop(
        0, n_full, lambda kb, c: attend(pl.multiple_of(kb * tk, tk), tk, bias_of(sc_scr[kb], kb * tk), c),
        tuple(_softmax_init(tq) for _ in range(H_B)))
    carries = attend(d0, tq, bias_of(scd_scr[...], d0), carries)
    outs = [acc / l for _, l, acc in carries]
    for pair in range(H_B // 2):
        o_ref[:, LANES * pair:LANES * (pair + 1)] = jnp.where(low, outs[2 * pair], outs[2 * pair + 1]).astype(BF16)


def _dsa(q, qidx, wi, rows, past, layer, tq, tk):
    b, t, _ = q.shape
    p_len = 0 if past is None else past.shape[2]
    l_len = p_len + t
    n_sel = min(TOPK_MAX, l_len // 4)
    n_blocks = max((l_len - tq) // tk, 1)
    args = [q, qidx, wi, rows] + ([past] if past is not None else [])
    transposed = tq % LANES == 0
    body = _dsat_kernel if transposed else _dsa_kernel
    return pl.pallas_call(
        functools.partial(body, p_len=p_len, t_len=t, tq=tq, tk=tk, n_sel=n_sel),
        grid=(b, t // tq),
        in_specs=_attn_specs(tq, [256, 512, LANES], rows, past, layer),
        out_specs=_out_spec(tq),
        out_shape=jax.ShapeDtypeStruct((b, t, 256), BF16),
        scratch_shapes=[pltpu.VMEM((l_len, LANES), BF16)] * 3
                       + [pltpu.VMEM((n_blocks, tk, tq) if transposed else (n_blocks, tq, tk), F32),
                          pltpu.VMEM((tq, tq), F32)],
        compiler_params=_params("parallel", "arbitrary"),
        name="dsa",
    )(*args)


def _gather_cols(w, src, scale):
    src = [int(s) for s in src]
    scale = [float(c) if s >= 0 else 0.0 for s, c in zip(src, scale)]
    pieces, i = [], 0
    while i < len(src):
        j = i + 1
        while j < len(src) and scale[j] == scale[i] and (src[j] == src[j - 1] + 1 if src[i] >= 0 else src[j] < 0):
            j += 1
        if src[i] < 0:
            pieces.append(jnp.zeros((w.shape[0], j - i), w.dtype))
        else:
            run = w[:, src[i]:src[i] + j - i]
            pieces.append(run if scale[i] == 1.0 else run * scale[i])
        i = j
    return jnp.concatenate(pieces, axis=1)


class _Cols:
    def __init__(self):
        self.src, self.scale, self.rope = [], [], []

    def plain(self, start, n, scale=1.0):
        for i in range(n):
            self._add(start + i, scale, None)

    def pad(self, n):
        for _ in range(n):
            self._add(-1, 0.0, None)

    def roped(self, start, d, scale=1.0):
        half = d // 2
        for i in range(d):
            self._add(start + i, scale, (d, i % half, -1.0 if i < half else 1.0))

    def _add(self, src, scale, rope):
        self.src.append(src)
        self.scale.append(scale)
        self.rope.append(rope)

    def weights(self, w):
        return _gather_cols(w, self.src, self.scale).astype(BF16)

    def tables(self, pos):
        cs = {}
        for d in sorted({r[0] for r in self.rope if r is not None}):
            freqs = ROPE_THETA ** (-jnp.arange(d // 2, dtype=F32) * 2.0 / d)
            ang = pos.astype(F32)[:, None] * freqs[None, :]
            cs[d] = (jnp.cos(ang), jnp.sin(ang))
        cos, sin, i, n = [], [], 0, pos.shape[0]
        while i < len(self.rope):
            r, j = self.rope[i], i + 1
            if r is None:
                while j < len(self.rope) and self.rope[j] is None:
                    j += 1
                cos.append(jnp.ones((n, j - i), F32))
                sin.append(jnp.zeros((n, j - i), F32))
            else:
                while j < len(self.rope) and self.rope[j] == (r[0], r[1] + j - i, r[2]):
                    j += 1
                cos.append(cs[r[0]][0][:, r[1]:r[1] + j - i])
                sin.append(r[2] * cs[r[0]][1][:, r[1]:r[1] + j - i])
            i = j
        return jnp.concatenate(cos, axis=1), jnp.concatenate(sin, axis=1)


def _plans():
    p = _Cols()
    p.plain(O_CQ, Q_LORA)
    p.plain(O_CKV, KV_LORA)
    for h in range(H_C):
        p.plain(O_KC + DH_C * h, DH_C)
        p.plain(O_VC + DH_C * h, DH_C)
    for h in range(H_C):
        p.plain(O_QC + DH_C * h, DH_C, DH_C ** -0.5)
        p.pad(LANES - DH_C)
    p.plain(O_WIB, H_IDX)
    p.pad(LANES - H_IDX)
    assert len(p.src) == P_END

    r = _Cols()
    r.roped(O_KB, DH_B)
    r.plain(O_VB, DH_B)
    r.roped(O_KIB, D_IDX)
    r.pad(R_DIFF - (2 * DH_B + D_IDX))
    for h in range(H_D):
        r.roped(O_KD + 2 * DQK_D * h, DQK_D)
        r.roped(O_KD + 2 * DQK_D * h + DQK_D, DQK_D)
        r.plain(O_VD + DV_D * h, DV_D)
    r.roped(O_KR, ROPE_A)
    r.pad(LANES - ROPE_A)
    for h in range(H_B):
        r.roped(O_QB + DH_B * h, DH_B, DH_B ** -0.5)
    for h in range(H_IDX):
        r.roped(O_QIB + D_IDX * h, D_IDX)
    for h in range(H_D):
        r.roped(O_QD + 2 * DQK_D * h, DQK_D)
        r.roped(O_QD + 2 * DQK_D * h + DQK_D, DQK_D)
        r.pad(LANES - 2 * DQK_D)
    assert len(r.src) == R_END

    q = _Cols()
    for h in range(H_A):
        q.plain((NOPE_A + ROPE_A) * h, NOPE_A)
        q.roped((NOPE_A + ROPE_A) * h + NOPE_A, ROPE_A)
        q.pad(LANES - NOPE_A - ROPE_A)
    return p, r, q


def _layer_weights(l, prm, plans):
    p, r, q = plans
    w_in = prm["w_in"][l]
    wkvb = prm["mla_w_kvb"][l]
    k_src, v_src = [], []
    for h in range(H_A):
        k_src += list(range((NOPE_A + V_A) * h, (NOPE_A + V_A) * h + NOPE_A)) + [-1] * (LANES - NOPE_A)
        v_src += list(range((NOPE_A + V_A) * h + NOPE_A, (NOPE_A + V_A) * (h + 1)))
    place = np.zeros((ROPE_A, H_A * LANES), np.float32)
    for h in range(H_A):
        place[np.arange(ROPE_A), LANES * h + NOPE_A + np.arange(ROPE_A)] = 1.0
    dn = jnp.concatenate([jnp.zeros((2 * DQK_D,), F32), prm["diff_norm"][l]])[None, :]

    def ff(tag):
        return (prm[f"norm_{tag}"][l][None, :], prm[f"w_{tag}_gate"][l].astype(BF16),
                prm[f"w_{tag}_up"][l].astype(BF16), prm[f"w_{tag}_down"][l].astype(BF16))

    return {
        "ff1": ff("ff1"), "ff2": ff("ff2"),
        "nm": prm["norm_mix"][l][None, :],
        "wp": p.weights(w_in), "wr": r.weights(w_in),
        "qn": prm["mla_q_norm"][l][None, :], "wqb": q.weights(prm["mla_w_qb"][l]),
        "kvn": prm["mla_kv_norm"][l][None, :],
        "wk": _gather_cols(wkvb, k_src, np.ones(len(k_src))).astype(BF16),
        "pk": jnp.asarray(place).astype(BF16),
        "wv": _gather_cols(wkvb, v_src, np.ones(len(v_src))).astype(BF16),
        "lam": prm["diff_lambda"][l], "dn": dn,
        "wo": prm["w_out"][l].astype(BF16),
    }


def _trunk(x, caches, pos, layers, plans, fnorm, tq, tk, proj_fold):
    b, t, _ = x.shape
    depth = len(layers)
    tm = min(512, b * t)
    _, r_plan, q_plan = plans
    pb, pt = (b // proj_fold, t * proj_fold)
    pos_p = jnp.tile(pos, proj_fold)
    tabs = r_plan.tables(pos_p) + q_plan.tables(pos_p)
    tq_p = min(256, pt)
    xf = x.reshape(b * t, D_MODEL)
    stacked = [jnp.zeros((depth, pb, pt * r, w), F32)
               for r, w in ((1, KV_LORA + ROPE_A), (1, 2 * DH_B + D_IDX), (H_C, 2 * DH_C), (H_D, 2 * DQK_D + DV_D))]
    past = [None] * 4 if caches is None else caches
    for l, lw in enumerate(layers):
        lam_init = 0.8 - 0.6 * math.exp(-0.3 * l)
        xf = _ffn(xf, None, None, *lw["ff1"], None, tm)
        outs = _proj(xf.reshape(pb, pt, D_MODEL), lw, tabs, stacked, l, tq_p)
        stacked = list(outs[:4])
        r_sb, r_diff, q_mla, q_dsa, q_idx, w_idx, q_sb, q_diff = [o.reshape(b, t, o.shape[-1]) for o in outs[4:]]
        r_mla, r_dsa = (a.reshape(depth, b, t, a.shape[-1]) for a in stacked[:2])
        mix = (
            _mla(q_mla, r_mla, past[0], l, lw, tq, tk),
            _dsa(q_dsa, q_idx, w_idx, r_dsa, past[1], l, tq, tk),
            _sb(q_sb, r_sb, past[2], l, tq, tk),
            _diff(q_diff, r_diff, past[3], l, lw, lam_init, tq, tk),
        )
        mix = [m.reshape(b * t, 256) for m in mix]
        xf = _ffn(xf, mix, lw["wo"], *lw["ff2"], fnorm if l == depth - 1 else None, tm)
    y = xf.reshape(b, t, D_MODEL)
    return (y, stacked[0].reshape(depth, b, t, -1), stacked[1].reshape(depth, b, t, -1),
            stacked[2].reshape(depth, b, t, H_C, 2 * DH_C),
            stacked[3].reshape(depth, b, t, H_D, 2 * DQK_D + DV_D))


def kernel(x_prompt, x_sample, cache_mla, cache_dsa, cache_sb, cache_diff, norm_ff1, w_ff1_gate, w_ff1_up, w_ff1_down, norm_mix, w_in, mla_q_norm, mla_w_qb, mla_kv_norm, mla_w_kvb, diff_lambda, diff_norm, w_out, norm_ff2, w_ff2_gate, w_ff2_up, w_ff2_down, final_norm):
    prm = dict(norm_ff1=norm_ff1, w_ff1_gate=w_ff1_gate, w_ff1_up=w_ff1_up, w_ff1_down=w_ff1_down,
               norm_mix=norm_mix, w_in=w_in, mla_q_norm=mla_q_norm, mla_w_qb=mla_w_qb,
               mla_kv_norm=mla_kv_norm, mla_w_kvb=mla_w_kvb, diff_lambda=diff_lambda, diff_norm=diff_norm,
               w_out=w_out, norm_ff2=norm_ff2, w_ff2_gate=w_ff2_gate, w_ff2_up=w_ff2_up, w_ff2_down=w_ff2_down)
    depth = w_in.shape[0]
    plans = _plans()
    layers = [_layer_weights(l, prm, plans) for l in range(depth)]
    fnorm = final_norm[None, :]

    t_p = x_prompt.shape[1]
    assert t_p % CHUNK == 0
    tq_p = min(512, t_p)
    out_p = _trunk(x_prompt, None, jnp.arange(t_p, dtype=jnp.int32), layers, plans, fnorm, tq_p, tq_p, 1)


    b_s, t_s, _ = x_sample.shape
    past = cache_mla.shape[2]
    assert past % CHUNK == 0 and t_s <= CHUNK, "new sample frames must sit in one chunk after whole cached chunks"
    tk_s = math.gcd(past, 512)
    fold_heads = lambda c: c.reshape(c.shape[:2] + (c.shape[2] * c.shape[3], c.shape[4]))
    caches = (cache_mla, cache_dsa, fold_heads(cache_sb), fold_heads(cache_diff))
    fold = math.gcd(b_s, max(1, 256 // t_s))
    out_s = _trunk(x_sample, caches, past + jnp.arange(t_s, dtype=jnp.int32), layers, plans, fnorm, t_s, tk_s, fold)

    return (out_p[0], out_s[0]) + tuple(out_p[1:]) + tuple(out_s[1:])
```

```python
import functools
import math

import numpy as np
import jax
import jax.numpy as jnp
from jax import lax
from jax.experimental import pallas as pl
from jax.experimental.pallas import tpu as pltpu

F32 = jnp.float32
BF16 = jnp.bfloat16

D_MODEL = 1024
CHUNK = 64
ROPE_THETA = 10000.0
EPS = 1e-6
NEG_INF = -1e30
HEAD_DIM = D_MODEL // 16
D_FF = 11 * D_MODEL // 4
H_A = 4
Q_LORA = 4 * HEAD_DIM
KV_LORA = 2 * HEAD_DIM
NOPE_A = HEAD_DIM
ROPE_A = HEAD_DIM // 2
V_A = HEAD_DIM
H_B = 4
DH_B = HEAD_DIM
H_IDX = 8
D_IDX = HEAD_DIM
TOPK_MAX = 256
H_C = 4
DH_C = HEAD_DIM
H_D = 4
DQK_D = HEAD_DIM // 2
DV_D = HEAD_DIM
COL_SIZES = (Q_LORA, KV_LORA, ROPE_A,
             H_B * DH_B, DH_B, DH_B, H_IDX * D_IDX, D_IDX, H_IDX,
             H_C * DH_C, H_C * DH_C, H_C * DH_C,
             H_D * 2 * DQK_D, H_D * 2 * DQK_D, H_D * DV_D)
(O_CQ, O_CKV, O_KR, O_QB, O_KB, O_VB, O_QIB, O_KIB, O_WIB,
 O_QC, O_KC, O_VC, O_QD, O_KD, O_VD) = np.concatenate([[0], np.cumsum(COL_SIZES)[:-1]]).tolist()

LANES = 128
VMEM_LIMIT = 56 * 1024 * 1024
MXU_TILE = 256
FF_CHUNK = MXU_TILE
ROW_CHUNK = 512
FFN_ROWS = 512
PROJ_ROWS = 256
ATTN_ROWS = 512
MIX_W = H_A * V_A
SLOT_W = H_A * LANES
assert MIX_W == H_B * DH_B == H_C * DH_C == H_D * DV_D and H_A == H_B == H_C == H_D

P_CQ, P_CKV, P_SB, P_QSB, P_WI, P_END = 0, 256, 384, 896, 1408, 1536
R_DSA, R_DIFF, R_KR, R_QDSA, R_QI, R_QDIFF, R_END = 0, 256, 768, 896, 1152, 1664, 2176


def _dot(a, b):
    return jnp.dot(a, b, preferred_element_type=F32)


def _dot_nt(a, b):
    return lax.dot_general(a, b, (((1,), (1,)), ((), ())), preferred_element_type=F32)


def _rms(x, g):
    return x * lax.rsqrt(jnp.mean(x * x, axis=-1, keepdims=True) + EPS) * g


def _const_spec(shape):
    n = len(shape)
    return pl.BlockSpec(shape, lambda *_: (0,) * n, pipeline_mode=pl.Buffered(1))


def _params(*sem):
    return pltpu.CompilerParams(dimension_semantics=sem, vmem_limit_bytes=VMEM_LIMIT)


def _ffn_kernel(*refs, has_mix, has_final):
    it = iter(refs)
    x_ref = next(it)
    if has_mix:
        o_refs = [next(it) for _ in range(4)]
        wo_ref = next(it)
    g_ref, wg_ref, wu_ref, wd_ref = next(it), next(it), next(it), next(it)
    fn_ref = next(it) if has_final else None
    out_ref = next(it)

    x = x_ref[...]
    if has_mix:
        for i, o_ref in enumerate(o_refs):
            x = x + _dot(o_ref[...], wo_ref[MIX_W * i:MIX_W * (i + 1), :])
    xn = _rms(x, g_ref[...]).astype(BF16)
    acc = jnp.zeros_like(x)
    for c0 in range(0, D_FF, FF_CHUNK):
        gate = _dot(xn, wg_ref[:, c0:c0 + FF_CHUNK])
        up = _dot(xn, wu_ref[:, c0:c0 + FF_CHUNK])
        hid = (gate * jax.nn.sigmoid(gate) * up).astype(BF16)
        acc = acc + _dot(hid, wd_ref[c0:c0 + FF_CHUNK, :])
    y = x + 0.5 * acc
    if has_final:
        y = _rms(y, fn_ref[...])
    out_ref[...] = y


def _ffn(x, mix, wo, g, wg, wu, wd, fnorm, tm):
    n = x.shape[0]
    has_mix, has_final = mix is not None, fnorm is not None
    row = lambda w: pl.BlockSpec((tm, w), lambda i: (i, 0))
    args, specs = [x], [row(D_MODEL)]
    if has_mix:
        args += list(mix) + [wo]
        specs += [row(MIX_W)] * 4 + [_const_spec(wo.shape)]
    args += [g, wg, wu, wd]
    specs += [_const_spec(g.shape), _const_spec(wg.shape), _const_spec(wu.shape), _const_spec(wd.shape)]
    if has_final:
        args.append(fnorm)
        specs.append(_const_spec(fnorm.shape))
    return pl.pallas_call(
        functools.partial(_ffn_kernel, has_mix=has_mix, has_final=has_final),
        grid=(n // tm,),
        in_specs=specs,
        out_specs=row(D_MODEL),
        out_shape=jax.ShapeDtypeStruct((n, D_MODEL), F32),
        compiler_params=_params("parallel"),
        name="ffn",
    )(*args)


def _rope(h, cos, sin_signed, d):
    w = h.shape[1]
    first = lax.broadcasted_iota(jnp.int32, h.shape, 1) % d < d // 2
    partner = jnp.where(first, pltpu.roll(h, w - d // 2, axis=1), pltpu.roll(h, d // 2, axis=1))
    return h * cos + partner * sin_signed


def _proj_kernel(x_ref, g_ref, wp_ref, wr_ref, cr_ref, sr_ref,
                 qn_ref, wqb_ref, cq_ref, sq_ref, kvn_ref,
                 _amla, _adsa, _asb, _adiff,
                 rmla_ref, rdsa_ref, rsbh_ref, rdiffh_ref, rsb_ref, rdiff_ref,
                 qmla_ref, qdsa_ref, qi_ref, wi_ref, qsb_ref, qdiff_ref):
    xn = _rms(x_ref[...], g_ref[...]).astype(BF16)
    tq = x_ref.shape[0]

    def plain(a, b):
        return _dot(xn, wp_ref[:, a:b])

    def roped(a, b, d):
        return _rope(_dot(xn, wr_ref[:, a:b]), cr_ref[:, a:b], sr_ref[:, a:b], d)

    cqn = _rms(plain(P_CQ, P_CKV), qn_ref[...]).astype(BF16)
    qmla_ref[...] = _rope(_dot(cqn, wqb_ref[...]), cq_ref[...], sq_ref[...], ROPE_A).astype(BF16)
    rmla_ref[:, :KV_LORA] = _rms(plain(P_CKV, P_SB), kvn_ref[...])
    rmla_ref[:, KV_LORA:] = roped(R_KR, R_QDSA, ROPE_A)[:, :ROPE_A]
    rdsa_ref[...] = roped(R_DSA, R_DIFF, DH_B)[:, :2 * DH_B + D_IDX]
    qdsa_ref[...] = roped(R_QDSA, R_QI, DH_B).astype(BF16)
    qi_ref[...] = roped(R_QI, R_QDIFF, D_IDX).astype(BF16)
    wi_ref[...] = plain(P_WI, P_END) * (H_IDX ** -0.5)
    rsb = plain(P_SB, P_QSB)
    rsb_ref[...] = rsb
    for h in range(H_C):
        rsbh_ref[pl.ds(h, tq, stride=H_C), :] = rsb[:, LANES * h:LANES * (h + 1)]
    qsb_ref[...] = plain(P_QSB, P_WI).astype(BF16)
    rdiff = roped(R_DIFF, R_KR, DQK_D)
    rdiff_ref[...] = rdiff
    for h in range(H_D):
        rdiffh_ref[pl.ds(h, tq, stride=H_D), :] = rdiff[:, LANES * h:LANES * (h + 1)]
    qdiff_ref[...] = roped(R_QDIFF, R_END, DQK_D).astype(BF16)


def _proj(x, lw, tabs, stacked, layer, tq):
    b, t, _ = x.shape
    cr, sr, cq, sq = tabs
    tok = lambda w: pl.BlockSpec((None, tq, w), lambda bi, i: (bi, i, 0))
    lay = lambda r, w: pl.BlockSpec((None, None, r, w), lambda bi, i: (layer, bi, i, 0))
    tab = lambda w: pl.BlockSpec((tq, w), lambda bi, i: (i, 0))
    consts = [lw["nm"], lw["wp"], lw["wr"]]
    consts2 = [lw["qn"], lw["wqb"]]
    in_specs = ([tok(D_MODEL)] + [_const_spec(a.shape) for a in consts] + [tab(R_END), tab(R_END)]
                + [_const_spec(a.shape) for a in consts2] + [tab(SLOT_W), tab(SLOT_W), _const_spec(lw["kvn"].shape)]
                + [pl.BlockSpec(memory_space=pl.ANY)] * 4)
    widths = [(SLOT_W, F32), (SLOT_W, F32), (SLOT_W, BF16), (MIX_W, BF16), (H_IDX * D_IDX, BF16), (LANES, F32),
              (SLOT_W, BF16), (SLOT_W, BF16)]
    n_in = len(in_specs)
    return pl.pallas_call(
        _proj_kernel,
        grid=(b, t // tq),
        in_specs=in_specs,
        out_specs=[lay(tq, 160), lay(tq, 192), lay(tq * H_C, LANES), lay(tq * H_D, LANES)]
                  + [tok(w) for w, _ in widths],
        out_shape=[jax.ShapeDtypeStruct(a.shape, a.dtype) for a in stacked]
                  + [jax.ShapeDtypeStruct((b, t, w), dt) for w, dt in widths],
        input_output_aliases={n_in - 4 + k: k for k in range(4)},
        compiler_params=_params("parallel", "parallel"),
        name="proj",
    )(x, *consts, cr, sr, *consts2, cq, sq, lw["kvn"], *stacked)


def _chunk_mask(tq):
    ri = lax.broadcasted_iota(jnp.int32, (tq, tq), 0)
    ci = lax.broadcasted_iota(jnp.int32, (tq, tq), 1)
    return (ci // CHUNK) <= (ri // CHUNK)


LOG2E = math.log2(math.e)


def _softmax_step(s, v, carry, scale=1.0):
    m, l, acc = carry
    c = scale * LOG2E
    m_new = jnp.maximum(m, jnp.max(s, axis=-1, keepdims=True))
    alpha = jnp.exp2((m - m_new) * c)
    p = jnp.exp2((s - m_new) * c)
    l = alpha * l + jnp.sum(p, axis=-1, keepdims=True)
    acc = alpha * acc + _dot(p.astype(BF16), v)
    return m_new, l, acc


def _softmax_init(tq):
    return (jnp.full((tq, 1), NEG_INF, F32), jnp.zeros((tq, 1), F32), jnp.zeros((tq, LANES), F32))


def _lane_lt(tq, n):
    return lax.broadcasted_iota(jnp.int32, (tq, LANES), 1) < n


def _attn_specs(tq, q_widths, rows, past, layer):
    def whole(a):
        if a.ndim == 3:
            return pl.BlockSpec((None,) + a.shape[1:], lambda bi, i: (bi, 0, 0))
        return pl.BlockSpec((None, None) + a.shape[2:], lambda bi, i: (layer, bi, 0, 0))
    specs = [pl.BlockSpec((None, tq, w), lambda bi, i: (bi, i, 0)) for w in q_widths]
    return specs + [whole(rows)] + ([whole(past)] if past is not None else [])


def _out_spec(tq):
    return pl.BlockSpec((None, tq, MIX_W), lambda bi, i: (bi, i, 0))


def _mla_kernel(*refs, p_len, t_len, tq, tk):
    if p_len:
        q_ref, new_ref, past_ref, wk_ref, pk_ref, wv_ref, o_ref, k_scr, v_scr = refs
    else:
        q_ref, new_ref, wk_ref, pk_ref, wv_ref, o_ref, k_scr, v_scr = refs
        past_ref = None
    qi = pl.program_id(1)
    scale = (NOPE_A + ROPE_A) ** -0.5

    @pl.when(qi == 0)
    def _fill():
        def fill(src_ref, dst, n):
            for r0 in range(0, n, ROW_CHUNK):
                rc = min(ROW_CHUNK, n - r0)
                lat = src_ref[r0:r0 + rc, :KV_LORA].astype(BF16)
                rope = src_ref[r0:r0 + rc, KV_LORA:].astype(BF16)
                k_scr[dst + r0:dst + r0 + rc, :] = (_dot(lat, wk_ref[...]) + _dot(rope, pk_ref[...])).astype(BF16)
                v_scr[dst + r0:dst + r0 + rc, :] = _dot(lat, wv_ref[...]).astype(BF16)
        if p_len:
            fill(past_ref, 0, p_len)
        fill(new_ref, p_len, t_len)

    d0 = pl.multiple_of(p_len + qi * tq, tq)
    n_full = (p_len + qi * tq) // tk
    dmask = _chunk_mask(tq)
    low = _lane_lt(tq, V_A)
    qs = [q_ref[:, LANES * h:LANES * (h + 1)] for h in range(H_A)]

    def step(r0, n, mask, carries):
        out = []
        for h in range(H_A):
            s = _dot_nt(qs[h], k_scr[pl.ds(r0, n), LANES * h:LANES * (h + 1)])
            if mask is not None:
                s = jnp.where(mask, s, NEG_INF)
            out.append(_softmax_step(s, v_scr[pl.ds(r0, n), LANES * (h // 2):LANES * (h // 2 + 1)], carries[h],
                                     scale))
        return tuple(out)

    carries = lax.fori_loop(0, n_full, lambda kb, c: step(pl.multiple_of(kb * tk, tk), tk, None, c),
                            tuple(_softmax_init(tq) for _ in range(H_A)))
    carries = step(d0, tq, dmask, carries)
    outs = [acc / l for _, l, acc in carries]
    for pair in range(H_A // 2):
        o_ref[:, LANES * pair:LANES * (pair + 1)] = jnp.where(low, outs[2 * pair], outs[2 * pair + 1]).astype(BF16)


def _mla(q, rows, past, layer, lw, tq, tk):
    b, t, _ = q.shape
    p_len = 0 if past is None else past.shape[2]
    consts = [lw["wk"], lw["pk"], lw["wv"]]
    args = [q, rows] + ([past] if past is not None else []) + consts
    return pl.pallas_call(
        functools.partial(_mla_kernel, p_len=p_len, t_len=t, tq=tq, tk=tk),
        grid=(b, t // tq),
        in_specs=_attn_specs(tq, [SLOT_W], rows, past, layer) + [_const_spec(a.shape) for a in consts],
        out_specs=_out_spec(tq),
        out_shape=jax.ShapeDtypeStruct((b, t, MIX_W), BF16),
        scratch_shapes=[pltpu.VMEM((p_len + t, SLOT_W), BF16), pltpu.VMEM((p_len + t, MIX_W), BF16)],
        compiler_params=_params("parallel", "arbitrary"),
        name="mla",
    )(*args)


def _diff_kernel(*refs, p_len, tq, tk, lam_init):
    if p_len:
        q_ref, new_ref, past_ref, lam_ref, dn_ref, o_ref = refs
    else:
        q_ref, new_ref, lam_ref, dn_ref, o_ref = refs
        past_ref = None
    qi = pl.program_id(1)
    scale = DQK_D ** -0.5
    lam = lam_ref[...]
    lam_full = (jnp.exp(jnp.sum(lam[0:1] * lam[1:2], axis=-1, keepdims=True))
                - jnp.exp(jnp.sum(lam[2:3] * lam[3:4], axis=-1, keepdims=True)) + lam_init)
    dmask = _chunk_mask(tq)
    lane = lax.broadcasted_iota(jnp.int32, (tq, LANES), 1)
    sel1 = jnp.where(lane < DQK_D, 1.0, 0.0).astype(BF16)
    sel2 = jnp.where((lane >= DQK_D) & (lane < 2 * DQK_D), 1.0, 0.0).astype(BF16)
    is_v = lane >= 2 * DQK_D
    n_past = p_len // tk
    n_new = (qi * tq) // tk

    q12 = []
    for h in range(H_D):
        qh = q_ref[:, LANES * h:LANES * (h + 1)]
        q12.append(jnp.concatenate([qh * sel1, qh * sel2], axis=0))
    dmask2 = jnp.concatenate([dmask, dmask], axis=0)

    def step(load, masked, carries):
        out = []
        for h in range(H_D):
            kv = load(h)
            s = _dot_nt(q12[h], kv)
            if masked:
                s = jnp.where(dmask2, s, NEG_INF)
            out.append(_softmax_step(s, kv, carries[h], scale))
        return tuple(out)

    def new_rows(r0, n):
        return lambda h: new_ref[pl.ds(r0, n), LANES * h:LANES * (h + 1)].astype(BF16)

    def past_rows(kb):
        return lambda h: past_ref[pl.ds(H_D * kb * tk + h, tk, stride=H_D), :].astype(BF16)

    carries = tuple(_softmax_init(2 * tq) for _ in range(H_D))
    for kb in range(n_past):
        carries = step(past_rows(kb), False, carries)
    carries = lax.fori_loop(0, n_new, lambda kb, c: step(new_rows(pl.multiple_of(kb * tk, tk), tk), False, c),
                            carries)
    carries = step(new_rows(pl.multiple_of(qi * tq, tq), tq), True, carries)
    outs = []
    for h in range(H_D):
        _, l, acc = carries[h]
        p = acc / l
        o = p[:tq] - lam_full * p[tq:]
        ms = jnp.sum(jnp.where(is_v, o * o, 0.0), axis=-1, keepdims=True) * (1.0 / DV_D)
        outs.append(o * lax.rsqrt(ms + EPS) * dn_ref[...] * (1.0 - lam_init))
    for pair in range(H_D // 2):
        o_ref[:, LANES * pair:LANES * (pair + 1)] = jnp.where(
            is_v, outs[2 * pair + 1], pltpu.roll(outs[2 * pair], 2 * DQK_D, axis=1)).astype(BF16)


def _diff(q, rows, past, layer, lw, lam_init, tq, tk):
    b, t, _ = q.shape
    p_len = 0 if past is None else past.shape[2] // H_D
    consts = [lw["lam"], lw["dn"]]
    args = [q, rows] + ([past] if past is not None else []) + consts
    return pl.pallas_call(
        functools.partial(_diff_kernel, p_len=p_len, tq=tq, tk=tk, lam_init=lam_init),
        grid=(b, t // tq),
        in_specs=_attn_specs(tq, [SLOT_W], rows, past, layer) + [_const_spec(a.shape) for a in consts],
        out_specs=_out_spec(tq),
        out_shape=jax.ShapeDtypeStruct((b, t, MIX_W), BF16),
        compiler_params=_params("parallel", "parallel"),
        name="diff",
    )(*args)


def _sb_kernel(*refs, p_len, tq, tk):
    if p_len:
        q_ref, new_ref, past_ref, o_ref = refs
    else:
        q_ref, new_ref, o_ref = refs
        past_ref = None
    qi = pl.program_id(1)
    n_past = p_len // tk
    n_new = (qi * tq) // tk

    def later(n):
        n = min(n, MXU_TILE)
        ri = lax.broadcasted_iota(jnp.int32, (n, n), 0)
        ci = lax.broadcasted_iota(jnp.int32, (n, n), 1)
        return jnp.where(ri > ci, 1.0, 0.0).astype(BF16)

    def later_sums(x, tri):
        w = tri.shape[0]
        pieces, after = [], None
        for j in reversed(range(x.shape[1] // w)):
            xj = x[:, w * j:w * (j + 1)]
            hi = xj.astype(BF16)
            lo = (xj - hi.astype(F32)).astype(BF16)
            inside = _dot(hi, tri) + _dot(lo, tri)
            total = inside[:, :1] + xj[:, :1]
            if after is not None:
                inside, total = inside + after, total + after
            pieces.append(inside)
            after = total
        return (pieces[0] if len(pieces) == 1 else jnp.concatenate(pieces[::-1], axis=1)), after

    later_k = later(tk)
    later_q = later_k if min(tq, MXU_TILE) == min(tk, MXU_TILE) else later(tq)
    ri = lax.broadcasted_iota(jnp.int32, (tq, tq), 0)
    ci = lax.broadcasted_iota(jnp.int32, (tq, tq), 1)
    causal = ci < ri
    is_v = lax.broadcasted_iota(jnp.int32, (tq, LANES), 1) >= DH_C

    qs = [q_ref[:, LANES * h:LANES * (h + 1)] for h in range(H_C)]

    groups = [list(range(H_C))] if tq < LANES else [[h] for h in range(H_C)]

    def step(load, mask, tri, carries):
        out = [None] * H_C
        for group in groups:
            kvs, takes, keeps = [], [], []
            for h in group:
                kv = load(h)
                z = _dot_nt(qs[h], kv)
                log_take = jnp.minimum(z, 0.0) - jnp.log(1.0 + jnp.exp2(jnp.abs(z) * -LOG2E))
                log_keep = log_take - z
                if mask is not None:
                    log_keep = jnp.where(mask, log_keep, 0.0)
                kvs.append(kv)
                takes.append(log_take)
                keeps.append(log_keep)
            later_sum, block_sum = later_sums(keeps[0] if len(group) == 1 else jnp.concatenate(keeps, axis=0), tri)
            for i, h in enumerate(group):
                run, acc = carries[h]
                rows = slice(tq * i, tq * (i + 1))
                a = jnp.exp2((takes[i] + (later_sum[rows] + run)) * LOG2E)
                if mask is not None:
                    a = jnp.where(mask, a, 0.0)
                out[h] = (run + block_sum[rows], acc + _dot(a.astype(BF16), kvs[i]))
        return tuple(out)

    def new_rows(r0, n):
        return lambda h: new_ref[pl.ds(r0, n), LANES * h:LANES * (h + 1)].astype(BF16)

    def past_rows(kb):
        return lambda h: past_ref[pl.ds(H_C * kb * tk + h, tk, stride=H_C), :].astype(BF16)

    carries = tuple((jnp.zeros((tq, 1), F32), jnp.zeros((tq, LANES), F32)) for _ in range(H_C))
    carries = step(new_rows(pl.multiple_of(qi * tq, tq), tq), causal, later_q, carries)
    carries = lax.fori_loop(
        0, n_new,
        lambda i, c: step(new_rows(pl.multiple_of((n_new - 1 - i) * tk, tk), tk), None, later_k, c), carries)
    for kb in reversed(range(n_past)):
        carries = step(past_rows(kb), None, later_k, carries)
    for pair in range(H_C // 2):
        o_ref[:, LANES * pair:LANES * (pair + 1)] = jnp.where(
            is_v, carries[2 * pair + 1][1], pltpu.roll(carries[2 * pair][1], DH_C, axis=1)).astype(BF16)


def _sb(q, rows, past, layer, tq, tk):
    b, t, _ = q.shape
    p_len = 0 if past is None else past.shape[2] // H_C
    args = [q, rows] + ([past] if past is not None else [])
    return pl.pallas_call(
        functools.partial(_sb_kernel, p_len=p_len, tq=tq, tk=tk),
        grid=(b, t // tq),
        in_specs=_attn_specs(tq, [SLOT_W], rows, past, layer),
        out_specs=_out_spec(tq),
        out_shape=jax.ShapeDtypeStruct((b, t, MIX_W), BF16),
        compiler_params=_params("parallel", "parallel"),
        name="sb",
    )(*args)


def _dsa_kernel(*refs, p_len, t_len, tq, tk, n_sel):
    if p_len:
        q_ref, qi_ref, wi_ref, new_ref, past_ref, o_ref, kk_scr, vv_scr, ki_scr, sc_scr, scd_scr = refs
    else:
        q_ref, qi_ref, wi_ref, new_ref, o_ref, kk_scr, vv_scr, ki_scr, sc_scr, scd_scr = refs
        past_ref = None
    qi = pl.program_id(1)
    l_len = p_len + t_len
    n_tiles = tk // LANES
    k_sel = float(n_sel)

    @pl.when(qi == 0)
    def _fill():
        r1 = lax.broadcasted_iota(jnp.int32, (LANES, 2 * LANES), 0)
        c1 = lax.broadcasted_iota(jnp.int32, (LANES, 2 * LANES), 1)
        dup_kv = jnp.where((c1 % DH_B == r1 % DH_B) & (c1 // LANES == r1 // DH_B), 1.0, 0.0).astype(BF16)
        r2 = lax.broadcasted_iota(jnp.int32, (D_IDX, LANES), 0)
        c2 = lax.broadcasted_iota(jnp.int32, (D_IDX, LANES), 1)
        dup_ki = jnp.where(c2 % D_IDX == r2, 1.0, 0.0).astype(BF16)

        def fill(src_ref, dst, n):
            for r0 in range(0, n, ROW_CHUNK):
                rc = min(ROW_CHUNK, n - r0)
                kv2 = _dot(src_ref[r0:r0 + rc, :LANES].astype(BF16), dup_kv)
                kk_scr[dst + r0:dst + r0 + rc, :] = kv2[:, :LANES].astype(BF16)
                vv_scr[dst + r0:dst + r0 + rc, :] = kv2[:, LANES:].astype(BF16)
                ki_scr[dst + r0:dst + r0 + rc, :] = _dot(src_ref[r0:r0 + rc, LANES:].astype(BF16), dup_ki).astype(BF16)
        if p_len:
            fill(past_ref, 0, p_len)
        fill(new_ref, p_len, t_len)

    d0 = pl.multiple_of(p_len + qi * tq, tq)
    n_full = (p_len + qi * tq) // tk
    lane = lax.broadcasted_iota(jnp.int32, (tq, LANES), 1)
    low = lane < DH_B
    sel_lo = jnp.where(low, 1.0, 0.0).astype(BF16)
    sel_hi = jnp.where(low, 0.0, 1.0).astype(BF16)
    dmask = _chunk_mask(tq)

    wi = wi_ref[...]
    wcol = [jnp.sum(jnp.where(lane == h, wi, 0.0), axis=-1, keepdims=True) for h in range(H_IDX)]
    q_idx = []
    for pr in range(H_IDX // 2):
        blk = qi_ref[:, LANES * pr:LANES * (pr + 1)]
        q_idx += [blk * sel_lo, blk * sel_hi]
    q_idx = jnp.concatenate(q_idx, axis=0)
    w_all = jnp.concatenate(wcol, axis=0)

    def idx_score(ki2):
        terms = jnp.maximum(_dot_nt(q_idx, ki2), 0.0) * w_all
        sc = terms[:tq]
        for h in range(1, H_IDX):
            sc = sc + terms[tq * h:tq * (h + 1)]
        return sc

    def score_body(kb, _):
        r0 = pl.multiple_of(kb * tk, tk)
        sc_scr[kb] = idx_score(ki_scr[pl.ds(r0, tk), :])
        return 0

    lax.fori_loop(0, n_full, score_body, 0)
    scd_scr[...] = jnp.where(dmask, idx_score(ki_scr[pl.ds(d0, tq), :]), -jnp.inf)

    def fold(tile_fn, diag_fn, comb, reduce, init):
        def body(kb, acc):
            for j in range(n_tiles):
                acc = comb(acc, tile_fn(sc_scr[kb, :, LANES * j:LANES * (j + 1)], kb * tk + LANES * j))
            return acc
        acc = lax.fori_loop(0, n_full, body, jnp.full((tq, LANES), init, F32))
        return comb(reduce(acc, axis=-1, keepdims=True), reduce(diag_fn(scd_scr[...]), axis=-1, keepdims=True))

    def wide(v):
        return jnp.broadcast_to(v, (tq, LANES))

    def count_ge(thr):
        thr_w = wide(thr)
        return fold(lambda s, _: jnp.where(s >= thr_w, 1.0, 0.0), lambda s: jnp.where(s >= thr, 1.0, 0.0),
                    jnp.add, jnp.sum, 0.0)

    def count_gt(thr):
        thr_w = wide(thr)
        return fold(lambda s, _: jnp.where(s > thr_w, 1.0, 0.0), lambda s: jnp.where(s > thr, 1.0, 0.0),
                    jnp.add, jnp.sum, 0.0)

    def min_ge(thr):
        thr_w = wide(thr)
        return fold(lambda s, _: jnp.where(s >= thr_w, s, jnp.inf), lambda s: jnp.where(s >= thr, s, jnp.inf),
                    jnp.minimum, jnp.min, jnp.inf)

    row_max = fold(lambda s, _: s, lambda s: s, jnp.maximum, jnp.max, -jnp.inf)
    row_min = fold(lambda s, _: s, lambda s: jnp.where(s > -jnp.inf, s, jnp.inf), jnp.minimum, jnp.min, jnp.inf)

    def bisect(_, lh):
        lo, hi = lh
        mid = lo + 0.5 * (hi - lo)
        ge = count_ge(mid) >= k_sel
        return jnp.where(ge, mid, lo), jnp.where(ge, hi, mid)

    def settle(lo):
        thr = min_ge(lo)
        n_gt = count_gt(thr)
        return thr, n_gt, jnp.max(jnp.where(n_gt >= k_sel, 1, 0))

    hi0 = row_max + jnp.abs(row_max) * 1e-6 + 1e-30
    lo, hi = lax.fori_loop(0, 20, bisect, (row_min, hi0))
    thr, n_gt, bad = settle(lo)

    def more(state):
        lo, hi, _, _, _, it = state
        lo, hi = lax.fori_loop(0, 4, bisect, (lo, hi))
        thr, n_gt, bad = settle(lo)
        return lo, hi, thr, n_gt, bad, it + 1

    _, _, thr, n_gt, _, _ = lax.while_loop(lambda st: (st[4] > 0) & (st[5] < 64), more,
                                           (lo, hi, thr, n_gt, bad, 0))

    n_ge = count_ge(thr)
    need = k_sel - n_gt
    excess = n_ge > k_sel
    thr_w = wide(thr)

    def cut_search(_):
        def count_tied_upto(j):
            j_w = wide(j)
            def tile(s, base):
                idx = base + lane
                return jnp.where((s == thr_w) & (idx <= j_w), 1.0, 0.0)
            def diag(s):
                idx = d0 + lax.broadcasted_iota(jnp.int32, (tq, tq), 1)
                return jnp.where((s == thr) & (idx <= j), 1.0, 0.0)
            return fold(tile, diag, jnp.add, jnp.sum, 0.0)

        def step(_, jj):
            jlo, jhi = jj
            mid = (jlo + jhi) >> 1
            ok = count_tied_upto(mid) >= need
            return jnp.where(ok, jlo, mid), jnp.where(ok, mid, jhi)

        steps = int(math.ceil(math.log2(l_len))) + 1
        _, jhi = lax.fori_loop(0, steps, step,
                               (jnp.full((tq, 1), -1, jnp.int32), jnp.full((tq, 1), l_len - 1, jnp.int32)))
        return jnp.where(excess, jhi, l_len)

    cut = lax.cond(jnp.max(jnp.where(excess, 1, 0)) > 0, cut_search,
                   lambda _: jnp.full((tq, 1), l_len, jnp.int32), 0)
    cut_w = wide(cut)

    def picked_tile(s, base):
        return (s > thr_w) | ((s == thr_w) & (base + lane <= cut_w))

    idx_d = d0 + lax.broadcasted_iota(jnp.int32, (tq, tq), 1)
    sd = scd_scr[...]
    picked_d = (sd > thr) | ((sd == thr) & (idx_d <= cut))

    qs = []
    for pair in range(H_B // 2):
        blk = q_ref[:, LANES * pair:LANES * (pair + 1)]
        qs += [blk * sel_lo, blk * sel_hi]
    qs = jnp.concatenate(qs, axis=0)

    def attend(r0, n, bias, carry):
        s = _dot_nt(qs, kk_scr[pl.ds(r0, n), :]) + jnp.concatenate([bias] * H_B, axis=0)
        return _softmax_step(s, vv_scr[pl.ds(r0, n), :], carry)

    def body(kb, carry):
        bias = jnp.concatenate(
            [jnp.where(picked_tile(sc_scr[kb, :, LANES * j:LANES * (j + 1)], kb * tk + LANES * j), 0.0, NEG_INF)
             for j in range(n_tiles)], axis=1)
        return attend(pl.multiple_of(kb * tk, tk), tk, bias, carry)

    carry = lax.fori_loop(0, n_full, body, _softmax_init(H_B * tq))
    _, l, acc = attend(d0, tq, jnp.where(picked_d, 0.0, NEG_INF), carry)
    out = acc / l
    for pair in range(H_B // 2):
        even, odd = out[tq * 2 * pair:tq * (2 * pair + 1)], out[tq * (2 * pair + 1):tq * (2 * pair + 2)]
        o_ref[:, LANES * pair:LANES * (pair + 1)] = jnp.where(low, even, odd).astype(BF16)


def _dsat_kernel(*refs, p_len, t_len, tq, tk, n_sel):
    if p_len:
        q_ref, qi_ref, wi_ref, new_ref, past_ref, o_ref, kk_scr, vv_scr, ki_scr, sc_scr, scd_scr = refs
    else:
        q_ref, qi_ref, wi_ref, new_ref, o_ref, kk_scr, vv_scr, ki_scr, sc_scr, scd_scr = refs
        past_ref = None
    qi = pl.program_id(1)
    l_len = p_len + t_len
    k_sel = float(n_sel)
    sub = 8

    @pl.when(qi == 0)
    def _fill():
        r1 = lax.broadcasted_iota(jnp.int32, (LANES, 2 * LANES), 0)
        c1 = lax.broadcasted_iota(jnp.int32, (LANES, 2 * LANES), 1)
        dup_kv = jnp.where((c1 % DH_B == r1 % DH_B) & (c1 // LANES == r1 // DH_B), 1.0, 0.0).astype(BF16)
        r2 = lax.broadcasted_iota(jnp.int32, (D_IDX, LANES), 0)
        c2 = lax.broadcasted_iota(jnp.int32, (D_IDX, LANES), 1)
        dup_ki = jnp.where(c2 % D_IDX == r2, 1.0, 0.0).astype(BF16)

        def fill(src_ref, dst, n):
            for r0 in range(0, n, ROW_CHUNK):
                rc = min(ROW_CHUNK, n - r0)
                kv2 = _dot(src_ref[r0:r0 + rc, :LANES].astype(BF16), dup_kv)
                kk_scr[dst + r0:dst + r0 + rc, :] = kv2[:, :LANES].astype(BF16)
                vv_scr[dst + r0:dst + r0 + rc, :] = kv2[:, LANES:].astype(BF16)
                ki_scr[dst + r0:dst + r0 + rc, :] = _dot(src_ref[r0:r0 + rc, LANES:].astype(BF16), dup_ki).astype(BF16)
        if p_len:
            fill(past_ref, 0, p_len)
        fill(new_ref, p_len, t_len)

    d0 = pl.multiple_of(p_len + qi * tq, tq)
    n_full = (p_len + qi * tq) // tk
    lane = lax.broadcasted_iota(jnp.int32, (tq, LANES), 1)
    low = lane < DH_B
    sel_lo = jnp.where(low, 1.0, 0.0).astype(BF16)
    sel_hi = jnp.where(low, 0.0, 1.0).astype(BF16)
    dmask = (lax.broadcasted_iota(jnp.int32, (tq, tq), 0) // CHUNK) <= (lax.broadcasted_iota(jnp.int32, (tq, tq), 1) // CHUNK)

    w_t = jnp.transpose(wi_ref[...])
    q_idx = []
    for pr in range(H_IDX // 2):
        blk = qi_ref[:, LANES * pr:LANES * (pr + 1)]
        q_idx += [blk * sel_lo, blk * sel_hi]

    def idx_score(ki2):
        sc = None
        for h in range(H_IDX):
            term = jnp.maximum(_dot_nt(ki2, q_idx[h]), 0.0) * w_t[h:h + 1, :]
            sc = term if sc is None else sc + term
        return sc

    def score_body(kb, _):
        sc_scr[kb] = idx_score(ki_scr[pl.ds(pl.multiple_of(kb * tk, tk), tk), :])
        return 0

    lax.fori_loop(0, n_full, score_body, 0)
    scd_scr[...] = jnp.where(dmask, idx_score(ki_scr[pl.ds(d0, tq), :]), -jnp.inf)

    def fold(fn, comb, reduce, init):
        def part(x):
            return reduce(x.reshape(x.shape[0] // sub, sub, tq), axis=0)
        acc = lax.fori_loop(0, n_full, lambda kb, a: comb(a, part(fn(sc_scr[kb], kb * tk))),
                            jnp.full((sub, tq), init, F32))
        acc = comb(acc, part(fn(scd_scr[...], d0)))
        return reduce(acc, axis=0, keepdims=True)

    def count_ge(thr):
        return fold(lambda s, _: jnp.where(s >= thr, 1.0, 0.0), jnp.add, jnp.sum, 0.0)

    def count_gt(thr):
        return fold(lambda s, _: jnp.where(s > thr, 1.0, 0.0), jnp.add, jnp.sum, 0.0)

    def min_ge(thr):
        return fold(lambda s, _: jnp.where(s >= thr, s, jnp.inf), jnp.minimum, jnp.min, jnp.inf)

    row_max = fold(lambda s, _: s, jnp.maximum, jnp.max, -jnp.inf)
    row_min = fold(lambda s, _: jnp.where(s > -jnp.inf, s, jnp.inf), jnp.minimum, jnp.min, jnp.inf)

    def bisect(_, lh):
        lo, hi = lh
        mid = lo + 0.5 * (hi - lo)
        ge = count_ge(mid) >= k_sel
        return jnp.where(ge, mid, lo), jnp.where(ge, hi, mid)

    def settle(lo):
        thr = min_ge(lo)
        n_gt = count_gt(thr)
        return thr, n_gt, jnp.max(jnp.where(n_gt >= k_sel, 1, 0))

    hi0 = row_max + jnp.abs(row_max) * 1e-6 + 1e-30
    lo, hi = lax.fori_loop(0, 20, bisect, (row_min, hi0))
    thr, n_gt, bad = settle(lo)

    def more(state):
        lo, hi, _, _, _, it = state
        lo, hi = lax.fori_loop(0, 4, bisect, (lo, hi))
        thr, n_gt, bad = settle(lo)
        return lo, hi, thr, n_gt, bad, it + 1

    _, _, thr, n_gt, _, _ = lax.while_loop(lambda st: (st[4] > 0) & (st[5] < 64), more,
                                           (lo, hi, thr, n_gt, bad, 0))

    n_ge = count_ge(thr)
    need = k_sel - n_gt
    excess = n_ge > k_sel

    def key_index(s, base):
        return base + lax.broadcasted_iota(jnp.int32, s.shape, 0)

    def cut_search(_):
        def count_tied_upto(j):
            return fold(lambda s, base: jnp.where((s == thr) & (key_index(s, base) <= j), 1.0, 0.0),
                        jnp.add, jnp.sum, 0.0)

        def step(_, jj):
            jlo, jhi = jj
            mid = (jlo + jhi) >> 1
            ok = count_tied_upto(mid) >= need
            return jnp.where(ok, jlo, mid), jnp.where(ok, mid, jhi)

        steps = int(math.ceil(math.log2(l_len))) + 1
        _, jhi = lax.fori_loop(0, steps, step,
                               (jnp.full((1, tq), -1, jnp.int32), jnp.full((1, tq), l_len - 1, jnp.int32)))
        return jnp.where(excess, jhi, l_len)

    cut = lax.cond(jnp.max(jnp.where(excess, 1, 0)) > 0, cut_search,
                   lambda _: jnp.full((1, tq), l_len, jnp.int32), 0)

    def bias_of(s, base):
        return jnp.where((s > thr) | ((s == thr) & (key_index(s, base) <= cut)), 0.0, NEG_INF)

    qs = []
    for pair in range(H_B // 2):
        blk = q_ref[:, LANES * pair:LANES * (pair + 1)]
        qs += [blk * sel_lo, blk * sel_hi]

    def attend(r0, n, bias_t, carries):
        bias = jnp.transpose(bias_t)
        kk, vv = kk_scr[pl.ds(r0, n), :], vv_scr[pl.ds(r0, n), :]
        return tuple(_softmax_step(_dot_nt(qs[h], kk) + bias, vv, carries[h]) for h in range(H_B))

    carries = lax.fori_loop(
        0, n_full, lambda kb, c: attend(pl.multiple_of(kb * tk, tk), tk, bias_of(sc_scr[kb], kb * tk), c),
        tuple(_softmax_init(tq) for _ in range(H_B)))
    carries = attend(d0, tq, bias_of(scd_scr[...], d0), carries)
    outs = [acc / l for _, l, acc in carries]
    for pair in range(H_B // 2):
        o_ref[:, LANES * pair:LANES * (pair + 1)] = jnp.where(low, outs[2 * pair], outs[2 * pair + 1]).astype(BF16)


def _dsa(q, qidx, wi, rows, past, layer, tq, tk):
    b, t, _ = q.shape
    p_len = 0 if past is None else past.shape[2]
    l_len = p_len + t
    n_sel = min(TOPK_MAX, l_len // 4)
    n_blocks = max((l_len - tq) // tk, 1)
    args = [q, qidx, wi, rows] + ([past] if past is not None else [])
    transposed = tq % LANES == 0
    body = _dsat_kernel if transposed else _dsa_kernel
    return pl.pallas_call(
        functools.partial(body, p_len=p_len, t_len=t, tq=tq, tk=tk, n_sel=n_sel),
        grid=(b, t // tq),
        in_specs=_attn_specs(tq, [MIX_W, H_IDX * D_IDX, LANES], rows, past, layer),
        out_specs=_out_spec(tq),
        out_shape=jax.ShapeDtypeStruct((b, t, MIX_W), BF16),
        scratch_shapes=[pltpu.VMEM((l_len, LANES), BF16)] * 3
                       + [pltpu.VMEM((n_blocks, tk, tq) if transposed else (n_blocks, tq, tk), F32),
                          pltpu.VMEM((tq, tq), F32)],
        compiler_params=_params("parallel", "arbitrary"),
        name="dsa",
    )(*args)


def _gather_cols(w, src, scale):
    src = [int(s) for s in src]
    scale = [float(c) if s >= 0 else 0.0 for s, c in zip(src, scale)]
    pieces, i = [], 0
    while i < len(src):
        j = i + 1
        while j < len(src) and scale[j] == scale[i] and (src[j] == src[j - 1] + 1 if src[i] >= 0 else src[j] < 0):
            j += 1
        if src[i] < 0:
            pieces.append(jnp.zeros((w.shape[0], j - i), w.dtype))
        else:
            run = w[:, src[i]:src[i] + j - i]
            pieces.append(run if scale[i] == 1.0 else run * scale[i])
        i = j
    return jnp.concatenate(pieces, axis=1)


class _Cols:
    def __init__(self):
        self.src, self.scale, self.rope = [], [], []

    def plain(self, start, n, scale=1.0):
        for i in range(n):
            self._add(start + i, scale, None)

    def pad(self, n):
        for _ in range(n):
            self._add(-1, 0.0, None)

    def roped(self, start, d, scale=1.0):
        half = d // 2
        for i in range(d):
            self._add(start + i, scale, (d, i % half, -1.0 if i < half else 1.0))

    def _add(self, src, scale, rope):
        self.src.append(src)
        self.scale.append(scale)
        self.rope.append(rope)

    def weights(self, w):
        return _gather_cols(w, self.src, self.scale).astype(BF16)

    def tables(self, pos):
        cs = {}
        for d in sorted({r[0] for r in self.rope if r is not None}):
            freqs = ROPE_THETA ** (-jnp.arange(d // 2, dtype=F32) * 2.0 / d)
            ang = pos.astype(F32)[:, None] * freqs[None, :]
            cs[d] = (jnp.cos(ang), jnp.sin(ang))
        cos, sin, i, n = [], [], 0, pos.shape[0]
        while i < len(self.rope):
            r, j = self.rope[i], i + 1
            if r is None:
                while j < len(self.rope) and self.rope[j] is None:
                    j += 1
                cos.append(jnp.ones((n, j - i), F32))
                sin.append(jnp.zeros((n, j - i), F32))
            else:
                while j < len(self.rope) and self.rope[j] == (r[0], r[1] + j - i, r[2]):
                    j += 1
                cos.append(cs[r[0]][0][:, r[1]:r[1] + j - i])
                sin.append(r[2] * cs[r[0]][1][:, r[1]:r[1] + j - i])
            i = j
        return jnp.concatenate(cos, axis=1), jnp.concatenate(sin, axis=1)


def _plans():
    p = _Cols()
    p.plain(O_CQ, Q_LORA)
    p.plain(O_CKV, KV_LORA)
    for h in range(H_C):
        p.plain(O_KC + DH_C * h, DH_C)
        p.plain(O_VC + DH_C * h, DH_C)
    for h in range(H_C):
        p.plain(O_QC + DH_C * h, DH_C, DH_C ** -0.5)
        p.pad(LANES - DH_C)
    p.plain(O_WIB, H_IDX)
    p.pad(LANES - H_IDX)
    assert len(p.src) == P_END

    r = _Cols()
    r.roped(O_KB, DH_B)
    r.plain(O_VB, DH_B)
    r.roped(O_KIB, D_IDX)
    r.pad(R_DIFF - (2 * DH_B + D_IDX))
    for h in range(H_D):
        r.roped(O_KD + 2 * DQK_D * h, DQK_D)
        r.roped(O_KD + 2 * DQK_D * h + DQK_D, DQK_D)
        r.plain(O_VD + DV_D * h, DV_D)
    r.roped(O_KR, ROPE_A)
    r.pad(LANES - ROPE_A)
    for h in range(H_B):
        r.roped(O_QB + DH_B * h, DH_B, DH_B ** -0.5)
    for h in range(H_IDX):
        r.roped(O_QIB + D_IDX * h, D_IDX)
    for h in range(H_D):
        r.roped(O_QD + 2 * DQK_D * h, DQK_D)
        r.roped(O_QD + 2 * DQK_D * h + DQK_D, DQK_D)
        r.pad(LANES - 2 * DQK_D)
    assert len(r.src) == R_END

    q = _Cols()
    for h in range(H_A):
        q.plain((NOPE_A + ROPE_A) * h, NOPE_A)
        q.roped((NOPE_A + ROPE_A) * h + NOPE_A, ROPE_A)
        q.pad(LANES - NOPE_A - ROPE_A)
    return p, r, q


def _layer_weights(l, prm, plans):
    p, r, q = plans
    w_in = prm["w_in"][l]
    wkvb = prm["mla_w_kvb"][l]
    k_src, v_src = [], []
    for h in range(H_A):
        k_src += list(range((NOPE_A + V_A) * h, (NOPE_A + V_A) * h + NOPE_A)) + [-1] * (LANES - NOPE_A)
        v_src += list(range((NOPE_A + V_A) * h + NOPE_A, (NOPE_A + V_A) * (h + 1)))
    place = np.zeros((ROPE_A, H_A * LANES), np.float32)
    for h in range(H_A):
        place[np.arange(ROPE_A), LANES * h + NOPE_A + np.arange(ROPE_A)] = 1.0
    dn = jnp.concatenate([jnp.zeros((2 * DQK_D,), F32), prm["diff_norm"][l]])[None, :]

    def ff(tag):
        return (prm[f"norm_{tag}"][l][None, :], prm[f"w_{tag}_gate"][l].astype(BF16),
                prm[f"w_{tag}_up"][l].astype(BF16), prm[f"w_{tag}_down"][l].astype(BF16))

    return {
        "ff1": ff("ff1"), "ff2": ff("ff2"),
        "nm": prm["norm_mix"][l][None, :],
        "wp": p.weights(w_in), "wr": r.weights(w_in),
        "qn": prm["mla_q_norm"][l][None, :], "wqb": q.weights(prm["mla_w_qb"][l]),
        "kvn": prm["mla_kv_norm"][l][None, :],
        "wk": _gather_cols(wkvb, k_src, np.ones(len(k_src))).astype(BF16),
        "pk": jnp.asarray(place).astype(BF16),
        "wv": _gather_cols(wkvb, v_src, np.ones(len(v_src))).astype(BF16),
        "lam": prm["diff_lambda"][l], "dn": dn,
        "wo": prm["w_out"][l].astype(BF16),
    }


def _trunk(x, caches, pos, layers, plans, fnorm, tq, tk, proj_fold):
    b, t, _ = x.shape
    depth = len(layers)
    tm = min(FFN_ROWS, b * t)
    _, r_plan, q_plan = plans
    pb, pt = (b // proj_fold, t * proj_fold)
    pos_p = jnp.tile(pos, proj_fold)
    tabs = r_plan.tables(pos_p) + q_plan.tables(pos_p)
    tq_p = min(PROJ_ROWS, pt)
    xf = x.reshape(b * t, D_MODEL)
    stacked = [jnp.zeros((depth, pb, pt * r, w), F32)
               for r, w in ((1, KV_LORA + ROPE_A), (1, 2 * DH_B + D_IDX), (H_C, 2 * DH_C), (H_D, 2 * DQK_D + DV_D))]
    past = [None] * 4 if caches is None else caches
    for l, lw in enumerate(layers):
        lam_init = 0.8 - 0.6 * math.exp(-0.3 * l)
        xf = _ffn(xf, None, None, *lw["ff1"], None, tm)
        outs = _proj(xf.reshape(pb, pt, D_MODEL), lw, tabs, stacked, l, tq_p)
        stacked = list(outs[:4])
        r_sb, r_diff, q_mla, q_dsa, q_idx, w_idx, q_sb, q_diff = [o.reshape(b, t, o.shape[-1]) for o in outs[4:]]
        r_mla, r_dsa = (a.reshape(depth, b, t, a.shape[-1]) for a in stacked[:2])
        mix = (
            _mla(q_mla, r_mla, past[0], l, lw, tq, tk),
            _dsa(q_dsa, q_idx, w_idx, r_dsa, past[1], l, tq, tk),
            _sb(q_sb, r_sb, past[2], l, tq, tk),
            _diff(q_diff, r_diff, past[3], l, lw, lam_init, tq, tk),
        )
        mix = [m.reshape(b * t, MIX_W) for m in mix]
        xf = _ffn(xf, mix, lw["wo"], *lw["ff2"], fnorm if l == depth - 1 else None, tm)
    y = xf.reshape(b, t, D_MODEL)
    return (y, stacked[0].reshape(depth, b, t, -1), stacked[1].reshape(depth, b, t, -1),
            stacked[2].reshape(depth, b, t, H_C, 2 * DH_C),
            stacked[3].reshape(depth, b, t, H_D, 2 * DQK_D + DV_D))


def kernel(x_prompt, x_sample, cache_mla, cache_dsa, cache_sb, cache_diff, norm_ff1, w_ff1_gate, w_ff1_up, w_ff1_down, norm_mix, w_in, mla_q_norm, mla_w_qb, mla_kv_norm, mla_w_kvb, diff_lambda, diff_norm, w_out, norm_ff2, w_ff2_gate, w_ff2_up, w_ff2_down, final_norm):
    prm = dict(norm_ff1=norm_ff1, w_ff1_gate=w_ff1_gate, w_ff1_up=w_ff1_up, w_ff1_down=w_ff1_down,
               norm_mix=norm_mix, w_in=w_in, mla_q_norm=mla_q_norm, mla_w_qb=mla_w_qb,
               mla_kv_norm=mla_kv_norm, mla_w_kvb=mla_w_kvb, diff_lambda=diff_lambda, diff_norm=diff_norm,
               w_out=w_out, norm_ff2=norm_ff2, w_ff2_gate=w_ff2_gate, w_ff2_up=w_ff2_up, w_ff2_down=w_ff2_down)
    depth = w_in.shape[0]
    plans = _plans()
    layers = [_layer_weights(l, prm, plans) for l in range(depth)]
    fnorm = final_norm[None, :]

    t_p = x_prompt.shape[1]
    assert t_p % CHUNK == 0
    tq_p = min(ATTN_ROWS, t_p)
    out_p = _trunk(x_prompt, None, jnp.arange(t_p, dtype=jnp.int32), layers, plans, fnorm, tq_p, tq_p, 1)


    b_s, t_s, _ = x_sample.shape
    past = cache_mla.shape[2]
    assert past % CHUNK == 0 and t_s <= CHUNK, "new sample frames must sit in one chunk after whole cached chunks"
    tk_s = math.gcd(past, ATTN_ROWS)
    fold_heads = lambda c: c.reshape(c.shape[:2] + (c.shape[2] * c.shape[3], c.shape[4]))
    caches = (cache_mla, cache_dsa, fold_heads(cache_sb), fold_heads(cache_diff))
    fold = math.gcd(b_s, max(1, PROJ_ROWS // t_s))
    out_s = _trunk(x_sample, caches, past + jnp.arange(t_s, dtype=jnp.int32), layers, plans, fnorm, t_s, tk_s, fold)

    return (out_p[0], out_s[0]) + tuple(out_p[1:]) + tuple(out_s[1:])
```

```python
import functools
import math

import numpy as np
import jax
import jax.numpy as jnp
from jax import lax
from jax.experimental import pallas as pl
from jax.experimental.pallas import tpu as pltpu

F32 = jnp.float32
BF16 = jnp.bfloat16

D_MODEL = 1024
CHUNK = 64
ROPE_THETA = 10000.0
EPS = 1e-6
NEG_INF = -1e30
HEAD_DIM = D_MODEL // 16
D_FF = 11 * D_MODEL // 4
H_A = 4
Q_LORA = 4 * HEAD_DIM
KV_LORA = 2 * HEAD_DIM
NOPE_A = HEAD_DIM
ROPE_A = HEAD_DIM // 2
V_A = HEAD_DIM
H_B = 4
DH_B = HEAD_DIM
H_IDX = 8
D_IDX = HEAD_DIM
TOPK_MAX = 256
H_C = 4
DH_C = HEAD_DIM
H_D = 4
DQK_D = HEAD_DIM // 2
DV_D = HEAD_DIM
COL_SIZES = (Q_LORA, KV_LORA, ROPE_A,
             H_B * DH_B, DH_B, DH_B, H_IDX * D_IDX, D_IDX, H_IDX,
             H_C * DH_C, H_C * DH_C, H_C * DH_C,
             H_D * 2 * DQK_D, H_D * 2 * DQK_D, H_D * DV_D)
(O_CQ, O_CKV, O_KR, O_QB, O_KB, O_VB, O_QIB, O_KIB, O_WIB,
 O_QC, O_KC, O_VC, O_QD, O_KD, O_VD) = np.concatenate([[0], np.cumsum(COL_SIZES)[:-1]]).tolist()

LANES = 128
VMEM_LIMIT = 56 * 1024 * 1024
MXU_TILE = 256
FF_CHUNK = MXU_TILE
ROW_CHUNK = 512
FFN_ROWS = 512
PROJ_ROWS = 256
ATTN_ROWS = 512
MIX_W = H_A * V_A
SLOT_W = H_A * LANES
assert MIX_W == H_B * DH_B == H_C * DH_C == H_D * DV_D and H_A == H_B == H_C == H_D

P_CQ, P_CKV, P_SB, P_QSB, P_WI, P_END = 0, 256, 384, 896, 1408, 1536
R_DSA, R_DIFF, R_KR, R_QDSA, R_QI, R_QDIFF, R_END = 0, 256, 768, 896, 1152, 1664, 2176


def _dot(a, b):
    return jnp.dot(a, b, preferred_element_type=F32)


def _dot_nt(a, b):
    return lax.dot_general(a, b, (((1,), (1,)), ((), ())), preferred_element_type=F32)


def _rms(x, g):
    return x * lax.rsqrt(jnp.mean(x * x, axis=-1, keepdims=True) + EPS) * g


def _const_spec(shape):
    n = len(shape)
    return pl.BlockSpec(shape, lambda *_: (0,) * n, pipeline_mode=pl.Buffered(1))


def _params(*sem):
    return pltpu.CompilerParams(dimension_semantics=sem, vmem_limit_bytes=VMEM_LIMIT)


def _ffn_kernel(*refs, has_mix, has_final):
    it = iter(refs)
    x_ref = next(it)
    if has_mix:
        o_refs = [next(it) for _ in range(4)]
        wo_ref = next(it)
    g_ref, wg_ref, wu_ref, wd_ref = next(it), next(it), next(it), next(it)
    fn_ref = next(it) if has_final else None
    out_ref = next(it)

    x = x_ref[...]
    if has_mix:
        for i, o_ref in enumerate(o_refs):
            x = x + _dot(o_ref[...], wo_ref[MIX_W * i:MIX_W * (i + 1), :])
    xn = _rms(x, g_ref[...]).astype(BF16)
    acc = jnp.zeros_like(x)
    for c0 in range(0, D_FF, FF_CHUNK):
        gate = _dot(xn, wg_ref[:, c0:c0 + FF_CHUNK])
        up = _dot(xn, wu_ref[:, c0:c0 + FF_CHUNK])
        hid = (gate * jax.nn.sigmoid(gate) * up).astype(BF16)
        acc = acc + _dot(hid, wd_ref[c0:c0 + FF_CHUNK, :])
    y = x + 0.5 * acc
    if has_final:
        y = _rms(y, fn_ref[...])
    out_ref[...] = y


def _ffn(x, mix, wo, g, wg, wu, wd, fnorm, tm):
    n = x.shape[0]
    has_mix, has_final = mix is not None, fnorm is not None
    row = lambda w: pl.BlockSpec((tm, w), lambda i: (i, 0))
    args, specs = [x], [row(D_MODEL)]
    if has_mix:
        args += list(mix) + [wo]
        specs += [row(MIX_W)] * 4 + [_const_spec(wo.shape)]
    args += [g, wg, wu, wd]
    specs += [_const_spec(g.shape), _const_spec(wg.shape), _const_spec(wu.shape), _const_spec(wd.shape)]
    if has_final:
        args.append(fnorm)
        specs.append(_const_spec(fnorm.shape))
    return pl.pallas_call(
        functools.partial(_ffn_kernel, has_mix=has_mix, has_final=has_final),
        grid=(n // tm,),
        in_specs=specs,
        out_specs=row(D_MODEL),
        out_shape=jax.ShapeDtypeStruct((n, D_MODEL), F32),
        compiler_params=_params("parallel"),
        name="ffn",
    )(*args)


def _rope(h, cos, sin_signed, d):
    w = h.shape[1]
    first = lax.broadcasted_iota(jnp.int32, h.shape, 1) % d < d // 2
    partner = jnp.where(first, pltpu.roll(h, w - d // 2, axis=1), pltpu.roll(h, d // 2, axis=1))
    return h * cos + partner * sin_signed


def _proj_kernel(x_ref, g_ref, wp_ref, wr_ref, cr_ref, sr_ref,
                 qn_ref, wqb_ref, cq_ref, sq_ref, kvn_ref,
                 _amla, _adsa, _asb, _adiff,
                 rmla_ref, rdsa_ref, rsbh_ref, rdiffh_ref, rsb_ref, rdiff_ref,
                 qmla_ref, qdsa_ref, qi_ref, wi_ref, qsb_ref, qdiff_ref):
    xn = _rms(x_ref[...], g_ref[...]).astype(BF16)
    tq = x_ref.shape[0]

    def plain(a, b):
        return _dot(xn, wp_ref[:, a:b])

    def roped(a, b, d):
        return _rope(_dot(xn, wr_ref[:, a:b]), cr_ref[:, a:b], sr_ref[:, a:b], d)

    cqn = _rms(plain(P_CQ, P_CKV), qn_ref[...]).astype(BF16)
    qmla_ref[...] = _rope(_dot(cqn, wqb_ref[...]), cq_ref[...], sq_ref[...], ROPE_A).astype(BF16)
    rmla_ref[:, :KV_LORA] = _rms(plain(P_CKV, P_SB), kvn_ref[...])
    rmla_ref[:, KV_LORA:] = roped(R_KR, R_QDSA, ROPE_A)[:, :ROPE_A]
    rdsa_ref[...] = roped(R_DSA, R_DIFF, DH_B)[:, :2 * DH_B + D_IDX]
    qdsa_ref[...] = roped(R_QDSA, R_QI, DH_B).astype(BF16)
    qi_ref[...] = roped(R_QI, R_QDIFF, D_IDX).astype(BF16)
    wi_ref[...] = plain(P_WI, P_END) * (H_IDX ** -0.5)
    rsb = plain(P_SB, P_QSB)
    rsb_ref[...] = rsb
    for h in range(H_C):
        rsbh_ref[pl.ds(h, tq, stride=H_C), :] = rsb[:, LANES * h:LANES * (h + 1)]
    qsb_ref[...] = plain(P_QSB, P_WI).astype(BF16)
    rdiff = roped(R_DIFF, R_KR, DQK_D)
    rdiff_ref[...] = rdiff
    for h in range(H_D):
        rdiffh_ref[pl.ds(h, tq, stride=H_D), :] = rdiff[:, LANES * h:LANES * (h + 1)]
    qdiff_ref[...] = roped(R_QDIFF, R_END, DQK_D).astype(BF16)


def _proj(x, lw, tabs, stacked, layer, tq):
    b, t, _ = x.shape
    cr, sr, cq, sq = tabs
    tok = lambda w: pl.BlockSpec((None, tq, w), lambda bi, i: (bi, i, 0))
    lay = lambda r, w: pl.BlockSpec((None, None, r, w), lambda bi, i: (layer, bi, i, 0))
    tab = lambda w: pl.BlockSpec((tq, w), lambda bi, i: (i, 0))
    consts = [lw["nm"], lw["wp"], lw["wr"]]
    consts2 = [lw["qn"], lw["wqb"]]
    in_specs = ([tok(D_MODEL)] + [_const_spec(a.shape) for a in consts] + [tab(R_END), tab(R_END)]
                + [_const_spec(a.shape) for a in consts2] + [tab(SLOT_W), tab(SLOT_W), _const_spec(lw["kvn"].shape)]
                + [pl.BlockSpec(memory_space=pl.ANY)] * 4)
    widths = [(SLOT_W, F32), (SLOT_W, F32), (SLOT_W, BF16), (MIX_W, BF16), (H_IDX * D_IDX, BF16), (LANES, F32),
              (SLOT_W, BF16), (SLOT_W, BF16)]
    n_in = len(in_specs)
    return pl.pallas_call(
        _proj_kernel,
        grid=(b, t // tq),
        in_specs=in_specs,
        out_specs=[lay(tq, 160), lay(tq, 192), lay(tq * H_C, LANES), lay(tq * H_D, LANES)]
                  + [tok(w) for w, _ in widths],
        out_shape=[jax.ShapeDtypeStruct(a.shape, a.dtype) for a in stacked]
                  + [jax.ShapeDtypeStruct((b, t, w), dt) for w, dt in widths],
        input_output_aliases={n_in - 4 + k: k for k in range(4)},
        compiler_params=_params("parallel", "parallel"),
        name="proj",
    )(x, *consts, cr, sr, *consts2, cq, sq, lw["kvn"], *stacked)


def _chunk_mask(tq):
    ri = lax.broadcasted_iota(jnp.int32, (tq, tq), 0)
    ci = lax.broadcasted_iota(jnp.int32, (tq, tq), 1)
    return (ci // CHUNK) <= (ri // CHUNK)


LOG2E = math.log2(math.e)


def _softmax_step(s, v, carry, scale=1.0):
    m, l, acc = carry
    c = scale * LOG2E
    m_new = jnp.maximum(m, jnp.max(s, axis=-1, keepdims=True))
    alpha = jnp.exp2((m - m_new) * c)
    p = jnp.exp2((s - m_new) * c)
    l = alpha * l + jnp.sum(p, axis=-1, keepdims=True)
    acc = alpha * acc + _dot(p.astype(BF16), v)
    return m_new, l, acc


def _softmax_init(tq):
    return (jnp.full((tq, 1), NEG_INF, F32), jnp.zeros((tq, 1), F32), jnp.zeros((tq, LANES), F32))


def _lane_lt(tq, n):
    return lax.broadcasted_iota(jnp.int32, (tq, LANES), 1) < n


def _attn_specs(tq, q_widths, rows, past, layer):
    def whole(a):
        if a.ndim == 3:
            return pl.BlockSpec((None,) + a.shape[1:], lambda bi, i: (bi, 0, 0))
        return pl.BlockSpec((None, None) + a.shape[2:], lambda bi, i: (layer, bi, 0, 0))
    specs = [pl.BlockSpec((None, tq, w), lambda bi, i: (bi, i, 0)) for w in q_widths]
    return specs + [whole(rows)] + ([whole(past)] if past is not None else [])


def _out_spec(tq):
    return pl.BlockSpec((None, tq, MIX_W), lambda bi, i: (bi, i, 0))


def _mla_kernel(*refs, p_len, t_len, tq, tk):
    if p_len:
        q_ref, new_ref, past_ref, wk_ref, pk_ref, wv_ref, o_ref, k_scr, v_scr = refs
    else:
        q_ref, new_ref, wk_ref, pk_ref, wv_ref, o_ref, k_scr, v_scr = refs
        past_ref = None
    qi = pl.program_id(1)
    scale = (NOPE_A + ROPE_A) ** -0.5

    @pl.when(qi == 0)
    def _fill():
        def fill(src_ref, dst, n):
            for r0 in range(0, n, ROW_CHUNK):
                rc = min(ROW_CHUNK, n - r0)
                lat = src_ref[r0:r0 + rc, :KV_LORA].astype(BF16)
                rope = src_ref[r0:r0 + rc, KV_LORA:].astype(BF16)
                k_scr[dst + r0:dst + r0 + rc, :] = (_dot(lat, wk_ref[...]) + _dot(rope, pk_ref[...])).astype(BF16)
                v_scr[dst + r0:dst + r0 + rc, :] = _dot(lat, wv_ref[...]).astype(BF16)
        if p_len:
            fill(past_ref, 0, p_len)
        fill(new_ref, p_len, t_len)

    d0 = pl.multiple_of(p_len + qi * tq, tq)
    n_full = (p_len + qi * tq) // tk
    dmask = _chunk_mask(tq)
    low = _lane_lt(tq, V_A)
    qs = [q_ref[:, LANES * h:LANES * (h + 1)] for h in range(H_A)]

    def step(r0, n, mask, carries):
        out = []
        for pair in range(H_A // 2):
            s = []
            for h in (2 * pair, 2 * pair + 1):
                sh = _dot_nt(qs[h], k_scr[pl.ds(r0, n), LANES * h:LANES * (h + 1)])
                s.append(sh if mask is None else jnp.where(mask, sh, NEG_INF))
            out.append(_softmax_step(jnp.concatenate(s, axis=0), v_scr[pl.ds(r0, n), LANES * pair:LANES * (pair + 1)],
                                     carries[pair], scale))
        return tuple(out)

    carries = lax.fori_loop(0, n_full, lambda kb, c: step(pl.multiple_of(kb * tk, tk), tk, None, c),
                            tuple(_softmax_init(2 * tq) for _ in range(H_A // 2)))
    carries = step(d0, tq, dmask, carries)
    for pair, (_, l, acc) in enumerate(carries):
        out = acc / l
        o_ref[:, LANES * pair:LANES * (pair + 1)] = jnp.where(low, out[:tq], out[tq:]).astype(BF16)


def _mla(q, rows, past, layer, lw, tq, tk):
    b, t, _ = q.shape
    p_len = 0 if past is None else past.shape[2]
    consts = [lw["wk"], lw["pk"], lw["wv"]]
    args = [q, rows] + ([past] if past is not None else []) + consts
    return pl.pallas_call(
        functools.partial(_mla_kernel, p_len=p_len, t_len=t, tq=tq, tk=tk),
        grid=(b, t // tq),
        in_specs=_attn_specs(tq, [SLOT_W], rows, past, layer) + [_const_spec(a.shape) for a in consts],
        out_specs=_out_spec(tq),
        out_shape=jax.ShapeDtypeStruct((b, t, MIX_W), BF16),
        scratch_shapes=[pltpu.VMEM((p_len + t, SLOT_W), BF16), pltpu.VMEM((p_len + t, MIX_W), BF16)],
        compiler_params=_params("parallel", "arbitrary"),
        name="mla",
    )(*args)


def _diff_kernel(*refs, p_len, tq, tk, lam_init):
    if p_len:
        q_ref, new_ref, past_ref, lam_ref, dn_ref, o_ref = refs
    else:
        q_ref, new_ref, lam_ref, dn_ref, o_ref = refs
        past_ref = None
    qi = pl.program_id(1)
    scale = DQK_D ** -0.5
    lam = lam_ref[...]
    lam_full = (jnp.exp(jnp.sum(lam[0:1] * lam[1:2], axis=-1, keepdims=True))
                - jnp.exp(jnp.sum(lam[2:3] * lam[3:4], axis=-1, keepdims=True)) + lam_init)
    dmask = _chunk_mask(tq)
    lane = lax.broadcasted_iota(jnp.int32, (tq, LANES), 1)
    sel1 = jnp.where(lane < DQK_D, 1.0, 0.0).astype(BF16)
    sel2 = jnp.where((lane >= DQK_D) & (lane < 2 * DQK_D), 1.0, 0.0).astype(BF16)
    is_v = lane >= 2 * DQK_D
    n_past = p_len // tk
    n_new = (qi * tq) // tk

    q12 = []
    for h in range(H_D):
        qh = q_ref[:, LANES * h:LANES * (h + 1)]
        q12.append(jnp.concatenate([qh * sel1, qh * sel2], axis=0))
    dmask2 = jnp.concatenate([dmask, dmask], axis=0)

    def step(load, masked, carries):
        out = []
        for h in range(H_D):
            kv = load(h)
            s = _dot_nt(q12[h], kv)
            if masked:
                s = jnp.where(dmask2, s, NEG_INF)
            out.append(_softmax_step(s, kv, carries[h], scale))
        return tuple(out)

    def new_rows(r0, n):
        return lambda h: new_ref[pl.ds(r0, n), LANES * h:LANES * (h + 1)].astype(BF16)

    def past_rows(kb):
        return lambda h: past_ref[pl.ds(H_D * kb * tk + h, tk, stride=H_D), :].astype(BF16)

    carries = tuple(_softmax_init(2 * tq) for _ in range(H_D))
    for kb in range(n_past):
        carries = step(past_rows(kb), False, carries)
    carries = lax.fori_loop(0, n_new, lambda kb, c: step(new_rows(pl.multiple_of(kb * tk, tk), tk), False, c),
                            carries)
    carries = step(new_rows(pl.multiple_of(qi * tq, tq), tq), True, carries)
    outs = []
    for h in range(H_D):
        _, l, acc = carries[h]
        p = acc / l
        o = p[:tq] - lam_full * p[tq:]
        ms = jnp.sum(jnp.where(is_v, o * o, 0.0), axis=-1, keepdims=True) * (1.0 / DV_D)
        outs.append(o * lax.rsqrt(ms + EPS) * dn_ref[...] * (1.0 - lam_init))
    for pair in range(H_D // 2):
        o_ref[:, LANES * pair:LANES * (pair + 1)] = jnp.where(
            is_v, outs[2 * pair + 1], pltpu.roll(outs[2 * pair], 2 * DQK_D, axis=1)).astype(BF16)


def _diff(q, rows, past, layer, lw, lam_init, tq, tk):
    b, t, _ = q.shape
    p_len = 0 if past is None else past.shape[2] // H_D
    consts = [lw["lam"], lw["dn"]]
    args = [q, rows] + ([past] if past is not None else []) + consts
    return pl.pallas_call(
        functools.partial(_diff_kernel, p_len=p_len, tq=tq, tk=tk, lam_init=lam_init),
        grid=(b, t // tq),
        in_specs=_attn_specs(tq, [SLOT_W], rows, past, layer) + [_const_spec(a.shape) for a in consts],
        out_specs=_out_spec(tq),
        out_shape=jax.ShapeDtypeStruct((b, t, MIX_W), BF16),
        compiler_params=_params("parallel", "parallel"),
        name="diff",
    )(*args)


def _sb_kernel(*refs, p_len, tq, tk):
    if p_len:
        q_ref, new_ref, past_ref, o_ref = refs
    else:
        q_ref, new_ref, o_ref = refs
        past_ref = None
    qi = pl.program_id(1)
    n_past = p_len // tk
    n_new = (qi * tq) // tk

    def later(n):
        n = min(n, MXU_TILE)
        ri = lax.broadcasted_iota(jnp.int32, (n, n), 0)
        ci = lax.broadcasted_iota(jnp.int32, (n, n), 1)
        return jnp.where(ri > ci, 1.0, 0.0).astype(BF16)

    def later_sums(x, tri):
        w = tri.shape[0]
        pieces, after = [], None
        for j in reversed(range(x.shape[1] // w)):
            xj = x[:, w * j:w * (j + 1)]
            hi = xj.astype(BF16)
            lo = (xj - hi.astype(F32)).astype(BF16)
            inside = _dot(hi, tri) + _dot(lo, tri)
            total = inside[:, :1] + xj[:, :1]
            if after is not None:
                inside, total = inside + after, total + after
            pieces.append(inside)
            after = total
        return (pieces[0] if len(pieces) == 1 else jnp.concatenate(pieces[::-1], axis=1)), after

    later_k = later(tk)
    later_q = later_k if min(tq, MXU_TILE) == min(tk, MXU_TILE) else later(tq)
    ri = lax.broadcasted_iota(jnp.int32, (tq, tq), 0)
    ci = lax.broadcasted_iota(jnp.int32, (tq, tq), 1)
    causal = ci < ri
    is_v = lax.broadcasted_iota(jnp.int32, (tq, LANES), 1) >= DH_C

    qs = [q_ref[:, LANES * h:LANES * (h + 1)] for h in range(H_C)]

    groups = [list(range(H_C))] if tq < LANES else [[h] for h in range(H_C)]

    def step(load, mask, tri, carries):
        out = [None] * H_C
        for group in groups:
            kvs, takes, keeps = [], [], []
            for h in group:
                kv = load(h)
                z = _dot_nt(qs[h], kv)
                log_take = jnp.minimum(z, 0.0) - jnp.log(1.0 + jnp.exp2(jnp.abs(z) * -LOG2E))
                log_keep = log_take - z
                if mask is not None:
                    log_keep = jnp.where(mask, log_keep, 0.0)
                kvs.append(kv)
                takes.append(log_take)
                keeps.append(log_keep)
            later_sum, block_sum = later_sums(keeps[0] if len(group) == 1 else jnp.concatenate(keeps, axis=0), tri)
            for i, h in enumerate(group):
                run, acc = carries[h]
                rows = slice(tq * i, tq * (i + 1))
                a = jnp.exp2((takes[i] + (later_sum[rows] + run)) * LOG2E)
                if mask is not None:
                    a = jnp.where(mask, a, 0.0)
                out[h] = (run + block_sum[rows], acc + _dot(a.astype(BF16), kvs[i]))
        return tuple(out)

    def new_rows(r0, n):
        return lambda h: new_ref[pl.ds(r0, n), LANES * h:LANES * (h + 1)].astype(BF16)

    def past_rows(kb):
        return lambda h: past_ref[pl.ds(H_C * kb * tk + h, tk, stride=H_C), :].astype(BF16)

    carries = tuple((jnp.zeros((tq, 1), F32), jnp.zeros((tq, LANES), F32)) for _ in range(H_C))
    carries = step(new_rows(pl.multiple_of(qi * tq, tq), tq), causal, later_q, carries)
    carries = lax.fori_loop(
        0, n_new,
        lambda i, c: step(new_rows(pl.multiple_of((n_new - 1 - i) * tk, tk), tk), None, later_k, c), carries)
    for kb in reversed(range(n_past)):
        carries = step(past_rows(kb), None, later_k, carries)
    for pair in range(H_C // 2):
        o_ref[:, LANES * pair:LANES * (pair + 1)] = jnp.where(
            is_v, carries[2 * pair + 1][1], pltpu.roll(carries[2 * pair][1], DH_C, axis=1)).astype(BF16)


def _sb(q, rows, past, layer, tq, tk):
    b, t, _ = q.shape
    p_len = 0 if past is None else past.shape[2] // H_C
    args = [q, rows] + ([past] if past is not None else [])
    return pl.pallas_call(
        functools.partial(_sb_kernel, p_len=p_len, tq=tq, tk=tk),
        grid=(b, t // tq),
        in_specs=_attn_specs(tq, [SLOT_W], rows, past, layer),
        out_specs=_out_spec(tq),
        out_shape=jax.ShapeDtypeStruct((b, t, MIX_W), BF16),
        compiler_params=_params("parallel", "parallel"),
        name="sb",
    )(*args)


def _dsa_kernel(*refs, p_len, t_len, tq, tk, n_sel):
    if p_len:
        q_ref, qi_ref, wi_ref, new_ref, past_ref, o_ref, kk_scr, vv_scr, ki_scr, sc_scr, scd_scr = refs
    else:
        q_ref, qi_ref, wi_ref, new_ref, o_ref, kk_scr, vv_scr, ki_scr, sc_scr, scd_scr = refs
        past_ref = None
    qi = pl.program_id(1)
    l_len = p_len + t_len
    n_tiles = tk // LANES
    k_sel = float(n_sel)

    @pl.when(qi == 0)
    def _fill():
        r1 = lax.broadcasted_iota(jnp.int32, (LANES, 2 * LANES), 0)
        c1 = lax.broadcasted_iota(jnp.int32, (LANES, 2 * LANES), 1)
        dup_kv = jnp.where((c1 % DH_B == r1 % DH_B) & (c1 // LANES == r1 // DH_B), 1.0, 0.0).astype(BF16)
        r2 = lax.broadcasted_iota(jnp.int32, (D_IDX, LANES), 0)
        c2 = lax.broadcasted_iota(jnp.int32, (D_IDX, LANES), 1)
        dup_ki = jnp.where(c2 % D_IDX == r2, 1.0, 0.0).astype(BF16)

        def fill(src_ref, dst, n):
            for r0 in range(0, n, ROW_CHUNK):
                rc = min(ROW_CHUNK, n - r0)
                kv2 = _dot(src_ref[r0:r0 + rc, :LANES].astype(BF16), dup_kv)
                kk_scr[dst + r0:dst + r0 + rc, :] = kv2[:, :LANES].astype(BF16)
                vv_scr[dst + r0:dst + r0 + rc, :] = kv2[:, LANES:].astype(BF16)
                ki_scr[dst + r0:dst + r0 + rc, :] = _dot(src_ref[r0:r0 + rc, LANES:].astype(BF16), dup_ki).astype(BF16)
        if p_len:
            fill(past_ref, 0, p_len)
        fill(new_ref, p_len, t_len)

    d0 = pl.multiple_of(p_len + qi * tq, tq)
    n_full = (p_len + qi * tq) // tk
    lane = lax.broadcasted_iota(jnp.int32, (tq, LANES), 1)
    low = lane < DH_B
    sel_lo = jnp.where(low, 1.0, 0.0).astype(BF16)
    sel_hi = jnp.where(low, 0.0, 1.0).astype(BF16)
    dmask = _chunk_mask(tq)

    wi = wi_ref[...]
    wcol = [jnp.sum(jnp.where(lane == h, wi, 0.0), axis=-1, keepdims=True) for h in range(H_IDX)]
    q_idx = []
    for pr in range(H_IDX // 2):
        blk = qi_ref[:, LANES * pr:LANES * (pr + 1)]
        q_idx += [blk * sel_lo, blk * sel_hi]
    q_idx = jnp.concatenate(q_idx, axis=0)
    w_all = jnp.concatenate(wcol, axis=0)

    def idx_score(ki2):
        terms = jnp.maximum(_dot_nt(q_idx, ki2), 0.0) * w_all
        sc = terms[:tq]
        for h in range(1, H_IDX):
            sc = sc + terms[tq * h:tq * (h + 1)]
        return sc

    def score_body(kb, _):
        r0 = pl.multiple_of(kb * tk, tk)
        sc_scr[kb] = idx_score(ki_scr[pl.ds(r0, tk), :])
        return 0

    lax.fori_loop(0, n_full, score_body, 0)
    scd_scr[...] = jnp.where(dmask, idx_score(ki_scr[pl.ds(d0, tq), :]), -jnp.inf)

    def fold(tile_fn, diag_fn, comb, reduce, init):
        def body(kb, acc):
            for j in range(n_tiles):
                acc = comb(acc, tile_fn(sc_scr[kb, :, LANES * j:LANES * (j + 1)], kb * tk + LANES * j))
            return acc
        acc = lax.fori_loop(0, n_full, body, jnp.full((tq, LANES), init, F32))
        return comb(reduce(acc, axis=-1, keepdims=True), reduce(diag_fn(scd_scr[...]), axis=-1, keepdims=True))

    def wide(v):
        return jnp.broadcast_to(v, (tq, LANES))

    def count_ge(thr):
        thr_w = wide(thr)
        return fold(lambda s, _: jnp.where(s >= thr_w, 1.0, 0.0), lambda s: jnp.where(s >= thr, 1.0, 0.0),
                    jnp.add, jnp.sum, 0.0)

    def count_gt(thr):
        thr_w = wide(thr)
        return fold(lambda s, _: jnp.where(s > thr_w, 1.0, 0.0), lambda s: jnp.where(s > thr, 1.0, 0.0),
                    jnp.add, jnp.sum, 0.0)

    def min_ge(thr):
        thr_w = wide(thr)
        return fold(lambda s, _: jnp.where(s >= thr_w, s, jnp.inf), lambda s: jnp.where(s >= thr, s, jnp.inf),
                    jnp.minimum, jnp.min, jnp.inf)

    row_max = fold(lambda s, _: s, lambda s: s, jnp.maximum, jnp.max, -jnp.inf)
    row_min = fold(lambda s, _: s, lambda s: jnp.where(s > -jnp.inf, s, jnp.inf), jnp.minimum, jnp.min, jnp.inf)

    def bisect(_, lh):
        lo, hi = lh
        mid = lo + 0.5 * (hi - lo)
        ge = count_ge(mid) >= k_sel
        return jnp.where(ge, mid, lo), jnp.where(ge, hi, mid)

    def settle(lo):
        thr = min_ge(lo)
        n_gt = count_gt(thr)
        return thr, n_gt, jnp.max(jnp.where(n_gt >= k_sel, 1, 0))

    hi0 = row_max + jnp.abs(row_max) * 1e-6 + 1e-30
    lo, hi = lax.fori_loop(0, 20, bisect, (row_min, hi0))
    thr, n_gt, bad = settle(lo)

    def more(state):
        lo, hi, _, _, _, it = state
        lo, hi = lax.fori_loop(0, 4, bisect, (lo, hi))
        thr, n_gt, bad = settle(lo)
        return lo, hi, thr, n_gt, bad, it + 1

    _, _, thr, n_gt, _, _ = lax.while_loop(lambda st: (st[4] > 0) & (st[5] < 64), more,
                                           (lo, hi, thr, n_gt, bad, 0))

    n_ge = count_ge(thr)
    need = k_sel - n_gt
    excess = n_ge > k_sel
    thr_w = wide(thr)

    def cut_search(_):
        def count_tied_upto(j):
            j_w = wide(j)
            def tile(s, base):
                idx = base + lane
                return jnp.where((s == thr_w) & (idx <= j_w), 1.0, 0.0)
            def diag(s):
                idx = d0 + lax.broadcasted_iota(jnp.int32, (tq, tq), 1)
                return jnp.where((s == thr) & (idx <= j), 1.0, 0.0)
            return fold(tile, diag, jnp.add, jnp.sum, 0.0)

        def step(_, jj):
            jlo, jhi = jj
            mid = (jlo + jhi) >> 1
            ok = count_tied_upto(mid) >= need
            return jnp.where(ok, jlo, mid), jnp.where(ok, mid, jhi)

        steps = int(math.ceil(math.log2(l_len))) + 1
        _, jhi = lax.fori_loop(0, steps, step,
                               (jnp.full((tq, 1), -1, jnp.int32), jnp.full((tq, 1), l_len - 1, jnp.int32)))
        return jnp.where(excess, jhi, l_len)

    cut = lax.cond(jnp.max(jnp.where(excess, 1, 0)) > 0, cut_search,
                   lambda _: jnp.full((tq, 1), l_len, jnp.int32), 0)
    cut_w = wide(cut)

    def picked_tile(s, base):
        return (s > thr_w) | ((s == thr_w) & (base + lane <= cut_w))

    idx_d = d0 + lax.broadcasted_iota(jnp.int32, (tq, tq), 1)
    sd = scd_scr[...]
    picked_d = (sd > thr) | ((sd == thr) & (idx_d <= cut))

    qs = []
    for pair in range(H_B // 2):
        blk = q_ref[:, LANES * pair:LANES * (pair + 1)]
        qs += [blk * sel_lo, blk * sel_hi]
    qs = jnp.concatenate(qs, axis=0)

    def attend(r0, n, bias, carry):
        s = _dot_nt(qs, kk_scr[pl.ds(r0, n), :]) + jnp.concatenate([bias] * H_B, axis=0)
        return _softmax_step(s, vv_scr[pl.ds(r0, n), :], carry)

    def body(kb, carry):
        bias = jnp.concatenate(
            [jnp.where(picked_tile(sc_scr[kb, :, LANES * j:LANES * (j + 1)], kb * tk + LANES * j), 0.0, NEG_INF)
             for j in range(n_tiles)], axis=1)
        return attend(pl.multiple_of(kb * tk, tk), tk, bias, carry)

    carry = lax.fori_loop(0, n_full, body, _softmax_init(H_B * tq))
    _, l, acc = attend(d0, tq, jnp.where(picked_d, 0.0, NEG_INF), carry)
    out = acc / l
    for pair in range(H_B // 2):
        even, odd = out[tq * 2 * pair:tq * (2 * pair + 1)], out[tq * (2 * pair + 1):tq * (2 * pair + 2)]
        o_ref[:, LANES * pair:LANES * (pair + 1)] = jnp.where(low, even, odd).astype(BF16)


def _dsat_kernel(*refs, p_len, t_len, tq, tk, n_sel):
    if p_len:
        q_ref, qi_ref, wi_ref, new_ref, past_ref, o_ref, kk_scr, vv_scr, ki_scr, sc_scr, scd_scr = refs
    else:
        q_ref, qi_ref, wi_ref, new_ref, o_ref, kk_scr, vv_scr, ki_scr, sc_scr, scd_scr = refs
        past_ref = None
    qi = pl.program_id(1)
    l_len = p_len + t_len
    k_sel = float(n_sel)
    sub = 8

    @pl.when(qi == 0)
    def _fill():
        r1 = lax.broadcasted_iota(jnp.int32, (LANES, 2 * LANES), 0)
        c1 = lax.broadcasted_iota(jnp.int32, (LANES, 2 * LANES), 1)
        dup_kv = jnp.where((c1 % DH_B == r1 % DH_B) & (c1 // LANES == r1 // DH_B), 1.0, 0.0).astype(BF16)
        r2 = lax.broadcasted_iota(jnp.int32, (D_IDX, LANES), 0)
        c2 = lax.broadcasted_iota(jnp.int32, (D_IDX, LANES), 1)
        dup_ki = jnp.where(c2 % D_IDX == r2, 1.0, 0.0).astype(BF16)

        def fill(src_ref, dst, n):
            for r0 in range(0, n, ROW_CHUNK):
                rc = min(ROW_CHUNK, n - r0)
                kv2 = _dot(src_ref[r0:r0 + rc, :LANES].astype(BF16), dup_kv)
                kk_scr[dst + r0:dst + r0 + rc, :] = kv2[:, :LANES].astype(BF16)
                vv_scr[dst + r0:dst + r0 + rc, :] = kv2[:, LANES:].astype(BF16)
                ki_scr[dst + r0:dst + r0 + rc, :] = _dot(src_ref[r0:r0 + rc, LANES:].astype(BF16), dup_ki).astype(BF16)
        if p_len:
            fill(past_ref, 0, p_len)
        fill(new_ref, p_len, t_len)

    d0 = pl.multiple_of(p_len + qi * tq, tq)
    n_full = (p_len + qi * tq) // tk
    lane = lax.broadcasted_iota(jnp.int32, (tq, LANES), 1)
    low = lane < DH_B
    sel_lo = jnp.where(low, 1.0, 0.0).astype(BF16)
    sel_hi = jnp.where(low, 0.0, 1.0).astype(BF16)
    dmask = (lax.broadcasted_iota(jnp.int32, (tq, tq), 0) // CHUNK) <= (lax.broadcasted_iota(jnp.int32, (tq, tq), 1) // CHUNK)

    w_t = jnp.transpose(wi_ref[...])
    q_idx = []
    for pr in range(H_IDX // 2):
        blk = qi_ref[:, LANES * pr:LANES * (pr + 1)]
        q_idx += [blk * sel_lo, blk * sel_hi]

    def idx_score(ki2):
        sc = None
        for h in range(H_IDX):
            term = jnp.maximum(_dot_nt(ki2, q_idx[h]), 0.0) * w_t[h:h + 1, :]
            sc = term if sc is None else sc + term
        return sc

    def score_body(kb, _):
        sc_scr[kb] = idx_score(ki_scr[pl.ds(pl.multiple_of(kb * tk, tk), tk), :])
        return 0

    lax.fori_loop(0, n_full, score_body, 0)
    scd_scr[...] = jnp.where(dmask, idx_score(ki_scr[pl.ds(d0, tq), :]), -jnp.inf)

    def fold(fn, comb, reduce, init):
        def part(x):
            return reduce(x.reshape(x.shape[0] // sub, sub, tq), axis=0)
        acc = lax.fori_loop(0, n_full, lambda kb, a: comb(a, part(fn(sc_scr[kb], kb * tk))),
                            jnp.full((sub, tq), init, F32))
        acc = comb(acc, part(fn(scd_scr[...], d0)))
        return reduce(acc, axis=0, keepdims=True)

    def count_ge(thr):
        return fold(lambda s, _: jnp.where(s >= thr, 1.0, 0.0), jnp.add, jnp.sum, 0.0)

    def count_gt(thr):
        return fold(lambda s, _: jnp.where(s > thr, 1.0, 0.0), jnp.add, jnp.sum, 0.0)

    def min_ge(thr):
        return fold(lambda s, _: jnp.where(s >= thr, s, jnp.inf), jnp.minimum, jnp.min, jnp.inf)

    row_max = fold(lambda s, _: s, jnp.maximum, jnp.max, -jnp.inf)
    row_min = fold(lambda s, _: jnp.where(s > -jnp.inf, s, jnp.inf), jnp.minimum, jnp.min, jnp.inf)

    def bisect(_, lh):
        lo, hi = lh
        mid = lo + 0.5 * (hi - lo)
        ge = count_ge(mid) >= k_sel
        return jnp.where(ge, mid, lo), jnp.where(ge, hi, mid)

    def settle(lo):
        thr = min_ge(lo)
        n_gt = count_gt(thr)
        return thr, n_gt, jnp.max(jnp.where(n_gt >= k_sel, 1, 0))

    hi0 = row_max + jnp.abs(row_max) * 1e-6 + 1e-30
    lo, hi = lax.fori_loop(0, 20, bisect, (row_min, hi0))
    thr, n_gt, bad = settle(lo)

    def more(state):
        lo, hi, _, _, _, it = state
        lo, hi = lax.fori_loop(0, 4, bisect, (lo, hi))
        thr, n_gt, bad = settle(lo)
        return lo, hi, thr, n_gt, bad, it + 1

    _, _, thr, n_gt, _, _ = lax.while_loop(lambda st: (st[4] > 0) & (st[5] < 64), more,
                                           (lo, hi, thr, n_gt, bad, 0))

    n_ge = count_ge(thr)
    need = k_sel - n_gt
    excess = n_ge > k_sel

    def key_index(s, base):
        return base + lax.broadcasted_iota(jnp.int32, s.shape, 0)

    def cut_search(_):
        def count_tied_upto(j):
            return fold(lambda s, base: jnp.where((s == thr) & (key_index(s, base) <= j), 1.0, 0.0),
                        jnp.add, jnp.sum, 0.0)

        def step(_, jj):
            jlo, jhi = jj
            mid = (jlo + jhi) >> 1
            ok = count_tied_upto(mid) >= need
            return jnp.where(ok, jlo, mid), jnp.where(ok, mid, jhi)

        steps = int(math.ceil(math.log2(l_len))) + 1
        _, jhi = lax.fori_loop(0, steps, step,
                               (jnp.full((1, tq), -1, jnp.int32), jnp.full((1, tq), l_len - 1, jnp.int32)))
        return jnp.where(excess, jhi, l_len)

    cut = lax.cond(jnp.max(jnp.where(excess, 1, 0)) > 0, cut_search,
                   lambda _: jnp.full((1, tq), l_len, jnp.int32), 0)

    def bias_of(s, base):
        return jnp.where((s > thr) | ((s == thr) & (key_index(s, base) <= cut)), 0.0, NEG_INF)

    qs = []
    for pair in range(H_B // 2):
        blk = q_ref[:, LANES * pair:LANES * (pair + 1)]
        qs += [blk * sel_lo, blk * sel_hi]

    def attend(r0, n, bias_t, carries):
        bias = jnp.transpose(bias_t)
        kk, vv = kk_scr[pl.ds(r0, n), :], vv_scr[pl.ds(r0, n), :]
        return tuple(_softmax_step(_dot_nt(qs[h], kk) + bias, vv, carries[h]) for h in range(H_B))

    carries = lax.fori_loop(
        0, n_full, lambda kb, c: attend(pl.multiple_of(kb * tk, tk), tk, bias_of(sc_scr[kb], kb * tk), c),
        tuple(_softmax_init(tq) for _ in range(H_B)))
    carries = attend(d0, tq, bias_of(scd_scr[...], d0), carries)
    outs = [acc / l for _, l, acc in carries]
    for pair in range(H_B // 2):
        o_ref[:, LANES * pair:LANES * (pair + 1)] = jnp.where(low, outs[2 * pair], outs[2 * pair + 1]).astype(BF16)


def _dsa(q, qidx, wi, rows, past, layer, tq, tk):
    b, t, _ = q.shape
    p_len = 0 if past is None else past.shape[2]
    l_len = p_len + t
    n_sel = min(TOPK_MAX, l_len // 4)
    n_blocks = max((l_len - tq) // tk, 1)
    args = [q, qidx, wi, rows] + ([past] if past is not None else [])
    transposed = tq % LANES == 0
    body = _dsat_kernel if transposed else _dsa_kernel
    return pl.pallas_call(
        functools.partial(body, p_len=p_len, t_len=t, tq=tq, tk=tk, n_sel=n_sel),
        grid=(b, t // tq),
        in_specs=_attn_specs(tq, [MIX_W, H_IDX * D_IDX, LANES], rows, past, layer),
        out_specs=_out_spec(tq),
        out_shape=jax.ShapeDtypeStruct((b, t, MIX_W), BF16),
        scratch_shapes=[pltpu.VMEM((l_len, LANES), BF16)] * 3
                       + [pltpu.VMEM((n_blocks, tk, tq) if transposed else (n_blocks, tq, tk), F32),
                          pltpu.VMEM((tq, tq), F32)],
        compiler_params=_params("parallel", "arbitrary"),
        name="dsa",
    )(*args)


def _gather_cols(w, src, scale):
    src = [int(s) for s in src]
    scale = [float(c) if s >= 0 else 0.0 for s, c in zip(src, scale)]
    pieces, i = [], 0
    while i < len(src):
        j = i + 1
        while j < len(src) and scale[j] == scale[i] and (src[j] == src[j - 1] + 1 if src[i] >= 0 else src[j] < 0):
            j += 1
        if src[i] < 0:
            pieces.append(jnp.zeros((w.shape[0], j - i), w.dtype))
        else:
            run = w[:, src[i]:src[i] + j - i]
            pieces.append(run if scale[i] == 1.0 else run * scale[i])
        i = j
    return jnp.concatenate(pieces, axis=1)


class _Cols:
    def __init__(self):
        self.src, self.scale, self.rope = [], [], []

    def plain(self, start, n, scale=1.0):
        for i in range(n):
            self._add(start + i, scale, None)

    def pad(self, n):
        for _ in range(n):
            self._add(-1, 0.0, None)

    def roped(self, start, d, scale=1.0):
        half = d // 2
        for i in range(d):
            self._add(start + i, scale, (d, i % half, -1.0 if i < half else 1.0))

    def _add(self, src, scale, rope):
        self.src.append(src)
        self.scale.append(scale)
        self.rope.append(rope)

    def weights(self, w):
        return _gather_cols(w, self.src, self.scale).astype(BF16)

    def tables(self, pos):
        cs = {}
        for d in sorted({r[0] for r in self.rope if r is not None}):
            freqs = ROPE_THETA ** (-jnp.arange(d // 2, dtype=F32) * 2.0 / d)
            ang = pos.astype(F32)[:, None] * freqs[None, :]
            cs[d] = (jnp.cos(ang), jnp.sin(ang))
        cos, sin, i, n = [], [], 0, pos.shape[0]
        while i < len(self.rope):
            r, j = self.rope[i], i + 1
            if r is None:
                while j < len(self.rope) and self.rope[j] is None:
                    j += 1
                cos.append(jnp.ones((n, j - i), F32))
                sin.append(jnp.zeros((n, j - i), F32))
            else:
                while j < len(self.rope) and self.rope[j] == (r[0], r[1] + j - i, r[2]):
                    j += 1
                cos.append(cs[r[0]][0][:, r[1]:r[1] + j - i])
                sin.append(r[2] * cs[r[0]][1][:, r[1]:r[1] + j - i])
            i = j
        return jnp.concatenate(cos, axis=1), jnp.concatenate(sin, axis=1)


def _plans():
    p = _Cols()
    p.plain(O_CQ, Q_LORA)
    p.plain(O_CKV, KV_LORA)
    for h in range(H_C):
        p.plain(O_KC + DH_C * h, DH_C)
        p.plain(O_VC + DH_C * h, DH_C)
    for h in range(H_C):
        p.plain(O_QC + DH_C * h, DH_C, DH_C ** -0.5)
        p.pad(LANES - DH_C)
    p.plain(O_WIB, H_IDX)
    p.pad(LANES - H_IDX)
    assert len(p.src) == P_END

    r = _Cols()
    r.roped(O_KB, DH_B)
    r.plain(O_VB, DH_B)
    r.roped(O_KIB, D_IDX)
    r.pad(R_DIFF - (2 * DH_B + D_IDX))
    for h in range(H_D):
        r.roped(O_KD + 2 * DQK_D * h, DQK_D)
        r.roped(O_KD + 2 * DQK_D * h + DQK_D, DQK_D)
        r.plain(O_VD + DV_D * h, DV_D)
    r.roped(O_KR, ROPE_A)
    r.pad(LANES - ROPE_A)
    for h in range(H_B):
        r.roped(O_QB + DH_B * h, DH_B, DH_B ** -0.5)
    for h in range(H_IDX):
        r.roped(O_QIB + D_IDX * h, D_IDX)
    for h in range(H_D):
        r.roped(O_QD + 2 * DQK_D * h, DQK_D)
        r.roped(O_QD + 2 * DQK_D * h + DQK_D, DQK_D)
        r.pad(LANES - 2 * DQK_D)
    assert len(r.src) == R_END

    q = _Cols()
    for h in range(H_A):
        q.plain((NOPE_A + ROPE_A) * h, NOPE_A)
        q.roped((NOPE_A + ROPE_A) * h + NOPE_A, ROPE_A)
        q.pad(LANES - NOPE_A - ROPE_A)
    return p, r, q


def _layer_weights(l, prm, plans):
    p, r, q = plans
    w_in = prm["w_in"][l]
    wkvb = prm["mla_w_kvb"][l]
    k_src, v_src = [], []
    for h in range(H_A):
        k_src += list(range((NOPE_A + V_A) * h, (NOPE_A + V_A) * h + NOPE_A)) + [-1] * (LANES - NOPE_A)
        v_src += list(range((NOPE_A + V_A) * h + NOPE_A, (NOPE_A + V_A) * (h + 1)))
    place = np.zeros((ROPE_A, H_A * LANES), np.float32)
    for h in range(H_A):
        place[np.arange(ROPE_A), LANES * h + NOPE_A + np.arange(ROPE_A)] = 1.0
    dn = jnp.concatenate([jnp.zeros((2 * DQK_D,), F32), prm["diff_norm"][l]])[None, :]

    def ff(tag):
        return (prm[f"norm_{tag}"][l][None, :], prm[f"w_{tag}_gate"][l].astype(BF16),
                prm[f"w_{tag}_up"][l].astype(BF16), prm[f"w_{tag}_down"][l].astype(BF16))

    return {
        "ff1": ff("ff1"), "ff2": ff("ff2"),
        "nm": prm["norm_mix"][l][None, :],
        "wp": p.weights(w_in), "wr": r.weights(w_in),
        "qn": prm["mla_q_norm"][l][None, :], "wqb": q.weights(prm["mla_w_qb"][l]),
        "kvn": prm["mla_kv_norm"][l][None, :],
        "wk": _gather_cols(wkvb, k_src, np.ones(len(k_src))).astype(BF16),
        "pk": jnp.asarray(place).astype(BF16),
        "wv": _gather_cols(wkvb, v_src, np.ones(len(v_src))).astype(BF16),
        "lam": prm["diff_lambda"][l], "dn": dn,
        "wo": prm["w_out"][l].astype(BF16),
    }


def _trunk(x, caches, pos, layers, plans, fnorm, tq, tk, proj_fold):
    b, t, _ = x.shape
    depth = len(layers)
    tm = min(FFN_ROWS, b * t)
    _, r_plan, q_plan = plans
    pb, pt = (b // proj_fold, t * proj_fold)
    pos_p = jnp.tile(pos, proj_fold)
    tabs = r_plan.tables(pos_p) + q_plan.tables(pos_p)
    tq_p = min(PROJ_ROWS, pt)
    xf = x.reshape(b * t, D_MODEL)
    stacked = [jnp.zeros((depth, pb, pt * r, w), F32)
               for r, w in ((1, KV_LORA + ROPE_A), (1, 2 * DH_B + D_IDX), (H_C, 2 * DH_C), (H_D, 2 * DQK_D + DV_D))]
    past = [None] * 4 if caches is None else caches
    for l, lw in enumerate(layers):
        lam_init = 0.8 - 0.6 * math.exp(-0.3 * l)
        xf = _ffn(xf, None, None, *lw["ff1"], None, tm)
        outs = _proj(xf.reshape(pb, pt, D_MODEL), lw, tabs, stacked, l, tq_p)
        stacked = list(outs[:4])
        r_sb, r_diff, q_mla, q_dsa, q_idx, w_idx, q_sb, q_diff = [o.reshape(b, t, o.shape[-1]) for o in outs[4:]]
        r_mla, r_dsa = (a.reshape(depth, b, t, a.shape[-1]) for a in stacked[:2])
        mix = (
            _mla(q_mla, r_mla, past[0], l, lw, tq, tk),
            _dsa(q_dsa, q_idx, w_idx, r_dsa, past[1], l, tq, tk),
            _sb(q_sb, r_sb, past[2], l, tq, tk),
            _diff(q_diff, r_diff, past[3], l, lw, lam_init, tq, tk),
        )
        mix = [m.reshape(b * t, MIX_W) for m in mix]
        xf = _ffn(xf, mix, lw["wo"], *lw["ff2"], fnorm if l == depth - 1 else None, tm)
    y = xf.reshape(b, t, D_MODEL)
    return (y, stacked[0].reshape(depth, b, t, -1), stacked[1].reshape(depth, b, t, -1),
            stacked[2].reshape(depth, b, t, H_C, 2 * DH_C),
            stacked[3].reshape(depth, b, t, H_D, 2 * DQK_D + DV_D))


def kernel(x_prompt, x_sample, cache_mla, cache_dsa, cache_sb, cache_diff, norm_ff1, w_ff1_gate, w_ff1_up, w_ff1_down, norm_mix, w_in, mla_q_norm, mla_w_qb, mla_kv_norm, mla_w_kvb, diff_lambda, diff_norm, w_out, norm_ff2, w_ff2_gate, w_ff2_up, w_ff2_down, final_norm):
    prm = dict(norm_ff1=norm_ff1, w_ff1_gate=w_ff1_gate, w_ff1_up=w_ff1_up, w_ff1_down=w_ff1_down,
               norm_mix=norm_mix, w_in=w_in, mla_q_norm=mla_q_norm, mla_w_qb=mla_w_qb,
               mla_kv_norm=mla_kv_norm, mla_w_kvb=mla_w_kvb, diff_lambda=diff_lambda, diff_norm=diff_norm,
               w_out=w_out, norm_ff2=norm_ff2, w_ff2_gate=w_ff2_gate, w_ff2_up=w_ff2_up, w_ff2_down=w_ff2_down)
    depth = w_in.shape[0]
    plans = _plans()
    layers = [_layer_weights(l, prm, plans) for l in range(depth)]
    fnorm = final_norm[None, :]

    t_p = x_prompt.shape[1]
    assert t_p % CHUNK == 0
    tq_p = min(ATTN_ROWS, t_p)
    out_p = _trunk(x_prompt, None, jnp.arange(t_p, dtype=jnp.int32), layers, plans, fnorm, tq_p, tq_p, 1)


    b_s, t_s, _ = x_sample.shape
    past = cache_mla.shape[2]
    assert past % CHUNK == 0 and t_s <= CHUNK, "new sample frames must sit in one chunk after whole cached chunks"
    tk_s = math.gcd(past, ATTN_ROWS)
    fold_heads = lambda c: c.reshape(c.shape[:2] + (c.shape[2] * c.shape[3], c.shape[4]))
    caches = (cache_mla, cache_dsa, fold_heads(cache_sb), fold_heads(cache_diff))
    fold = math.gcd(b_s, max(1, PROJ_ROWS // t_s))
    out_s = _trunk(x_sample, caches, past + jnp.arange(t_s, dtype=jnp.int32), layers, plans, fnorm, t_s, tk_s, fold)

    return (out_p[0], out_s[0]) + tuple(out_p[1:]) + tuple(out_s[1:])
```
